```python
import jax, jax.numpy as jnp
from jax import lax
import numpy as np

D_MODEL = 1024
BATCH = 2
SEQ = 8192
DEPTH = 1

GRID_W = 64
N_HEADS = 8
HEAD_DIM = 64
ATTN_WIDTH = N_HEADS * HEAD_DIM
CONV_WIDTH = D_MODEL // 2
CONV_K = 3
WIN_ROWS_MAX = 8
WIN_COLS = 16
Q_COL_BLOCK = 16
KEY_COL_SPAN = 32
D_FF = 4 * D_MODEL
EPS = 1e-6
NEG_INF = -1e30
PROJ_SPLITS = [ATTN_WIDTH, ATTN_WIDTH, ATTN_WIDTH, CONV_WIDTH, CONV_WIDTH, CONV_WIDTH, D_MODEL]
PROJ_WIDTH = 3 * ATTN_WIDTH + 3 * CONV_WIDTH + 2 * D_MODEL

kernel_name = "hybrid_natten2d_shortconv_gated_encoder"


def rms_norm(x, g):
    xf = x.astype(jnp.float32)
    xf = xf * lax.rsqrt(jnp.mean(xf * xf, axis=-1, keepdims=True) + EPS)
    return xf.astype(x.dtype) * g


def _na_indices(rows):
    kr = min(WIN_ROWS_MAX, rows)
    r = np.arange(rows)
    row_start = np.clip(r - kr // 2, 0, rows - kr)
    key_rows = row_start[:, None] + np.arange(kr)[None, :]
    rel_r = key_rows - r[:, None] + WIN_ROWS_MAX - 1
    ncb = GRID_W // Q_COL_BLOCK
    j = np.arange(ncb)
    cb_start = np.clip(j * Q_COL_BLOCK - WIN_COLS // 2, 0, GRID_W - KEY_COL_SPAN)
    key_cols = cb_start[:, None] + np.arange(KEY_COL_SPAN)[None, :]
    q_cols = j[:, None] * Q_COL_BLOCK + np.arange(Q_COL_BLOCK)[None, :]
    win_start = np.clip(q_cols - WIN_COLS // 2, 0, GRID_W - WIN_COLS)
    kc = key_cols[:, None, :]
    col_mask = (kc >= win_start[..., None]) & (kc < win_start[..., None] + WIN_COLS)
    rel_c = np.clip(kc - q_cols[..., None] + WIN_COLS - 1, 0, 2 * WIN_COLS - 2)
    key_tok = key_rows[:, None, :, None] * GRID_W + key_cols[None, :, None, :]
    key_tok = key_tok.reshape(rows, ncb, kr * KEY_COL_SPAN).astype(np.int32)
    mask = np.broadcast_to(col_mask[:, :, None, :], (ncb, Q_COL_BLOCK, kr, KEY_COL_SPAN))
    mask = mask.reshape(ncb, Q_COL_BLOCK, kr * KEY_COL_SPAN)
    return kr, key_tok, rel_r.astype(np.int32), rel_c.astype(np.int32), mask


def neighbourhood_attention(q, k, v, rpb):
    B, S, H, dh = q.shape
    rows = S // GRID_W
    kr, key_tok, rel_r, rel_c, mask = _na_indices(rows)
    ncb = GRID_W // Q_COL_BLOCK
    qb = q.reshape(B, rows, ncb, Q_COL_BLOCK, H, dh)
    kb = k[:, key_tok]
    vb = v[:, key_tok]
    s = jnp.einsum('brjqhd,brjkhd->bhrjqk', qb, kb).astype(jnp.float32) * (dh ** -0.5)
    bias = rpb[:, rel_r[:, None, None, :, None], rel_c[None, :, :, None, :]]
    bias = bias.reshape(H, rows, ncb, Q_COL_BLOCK, kr * KEY_COL_SPAN).astype(jnp.float32)
    s = jnp.where(jnp.asarray(mask), s + bias[None], NEG_INF)
    p = jax.nn.softmax(s, axis=-1).astype(v.dtype)
    o = jnp.einsum('bhrjqk,brjkhd->brjqhd', p, vb)
    return o.reshape(B, S, H * dh)


def short_gated_conv(cb, cc, ch, w):
    z = cc * ch
    zp = jnp.pad(z, ((0, 0), (1, 1), (0, 0)))
    conv = zp[:, :-2] * w[0] + zp[:, 1:-1] * w[1] + zp[:, 2:] * w[2]
    return cb * conv


def setup_inputs(seed: int = 0) -> dict:
    key = jax.random.key(seed)
    ks = jax.random.split(key, 16)
    f32 = jnp.float32

    def nrm(k, shape, scale):
        return jax.random.normal(k, shape, f32) * scale

    return {
        "x": nrm(ks[0], (BATCH, SEQ, D_MODEL), 1.0),
        "norm1_g": 1.0 + nrm(ks[1], (DEPTH, D_MODEL), 0.02),
        "w_in": nrm(ks[2], (DEPTH, D_MODEL, PROJ_WIDTH), D_MODEL ** -0.5),
        "q_norm_g": 1.0 + nrm(ks[3], (DEPTH, HEAD_DIM), 0.02),
        "k_norm_g": 1.0 + nrm(ks[4], (DEPTH, HEAD_DIM), 0.02),
        "rpb": nrm(ks[5], (DEPTH, N_HEADS, 2 * WIN_ROWS_MAX - 1, 2 * WIN_COLS - 1), 0.5),
        "conv_w": nrm(ks[6], (DEPTH, CONV_K, CONV_WIDTH), CONV_K ** -0.5),
        "w_attn_branch": nrm(ks[7], (DEPTH, ATTN_WIDTH, D_MODEL), ATTN_WIDTH ** -0.5),
        "w_conv_branch": nrm(ks[8], (DEPTH, CONV_WIDTH, D_MODEL), CONV_WIDTH ** -0.5),
        "w_o": nrm(ks[9], (DEPTH, D_MODEL, D_MODEL), D_MODEL ** -0.5),
        "norm2_g": 1.0 + nrm(ks[10], (DEPTH, D_MODEL), 0.02),
        "w_mlp_in": nrm(ks[11], (DEPTH, D_MODEL, D_FF), D_MODEL ** -0.5),
        "w_mlp_out": nrm(ks[12], (DEPTH, D_FF, D_MODEL), D_FF ** -0.5),
    }


def reference(x, norm1_g, w_in, q_norm_g, k_norm_g, rpb, conv_w, w_attn_branch,
              w_conv_branch, w_o, norm2_g, w_mlp_in, w_mlp_out):
    B, S, _ = x.shape
    split_pts = [int(p) for p in np.cumsum(PROJ_SPLITS)]
    for l in range(DEPTH):
        u = rms_norm(x, norm1_g[l])
        proj = u @ w_in[l]
        q, k, v, cb, cc, ch, ga, gb = jnp.split(proj, split_pts, axis=-1)
        q = rms_norm(q.reshape(B, S, N_HEADS, HEAD_DIM), q_norm_g[l])
        k = rms_norm(k.reshape(B, S, N_HEADS, HEAD_DIM), k_norm_g[l])
        v = v.reshape(B, S, N_HEADS, HEAD_DIM)
        y_a = neighbourhood_attention(q, k, v, rpb[l]) @ w_attn_branch[l]
        y_b = short_gated_conv(cb, cc, ch, conv_w[l]) @ w_conv_branch[l]
        merged = jax.nn.sigmoid(ga) * y_a + jax.nn.sigmoid(gb) * y_b
        x = x + merged @ w_o[l]
        h = rms_norm(x, norm2_g[l]) @ w_mlp_in[l]
        x = x + jnp.square(jax.nn.relu(h)) @ w_mlp_out[l]
    return x
```

```python
import functools

import numpy as np
import jax
import jax.numpy as jnp
from jax import lax
from jax.experimental import pallas as pl
from jax.experimental.pallas import tpu as pltpu

F32 = jnp.float32
BF16 = jnp.bfloat16

D_MODEL = 1024
GRID_W = 64
N_HEADS = 8
HEAD_DIM = 64
ATTN_WIDTH = N_HEADS * HEAD_DIM
CONV_WIDTH = D_MODEL // 2
WIN_ROWS = 8
WIN_COLS = 16
D_FF = 4 * D_MODEL
EPS = 1e-6
NEG_INF = -1e30
PROJ_WIDTH = 3 * ATTN_WIDTH + 3 * CONV_WIDTH + 2 * D_MODEL

LANES = 128
HEAD_PAIRS = ATTN_WIDTH // LANES
N_REL_ROWS = 2 * WIN_ROWS - 1
N_REL_COLS = 2 * WIN_COLS - 1

TOK_TILE = 512
ROWS_PER_BLOCK = 8
MIX_TILE = ROWS_PER_BLOCK * GRID_W
FF_CHUNK = 512
HALO_ROWS = 16
VMEM_LIMIT = 56 * 1024 * 1024


def _const_spec(shape):
    zeros = (0,) * len(shape)
    return pl.BlockSpec(shape, lambda *_: zeros)


def _inproj_kernel(x_ref, g1_ref, w_ref, gq_ref, gk_ref, bd_ref,
                   qe_ref, qo_ref, k_ref, v_ref, cb_ref, z_ref, sga_ref, sgb_ref):
    xf = x_ref[...]
    ms = jnp.mean(xf * xf, axis=-1, keepdims=True)
    u = ((xf * lax.rsqrt(ms + EPS)) * g1_ref[...]).astype(BF16)

    def proj(c0, width):
        return jnp.dot(u, w_ref[:, c0:c0 + width], preferred_element_type=F32)

    def head_norm(t, g):
        t2 = t * t
        hi = t2.astype(BF16)
        lo = (t2 - hi.astype(F32)).astype(BF16)
        ss = (jnp.dot(hi, bd_ref[...], preferred_element_type=F32)
              + jnp.dot(lo, bd_ref[...], preferred_element_type=F32))
        return (t * lax.rsqrt(ss * (1.0 / HEAD_DIM) + EPS)) * g

    lane = lax.broadcasted_iota(jnp.int32, (TOK_TILE, ATTN_WIDTH), 1)
    even_head = (lane % LANES) < HEAD_DIM

    q = head_norm(proj(0, ATTN_WIDTH), gq_ref[...]) * (HEAD_DIM ** -0.5)
    qe_ref[...] = jnp.where(even_head, q, 0.0).astype(BF16)
    qo_ref[...] = jnp.where(even_head, 0.0, q).astype(BF16)
    k_ref[...] = head_norm(proj(ATTN_WIDTH, ATTN_WIDTH), gk_ref[...]).astype(BF16)
    v_ref[...] = proj(2 * ATTN_WIDTH, ATTN_WIDTH).astype(BF16)
    c0 = 3 * ATTN_WIDTH
    cb_ref[...] = proj(c0, CONV_WIDTH).astype(BF16)
    z_ref[...] = (proj(c0 + CONV_WIDTH, CONV_WIDTH)
                  * proj(c0 + 2 * CONV_WIDTH, CONV_WIDTH)).astype(BF16)
    g0 = c0 + 3 * CONV_WIDTH
    sga_ref[...] = jax.nn.sigmoid(proj(g0, D_MODEL)).astype(BF16)
    sgb_ref[...] = jax.nn.sigmoid(proj(g0 + D_MODEL, D_MODEL)).astype(BF16)


def _inproj(x2d, g1, w_in, gq, gk, bd):
    m = x2d.shape[0]
    tok = lambda w: pl.BlockSpec((TOK_TILE, w), lambda i: (i, 0))
    out_widths = [ATTN_WIDTH] * 4 + [CONV_WIDTH] * 2 + [D_MODEL] * 2
    return pl.pallas_call(
        _inproj_kernel,
        grid=(m // TOK_TILE,),
        in_specs=[tok(D_MODEL), _const_spec((1, D_MODEL)), _const_spec((D_MODEL, PROJ_WIDTH)),
                  _const_spec((1, ATTN_WIDTH)), _const_spec((1, ATTN_WIDTH)),
                  _const_spec((ATTN_WIDTH, ATTN_WIDTH))],
        out_specs=[tok(w) for w in out_widths],
        out_shape=[jax.ShapeDtypeStruct((m, w), BF16) for w in out_widths],
        compiler_params=pltpu.CompilerParams(
            dimension_semantics=("arbitrary",), vmem_limit_bytes=VMEM_LIMIT),
        name="inproj",
    )(x2d, g1, w_in, gq, gk, bd)


def _mixer_kernel(rows_per_batch,
                  qe_ref, qo_ref, kp_ref, kc_ref, kn_ref, vp_ref, vc_ref, vn_ref,
                  cb_ref, z_ref, zp_ref, zn_ref, sga_ref, sgb_ref, x_ref,
                  tp_ref, cw_ref, wa_ref, wb_ref, wo_ref,
                  o_ref, kbuf, vbuf, attn_buf):
    blk = pl.program_id(1)
    n_blk = pl.num_programs(1)
    kbuf[0:MIX_TILE] = kp_ref[...]
    kbuf[MIX_TILE:2 * MIX_TILE] = kc_ref[...]
    kbuf[2 * MIX_TILE:3 * MIX_TILE] = kn_ref[...]
    vbuf[0:MIX_TILE] = vp_ref[...]
    vbuf[MIX_TILE:2 * MIX_TILE] = vc_ref[...]
    vbuf[2 * MIX_TILE:3 * MIX_TILE] = vn_ref[...]

    lane = lax.broadcasted_iota(jnp.int32, (GRID_W, LANES), 1)
    first_head = lane < HEAD_DIM
    n_keys = WIN_ROWS * GRID_W

    def row_body(j, carry):
        r = blk * ROWS_PER_BLOCK + j
        row_start = jnp.clip(r - WIN_ROWS // 2, 0, rows_per_batch - WIN_ROWS)
        d = r - row_start
        ks = pl.multiple_of((row_start - (blk - 1) * ROWS_PER_BLOCK) * GRID_W, GRID_W)
        qs = pl.multiple_of(j * GRID_W, GRID_W)
        for p in range(HEAD_PAIRS):
            cols = slice(p * LANES, (p + 1) * LANES)
            k_p = kbuf[pl.ds(ks, n_keys), cols]
            v_p = vbuf[pl.ds(ks, n_keys), cols]
            outs = []
            for hh, q_ref in enumerate((qe_ref, qo_ref)):
                h = 2 * p + hh
                q = q_ref[pl.ds(qs, GRID_W), cols]
                s = lax.dot_general(q, k_p, (((1,), (1,)), ((), ())),
                                    preferred_element_type=F32)
                bias = jnp.concatenate(
                    [tp_ref[h, WIN_ROWS - 1 - d + i] for i in range(0, WIN_ROWS, 2)], axis=-1)
                s = s + bias
                m = jnp.max(s, axis=-1, keepdims=True)
                e = jnp.exp(s - m)
                l = jnp.sum(e, axis=-1, keepdims=True)
                o = jnp.dot(e.astype(BF16), v_p, preferred_element_type=F32)
                outs.append(o / l)
            attn_buf[pl.ds(qs, GRID_W), cols] = jnp.where(first_head, outs[0], outs[1]).astype(BF16)
        return carry

    lax.fori_loop(0, ROWS_PER_BLOCK, row_body, 0)

    ya = jnp.dot(attn_buf[...], wa_ref[...], preferred_element_type=F32)

    zf = z_ref[...].astype(F32)
    row = lax.broadcasted_iota(jnp.int32, (MIX_TILE, CONV_WIDTH), 0)
    z_before = jnp.where(blk > 0, zp_ref[HALO_ROWS - 1:HALO_ROWS, :].astype(F32), 0.0)
    z_after = jnp.where(blk < n_blk - 1, zn_ref[0:1, :].astype(F32), 0.0)
    z_m1 = jnp.where(row == 0, z_before, pltpu.roll(zf, 1, 0))
    z_p1 = jnp.where(row == MIX_TILE - 1, z_after, pltpu.roll(zf, MIX_TILE - 1, 0))
    conv = z_m1 * cw_ref[0:1, :] + zf * cw_ref[1:2, :] + z_p1 * cw_ref[2:3, :]
    bb = (cb_ref[...].astype(F32) * conv).astype(BF16)
    yb = jnp.dot(bb, wb_ref[...], preferred_element_type=F32)

    merged = (sga_ref[...].astype(F32) * ya + sgb_ref[...].astype(F32) * yb).astype(BF16)
    o_ref[...] = x_ref[...] + jnp.dot(merged, wo_ref[...], preferred_element_type=F32)


def _mixer(batch, seq, qe, qo, k, v, cb, z, sga, sgb, x, tp, cw, wa, wb, wo):
    rows = seq // GRID_W
    n_blk = rows // ROWS_PER_BLOCK
    halo_per_blk = MIX_TILE // HALO_ROWS
    n_halo = seq // HALO_ROWS

    def r3(a):
        return a.reshape(batch, seq, a.shape[-1])

    cur = lambda w: pl.BlockSpec((None, MIX_TILE, w), lambda b, i: (b, i, 0))
    prev = lambda w: pl.BlockSpec((None, MIX_TILE, w), lambda b, i: (b, jnp.maximum(i - 1, 0), 0))
    nxt = lambda w: pl.BlockSpec((None, MIX_TILE, w),
                                 lambda b, i: (b, jnp.minimum(i + 1, n_blk - 1), 0))
    halo_prev = pl.BlockSpec((None, HALO_ROWS, CONV_WIDTH),
                             lambda b, i: (b, jnp.maximum(i * halo_per_blk - 1, 0), 0))
    halo_next = pl.BlockSpec((None, HALO_ROWS, CONV_WIDTH),
                             lambda b, i: (b, jnp.minimum((i + 1) * halo_per_blk, n_halo - 1), 0))
    aw, cwid = ATTN_WIDTH, CONV_WIDTH
    in_specs = [cur(aw), cur(aw), prev(aw), cur(aw), nxt(aw), prev(aw), cur(aw), nxt(aw),
                cur(cwid), cur(cwid), halo_prev, halo_next, cur(D_MODEL), cur(D_MODEL), cur(D_MODEL),
                _const_spec(tp.shape), _const_spec(cw.shape), _const_spec(wa.shape),
                _const_spec(wb.shape), _const_spec(wo.shape)]
    k3, v3, z3 = r3(k), r3(v), r3(z)
    out = pl.pallas_call(
        functools.partial(_mixer_kernel, rows),
        grid=(batch, n_blk),
        in_specs=in_specs,
        out_specs=cur(D_MODEL),
        out_shape=jax.ShapeDtypeStruct((batch, seq, D_MODEL), F32),
        scratch_shapes=[pltpu.VMEM((3 * MIX_TILE, ATTN_WIDTH), BF16),
                        pltpu.VMEM((3 * MIX_TILE, ATTN_WIDTH), BF16),
                        pltpu.VMEM((MIX_TILE, ATTN_WIDTH), BF16)],
        compiler_params=pltpu.CompilerParams(
            dimension_semantics=("arbitrary", "arbitrary"), vmem_limit_bytes=VMEM_LIMIT),
        name="mixer",
    )(r3(qe), r3(qo), k3, k3, k3, v3, v3, v3, r3(cb), z3, z3, z3, r3(sga), r3(sgb), x,
      tp, cw, wa, wb, wo)
    return out


def _bias_table(rpb):
    c = np.arange(GRID_W)
    win_start = np.clip(c - WIN_COLS // 2, 0, GRID_W - WIN_COLS)
    kc = np.arange(GRID_W)
    mask = (kc[None, :] >= win_start[:, None]) & (kc[None, :] < win_start[:, None] + WIN_COLS)
    rel_c = np.clip(kc[None, :] - c[:, None] + WIN_COLS - 1, 0, N_REL_COLS - 1)
    blocks = jnp.where(jnp.asarray(mask)[None, None], rpb[:, :, rel_c], NEG_INF)
    return jnp.concatenate([blocks[:, :-1], blocks[:, 1:]], axis=-1).astype(F32)


def _mlp_kernel(x_ref, g_ref, w1_ref, w2_ref, o_ref, h_buf):
    xf = x_ref[...]
    ms = jnp.mean(xf * xf, axis=-1, keepdims=True)
    u = ((xf * lax.rsqrt(ms + EPS)) * g_ref[...]).astype(BF16)
    for c in range(D_FF // FF_CHUNK):
        cols = slice(c * FF_CHUNK, (c + 1) * FF_CHUNK)
        h = jnp.dot(u, w1_ref[:, cols], preferred_element_type=F32)
        h_buf[:, cols] = jnp.square(jnp.maximum(h, 0.0)).astype(BF16)
    o_ref[...] = xf + jnp.dot(h_buf[...], w2_ref[...], preferred_element_type=F32)


def _mlp(x2d, g2, w1, w2):
    m = x2d.shape[0]
    tok = pl.BlockSpec((TOK_TILE, D_MODEL), lambda i: (i, 0))
    return pl.pallas_call(
        _mlp_kernel,
        grid=(m // TOK_TILE,),
        in_specs=[tok, _const_spec((1, D_MODEL)), _const_spec(w1.shape), _const_spec(w2.shape)],
        out_specs=tok,
        out_shape=jax.ShapeDtypeStruct((m, D_MODEL), F32),
        scratch_shapes=[pltpu.VMEM((TOK_TILE, D_FF), BF16)],
        compiler_params=pltpu.CompilerParams(
            dimension_semantics=("arbitrary",), vmem_limit_bytes=VMEM_LIMIT),
        name="mlp",
    )(x2d, g2, w1, w2)


def kernel(x, norm1_g, w_in, q_norm_g, k_norm_g, rpb, conv_w, w_attn_branch, w_conv_branch,
           w_o, norm2_g, w_mlp_in, w_mlp_out):
    batch, seq, _ = x.shape
    depth = w_in.shape[0]
    head_of_lane = np.arange(ATTN_WIDTH) // HEAD_DIM
    bd = jnp.asarray(head_of_lane[:, None] == head_of_lane[None, :], dtype=BF16)
    for l in range(depth):
        x2d = x.reshape(batch * seq, D_MODEL)
        gq = jnp.tile(q_norm_g[l], N_HEADS)[None, :]
        gk = jnp.tile(k_norm_g[l], N_HEADS)[None, :]
        qe, qo, k, v, cb, z, sga, sgb = _inproj(
            x2d, norm1_g[l][None, :], w_in[l].astype(BF16), gq, gk, bd)
        x = _mixer(batch, seq, qe, qo, k, v, cb, z, sga, sgb, x,
                   _bias_table(rpb[l]), conv_w[l],
                   w_attn_branch[l].astype(BF16), w_conv_branch[l].astype(BF16),
                   w_o[l].astype(BF16))
        x = _mlp(x.reshape(batch * seq, D_MODEL), norm2_g[l][None, :],
                 w_mlp_in[l].astype(BF16), w_mlp_out[l].astype(BF16)).reshape(batch, seq, D_MODEL)
    return x
```

```python
import functools

import numpy as np
import jax
import jax.numpy as jnp
from jax import lax
from jax.experimental import pallas as pl
from jax.experimental.pallas import tpu as pltpu

F32 = jnp.float32
BF16 = jnp.bfloat16

D_MODEL = 1024
GRID_W = 64
N_HEADS = 8
HEAD_DIM = 64
ATTN_WIDTH = N_HEADS * HEAD_DIM
CONV_WIDTH = D_MODEL // 2
WIN_ROWS = 8
WIN_COLS = 16
D_FF = 4 * D_MODEL
EPS = 1e-6
NEG_INF = -1e30
PROJ_WIDTH = 3 * ATTN_WIDTH + 3 * CONV_WIDTH + 2 * D_MODEL

LANES = 128
HEAD_PAIRS = ATTN_WIDTH // LANES
N_REL_ROWS = 2 * WIN_ROWS - 1
N_REL_COLS = 2 * WIN_COLS - 1

TOK_TILE = 512
ROWS_PER_BLOCK = 8
MIX_TILE = ROWS_PER_BLOCK * GRID_W
FF_CHUNK = 512
HALO_ROWS = 16
VMEM_LIMIT = 56 * 1024 * 1024


def _const_spec(shape):
    zeros = (0,) * len(shape)
    return pl.BlockSpec(shape, lambda *_: zeros)


def _inproj_kernel(x_ref, g1_ref, w_ref, gq_ref, gk_ref, bd_ref,
                   qe_ref, qo_ref, k_ref, v_ref, cb_ref, z_ref, sga_ref, sgb_ref):
    xf = x_ref[...]
    ms = jnp.mean(xf * xf, axis=-1, keepdims=True)
    u = ((xf * lax.rsqrt(ms + EPS)) * g1_ref[...]).astype(BF16)

    def proj(c0, width):
        return jnp.dot(u, w_ref[:, c0:c0 + width], preferred_element_type=F32)

    def head_norm(t, g):
        t2 = t * t
        hi = t2.astype(BF16)
        lo = (t2 - hi.astype(F32)).astype(BF16)
        ss = (jnp.dot(hi, bd_ref[...], preferred_element_type=F32)
              + jnp.dot(lo, bd_ref[...], preferred_element_type=F32))
        return (t * lax.rsqrt(ss * (1.0 / HEAD_DIM) + EPS)) * g

    lane = lax.broadcasted_iota(jnp.int32, (TOK_TILE, ATTN_WIDTH), 1)
    even_head = (lane % LANES) < HEAD_DIM

    q = head_norm(proj(0, ATTN_WIDTH), gq_ref[...]) * (HEAD_DIM ** -0.5)
    qe_ref[...] = jnp.where(even_head, q, 0.0).astype(BF16)
    qo_ref[...] = jnp.where(even_head, 0.0, q).astype(BF16)
    k_ref[...] = head_norm(proj(ATTN_WIDTH, ATTN_WIDTH), gk_ref[...]).astype(BF16)
    v_ref[...] = proj(2 * ATTN_WIDTH, ATTN_WIDTH).astype(BF16)
    c0 = 3 * ATTN_WIDTH
    cb_ref[...] = proj(c0, CONV_WIDTH).astype(BF16)
    z_ref[...] = (proj(c0 + CONV_WIDTH, CONV_WIDTH)
                  * proj(c0 + 2 * CONV_WIDTH, CONV_WIDTH)).astype(BF16)
    g0 = c0 + 3 * CONV_WIDTH
    sga_ref[...] = jax.nn.sigmoid(proj(g0, D_MODEL)).astype(BF16)
    sgb_ref[...] = jax.nn.sigmoid(proj(g0 + D_MODEL, D_MODEL)).astype(BF16)


def _inproj(x2d, g1, w_in, gq, gk, bd):
    m = x2d.shape[0]
    tok = lambda w: pl.BlockSpec((TOK_TILE, w), lambda i: (i, 0))
    out_widths = [ATTN_WIDTH] * 4 + [CONV_WIDTH] * 2 + [D_MODEL] * 2
    return pl.pallas_call(
        _inproj_kernel,
        grid=(m // TOK_TILE,),
        in_specs=[tok(D_MODEL), _const_spec((1, D_MODEL)), _const_spec((D_MODEL, PROJ_WIDTH)),
                  _const_spec((1, ATTN_WIDTH)), _const_spec((1, ATTN_WIDTH)),
                  _const_spec((ATTN_WIDTH, ATTN_WIDTH))],
        out_specs=[tok(w) for w in out_widths],
        out_shape=[jax.ShapeDtypeStruct((m, w), BF16) for w in out_widths],
        compiler_params=pltpu.CompilerParams(
            dimension_semantics=("arbitrary",), vmem_limit_bytes=VMEM_LIMIT),
        name="inproj",
    )(x2d, g1, w_in, gq, gk, bd)


def _mixer_kernel(rows_per_batch,
                  qe_ref, qo_ref, kp_ref, kc_ref, kn_ref, vp_ref, vc_ref, vn_ref,
                  cb_ref, z_ref, zp_ref, zn_ref, sga_ref, sgb_ref, x_ref,
                  tp_ref, cw_ref, wa_ref, wb_ref, wo_ref,
                  o_ref, kbuf, vbuf, attn_buf, s_buf):
    blk = pl.program_id(1)
    n_blk = pl.num_programs(1)
    kbuf[0:MIX_TILE] = kp_ref[...]
    kbuf[MIX_TILE:2 * MIX_TILE] = kc_ref[...]
    kbuf[2 * MIX_TILE:3 * MIX_TILE] = kn_ref[...]
    vbuf[0:MIX_TILE] = vp_ref[...]
    vbuf[MIX_TILE:2 * MIX_TILE] = vc_ref[...]
    vbuf[2 * MIX_TILE:3 * MIX_TILE] = vn_ref[...]

    lane = lax.broadcasted_iota(jnp.int32, (GRID_W, LANES), 1)
    first_head = lane < HEAD_DIM
    n_keys = WIN_ROWS * GRID_W

    def window(j):
        r = blk * ROWS_PER_BLOCK + j
        row_start = jnp.clip(r - WIN_ROWS // 2, 0, rows_per_batch - WIN_ROWS)
        ks = pl.multiple_of((row_start - (blk - 1) * ROWS_PER_BLOCK) * GRID_W, GRID_W)
        return ks, r - row_start

    def scores(j, p):
        ks, d = window(j)
        cols = slice(p * LANES, (p + 1) * LANES)
        rows = slice(j * GRID_W, (j + 1) * GRID_W)
        q2 = jnp.concatenate([qe_ref[rows, cols], qo_ref[rows, cols]], axis=0)
        s = lax.dot_general(q2, kbuf[pl.ds(ks, n_keys), cols], (((1,), (1,)), ((), ())),
                            preferred_element_type=F32)
        bias = jnp.concatenate(
            [tp_ref[p, WIN_ROWS - 1 - d + i] for i in range(0, WIN_ROWS, 2)], axis=-1)
        s_buf[j % 2, p] = s + bias

    def attend(j, p):
        ks, _ = window(j)
        cols = slice(p * LANES, (p + 1) * LANES)
        rows = slice(j * GRID_W, (j + 1) * GRID_W)
        s = s_buf[j % 2, p]
        m = jnp.max(s, axis=-1, keepdims=True)
        e = jnp.exp(s - m)
        l = jnp.sum(e, axis=-1, keepdims=True)
        o = jnp.dot(e.astype(BF16), vbuf[pl.ds(ks, n_keys), cols], preferred_element_type=F32)
        o = o / l
        attn_buf[rows, cols] = jnp.where(first_head, o[:GRID_W], o[GRID_W:]).astype(BF16)

    for p in range(HEAD_PAIRS):
        scores(0, p)
    for j in range(ROWS_PER_BLOCK):
        for p in range(HEAD_PAIRS):
            if j + 1 < ROWS_PER_BLOCK:
                scores(j + 1, p)
            attend(j, p)

    ya = jnp.dot(attn_buf[...], wa_ref[...], preferred_element_type=F32)

    zf = z_ref[...].astype(F32)
    row = lax.broadcasted_iota(jnp.int32, (MIX_TILE, CONV_WIDTH), 0)
    z_before = jnp.where(blk > 0, zp_ref[HALO_ROWS - 1:HALO_ROWS, :].astype(F32), 0.0)
    z_after = jnp.where(blk < n_blk - 1, zn_ref[0:1, :].astype(F32), 0.0)
    z_m1 = jnp.where(row == 0, z_before, pltpu.roll(zf, 1, 0))
    z_p1 = jnp.where(row == MIX_TILE - 1, z_after, pltpu.roll(zf, MIX_TILE - 1, 0))
    conv = z_m1 * cw_ref[0:1, :] + zf * cw_ref[1:2, :] + z_p1 * cw_ref[2:3, :]
    bb = (cb_ref[...].astype(F32) * conv).astype(BF16)
    yb = jnp.dot(bb, wb_ref[...], preferred_element_type=F32)

    merged = (sga_ref[...].astype(F32) * ya + sgb_ref[...].astype(F32) * yb).astype(BF16)
    o_ref[...] = x_ref[...] + jnp.dot(merged, wo_ref[...], preferred_element_type=F32)


def _mixer(batch, seq, qe, qo, k, v, cb, z, sga, sgb, x, tp, cw, wa, wb, wo):
    rows = seq // GRID_W
    n_blk = rows // ROWS_PER_BLOCK
    halo_per_blk = MIX_TILE // HALO_ROWS
    n_halo = seq // HALO_ROWS

    def r3(a):
        return a.reshape(batch, seq, a.shape[-1])

    cur = lambda w: pl.BlockSpec((None, MIX_TILE, w), lambda b, i: (b, i, 0))
    prev = lambda w: pl.BlockSpec((None, MIX_TILE, w), lambda b, i: (b, jnp.maximum(i - 1, 0), 0))
    nxt = lambda w: pl.BlockSpec((None, MIX_TILE, w),
                                 lambda b, i: (b, jnp.minimum(i + 1, n_blk - 1), 0))
    halo_prev = pl.BlockSpec((None, HALO_ROWS, CONV_WIDTH),
                             lambda b, i: (b, jnp.maximum(i * halo_per_blk - 1, 0), 0))
    halo_next = pl.BlockSpec((None, HALO_ROWS, CONV_WIDTH),
                             lambda b, i: (b, jnp.minimum((i + 1) * halo_per_blk, n_halo - 1), 0))
    aw, cwid = ATTN_WIDTH, CONV_WIDTH
    in_specs = [cur(aw), cur(aw), prev(aw), cur(aw), nxt(aw), prev(aw), cur(aw), nxt(aw),
                cur(cwid), cur(cwid), halo_prev, halo_next, cur(D_MODEL), cur(D_MODEL), cur(D_MODEL),
                _const_spec(tp.shape), _const_spec(cw.shape), _const_spec(wa.shape),
                _const_spec(wb.shape), _const_spec(wo.shape)]
    k3, v3, z3 = r3(k), r3(v), r3(z)
    out = pl.pallas_call(
        functools.partial(_mixer_kernel, rows),
        grid=(batch, n_blk),
        in_specs=in_specs,
        out_specs=cur(D_MODEL),
        out_shape=jax.ShapeDtypeStruct((batch, seq, D_MODEL), F32),
        scratch_shapes=[pltpu.VMEM((3 * MIX_TILE, ATTN_WIDTH), BF16),
                        pltpu.VMEM((3 * MIX_TILE, ATTN_WIDTH), BF16),
                        pltpu.VMEM((MIX_TILE, ATTN_WIDTH), BF16),
                        pltpu.VMEM((2, HEAD_PAIRS, 2 * GRID_W, WIN_ROWS * GRID_W), F32)],
        compiler_params=pltpu.CompilerParams(
            dimension_semantics=("arbitrary", "arbitrary"), vmem_limit_bytes=VMEM_LIMIT),
        name="mixer",
    )(r3(qe), r3(qo), k3, k3, k3, v3, v3, v3, r3(cb), z3, z3, z3, r3(sga), r3(sgb), x,
      tp, cw, wa, wb, wo)
    return out


def _bias_table(rpb):
    c = np.arange(GRID_W)
    win_start = np.clip(c - WIN_COLS // 2, 0, GRID_W - WIN_COLS)
    kc = np.arange(GRID_W)
    mask = (kc[None, :] >= win_start[:, None]) & (kc[None, :] < win_start[:, None] + WIN_COLS)
    rel_c = np.clip(kc[None, :] - c[:, None] + WIN_COLS - 1, 0, N_REL_COLS - 1)
    blocks = jnp.where(jnp.asarray(mask)[None, None], rpb[:, :, rel_c], NEG_INF)
    tbl = jnp.concatenate([blocks[:, :-1], blocks[:, 1:]], axis=-1).astype(F32)
    tbl = tbl.reshape(HEAD_PAIRS, 2, N_REL_ROWS - 1, GRID_W, LANES)
    return tbl.transpose(0, 2, 1, 3, 4).reshape(HEAD_PAIRS, N_REL_ROWS - 1, 2 * GRID_W, LANES)


def _mlp_kernel(x_ref, g_ref, w1_ref, w2_ref, o_ref, h_buf):
    xf = x_ref[...]
    ms = jnp.mean(xf * xf, axis=-1, keepdims=True)
    u = ((xf * lax.rsqrt(ms + EPS)) * g_ref[...]).astype(BF16)
    for c in range(D_FF // FF_CHUNK):
        cols = slice(c * FF_CHUNK, (c + 1) * FF_CHUNK)
        h = jnp.dot(u, w1_ref[:, cols], preferred_element_type=F32)
        h_buf[:, cols] = jnp.square(jnp.maximum(h, 0.0)).astype(BF16)
    o_ref[...] = xf + jnp.dot(h_buf[...], w2_ref[...], preferred_element_type=F32)


def _mlp(x2d, g2, w1, w2):
    m = x2d.shape[0]
    tok = pl.BlockSpec((TOK_TILE, D_MODEL), lambda i: (i, 0))
    return pl.pallas_call(
        _mlp_kernel,
        grid=(m // TOK_TILE,),
        in_specs=[tok, _const_spec((1, D_MODEL)), _const_spec(w1.shape), _const_spec(w2.shape)],
        out_specs=tok,
        out_shape=jax.ShapeDtypeStruct((m, D_MODEL), F32),
        scratch_shapes=[pltpu.VMEM((TOK_TILE, D_FF), BF16)],
        compiler_params=pltpu.CompilerParams(
            dimension_semantics=("arbitrary",), vmem_limit_bytes=VMEM_LIMIT),
        name="mlp",
    )(x2d, g2, w1, w2)


def kernel(x, norm1_g, w_in, q_norm_g, k_norm_g, rpb, conv_w, w_attn_branch, w_conv_branch,
           w_o, norm2_g, w_mlp_in, w_mlp_out):
    batch, seq, _ = x.shape
    depth = w_in.shape[0]
    head_of_lane = np.arange(ATTN_WIDTH) // HEAD_DIM
    bd = jnp.asarray(head_of_lane[:, None] == head_of_lane[None, :], dtype=BF16)
    for l in range(depth):
        x2d = x.reshape(batch * seq, D_MODEL)
        gq = jnp.tile(q_norm_g[l], N_HEADS)[None, :]
        gk = jnp.tile(k_norm_g[l], N_HEADS)[None, :]
        qe, qo, k, v, cb, z, sga, sgb = _inproj(
            x2d, norm1_g[l][None, :], w_in[l].astype(BF16), gq, gk, bd)
        x = _mixer(batch, seq, qe, qo, k, v, cb, z, sga, sgb, x,
                   _bias_table(rpb[l]), conv_w[l],
                   w_attn_branch[l].astype(BF16), w_conv_branch[l].astype(BF16),
                   w_o[l].astype(BF16))
        x = _mlp(x.reshape(batch * seq, D_MODEL), norm2_g[l][None, :],
                 w_mlp_in[l].astype(BF16), w_mlp_out[l].astype(BF16)).reshape(batch, seq, D_MODEL)
    return x
```

```python
import functools

import numpy as np
import jax
import jax.numpy as jnp
from jax import lax
from jax.experimental import pallas as pl
from jax.experimental.pallas import tpu as pltpu

F32 = jnp.float32
BF16 = jnp.bfloat16

D_MODEL = 1024
GRID_W = 64
N_HEADS = 8
HEAD_DIM = 64
ATTN_WIDTH = N_HEADS * HEAD_DIM
CONV_WIDTH = D_MODEL // 2
WIN_ROWS = 8
WIN_COLS = 16
D_FF = 4 * D_MODEL
EPS = 1e-6
NEG_INF = -1e30
PROJ_WIDTH = 3 * ATTN_WIDTH + 3 * CONV_WIDTH + 2 * D_MODEL

LANES = 128
HEAD_PAIRS = ATTN_WIDTH // LANES
N_REL_ROWS = 2 * WIN_ROWS - 1
N_REL_COLS = 2 * WIN_COLS - 1

TOK_TILE = 512
ROWS_PER_BLOCK = 8
MIX_TILE = ROWS_PER_BLOCK * GRID_W
FF_CHUNK = 512
BF16_SUBLANES = 16
HALO_ROWS = BF16_SUBLANES
VMEM_LIMIT = 56 * 1024 * 1024


def _const_spec(shape):
    zeros = (0,) * len(shape)
    return pl.BlockSpec(shape, lambda *_: zeros)


def _inproj_kernel(n_cast, x_ref, g1_ref, w_ref, gq_ref, gk_ref, *refs):
    cast_in, refs = refs[:n_cast], refs[n_cast:]
    qe_ref, qo_ref, k_ref, v_ref, cb_ref, z_ref, sga_ref, sgb_ref = refs[:8]
    cast_out = refs[8:]
    for src, dst in zip(cast_in, cast_out):
        dst[...] = src[...].astype(BF16)

    xf = x_ref[...]
    ms = jnp.mean(xf * xf, axis=-1, keepdims=True)
    u = ((xf * lax.rsqrt(ms + EPS)) * g1_ref[...]).astype(BF16)

    def proj(c0, width):
        return jnp.dot(u, w_ref[:, c0:c0 + width], preferred_element_type=F32)

    first_head = lax.broadcasted_iota(jnp.int32, (TOK_TILE, LANES), 1) < HEAD_DIM

    def head_norm(t, g):
        outs = []
        for p in range(HEAD_PAIRS):
            tp = t[:, p * LANES:(p + 1) * LANES]
            t2 = tp * tp
            ss_a = jnp.sum(jnp.where(first_head, t2, 0.0), axis=-1, keepdims=True)
            ss_b = jnp.sum(jnp.where(first_head, 0.0, t2), axis=-1, keepdims=True)
            r_a = lax.rsqrt(ss_a * (1.0 / HEAD_DIM) + EPS)
            r_b = lax.rsqrt(ss_b * (1.0 / HEAD_DIM) + EPS)
            outs.append(tp * jnp.where(first_head, r_a, r_b))
        return jnp.concatenate(outs, axis=-1) * g

    lane = lax.broadcasted_iota(jnp.int32, (TOK_TILE, ATTN_WIDTH), 1)
    even_head = (lane % LANES) < HEAD_DIM

    q = head_norm(proj(0, ATTN_WIDTH), gq_ref[...]) * (HEAD_DIM ** -0.5)
    qe_ref[...] = jnp.where(even_head, q, 0.0).astype(BF16)
    qo_ref[...] = jnp.where(even_head, 0.0, q).astype(BF16)
    k_ref[...] = head_norm(proj(ATTN_WIDTH, ATTN_WIDTH), gk_ref[...]).astype(BF16)
    v_ref[...] = proj(2 * ATTN_WIDTH, ATTN_WIDTH).astype(BF16)
    c0 = 3 * ATTN_WIDTH
    cb_ref[...] = proj(c0, CONV_WIDTH).astype(BF16)
    z_ref[...] = (proj(c0 + CONV_WIDTH, CONV_WIDTH)
                  * proj(c0 + 2 * CONV_WIDTH, CONV_WIDTH)).astype(BF16)
    g0 = c0 + 3 * CONV_WIDTH
    sga_ref[...] = jax.nn.sigmoid(proj(g0, D_MODEL)).astype(BF16)
    sgb_ref[...] = jax.nn.sigmoid(proj(g0 + D_MODEL, D_MODEL)).astype(BF16)


def _inproj(x2d, g1, w_in, gq, gk, later_weights):
    m = x2d.shape[0]
    n_steps = m // TOK_TILE
    tok = lambda w: pl.BlockSpec((TOK_TILE, w), lambda i: (i, 0))
    out_widths = [ATTN_WIDTH] * 4 + [CONV_WIDTH] * 2 + [D_MODEL] * 2
    cast_specs = []
    for w in later_weights:
        rows = w.shape[0] // n_steps
        assert rows * n_steps == w.shape[0] and rows % BF16_SUBLANES == 0, w.shape
        cast_specs.append(pl.BlockSpec((rows, w.shape[1]), lambda i: (i, 0)))
    return pl.pallas_call(
        functools.partial(_inproj_kernel, len(later_weights)),
        grid=(n_steps,),
        in_specs=[tok(D_MODEL), _const_spec((1, D_MODEL)), _const_spec((D_MODEL, PROJ_WIDTH)),
                  _const_spec((1, ATTN_WIDTH)), _const_spec((1, ATTN_WIDTH))] + cast_specs,
        out_specs=[tok(w) for w in out_widths] + cast_specs,
        out_shape=([jax.ShapeDtypeStruct((m, w), BF16) for w in out_widths]
                   + [jax.ShapeDtypeStruct(w.shape, BF16) for w in later_weights]),
        compiler_params=pltpu.CompilerParams(
            dimension_semantics=("arbitrary",), vmem_limit_bytes=VMEM_LIMIT),
        name="inproj",
    )(x2d, g1, w_in, gq, gk, *later_weights)


def _mixer_kernel(rows_per_batch,
                  qe_ref, qo_ref, kp_ref, kc_ref, kn_ref, vp_ref, vc_ref, vn_ref,
                  cb_ref, z_ref, zp_ref, zn_ref, sga_ref, sgb_ref, x_ref,
                  tp_ref, cw_ref, wa_ref, wb_ref, wo_ref,
                  o_ref, kbuf, vbuf, attn_buf, s_buf):
    blk = pl.program_id(1)
    n_blk = pl.num_programs(1)
    kbuf[0:MIX_TILE] = kp_ref[...]
    kbuf[MIX_TILE:2 * MIX_TILE] = kc_ref[...]
    kbuf[2 * MIX_TILE:3 * MIX_TILE] = kn_ref[...]
    vbuf[0:MIX_TILE] = vp_ref[...]
    vbuf[MIX_TILE:2 * MIX_TILE] = vc_ref[...]
    vbuf[2 * MIX_TILE:3 * MIX_TILE] = vn_ref[...]

    lane = lax.broadcasted_iota(jnp.int32, (GRID_W, LANES), 1)
    first_head = lane < HEAD_DIM
    n_keys = WIN_ROWS * GRID_W

    def window(j):
        r = blk * ROWS_PER_BLOCK + j
        row_start = jnp.clip(r - WIN_ROWS // 2, 0, rows_per_batch - WIN_ROWS)
        ks = pl.multiple_of((row_start - (blk - 1) * ROWS_PER_BLOCK) * GRID_W, GRID_W)
        return ks, r - row_start

    def scores(j, p):
        ks, d = window(j)
        cols = slice(p * LANES, (p + 1) * LANES)
        rows = slice(j * GRID_W, (j + 1) * GRID_W)
        q2 = jnp.concatenate([qe_ref[rows, cols], qo_ref[rows, cols]], axis=0)
        s = lax.dot_general(q2, kbuf[pl.ds(ks, n_keys), cols], (((1,), (1,)), ((), ())),
                            preferred_element_type=F32)
        bias = jnp.concatenate(
            [tp_ref[p, WIN_ROWS - 1 - d + i] for i in range(0, WIN_ROWS, 2)], axis=-1)
        s_buf[j % 2, p] = s + bias

    def attend(j, p):
        ks, _ = window(j)
        cols = slice(p * LANES, (p + 1) * LANES)
        rows = slice(j * GRID_W, (j + 1) * GRID_W)
        s = s_buf[j % 2, p]
        m = jnp.max(s, axis=-1, keepdims=True)
        e = jnp.exp(s - m)
        l = jnp.sum(e, axis=-1, keepdims=True)
        o = jnp.dot(e.astype(BF16), vbuf[pl.ds(ks, n_keys), cols], preferred_element_type=F32)
        o = o / l
        attn_buf[rows, cols] = jnp.where(first_head, o[:GRID_W], o[GRID_W:]).astype(BF16)

    for p in range(HEAD_PAIRS):
        scores(0, p)
    for j in range(ROWS_PER_BLOCK):
        for p in range(HEAD_PAIRS):
            if j + 1 < ROWS_PER_BLOCK:
                scores(j + 1, p)
            attend(j, p)

    ya = jnp.dot(attn_buf[...], wa_ref[...], preferred_element_type=F32)

    zf = z_ref[...].astype(F32)
    row = lax.broadcasted_iota(jnp.int32, (MIX_TILE, CONV_WIDTH), 0)
    z_before = jnp.where(blk > 0, zp_ref[HALO_ROWS - 1:HALO_ROWS, :].astype(F32), 0.0)
    z_after = jnp.where(blk < n_blk - 1, zn_ref[0:1, :].astype(F32), 0.0)
    z_m1 = jnp.where(row == 0, z_before, pltpu.roll(zf, 1, 0))
    z_p1 = jnp.where(row == MIX_TILE - 1, z_after, pltpu.roll(zf, MIX_TILE - 1, 0))
    conv = z_m1 * cw_ref[0:1, :] + zf * cw_ref[1:2, :] + z_p1 * cw_ref[2:3, :]
    bb = (cb_ref[...].astype(F32) * conv).astype(BF16)
    yb = jnp.dot(bb, wb_ref[...], preferred_element_type=F32)

    merged = (sga_ref[...].astype(F32) * ya + sgb_ref[...].astype(F32) * yb).astype(BF16)
    o_ref[...] = x_ref[...] + jnp.dot(merged, wo_ref[...], preferred_element_type=F32)


def _mixer(batch, seq, qe, qo, k, v, cb, z, sga, sgb, x, tp, cw, wa, wb, wo):
    rows = seq // GRID_W
    n_blk = rows // ROWS_PER_BLOCK
    halo_per_blk = MIX_TILE // HALO_ROWS
    n_halo = seq // HALO_ROWS

    def r3(a):
        return a.reshape(batch, seq, a.shape[-1])

    cur = lambda w: pl.BlockSpec((None, MIX_TILE, w), lambda b, i: (b, i, 0))
    prev = lambda w: pl.BlockSpec((None, MIX_TILE, w), lambda b, i: (b, jnp.maximum(i - 1, 0), 0))
    nxt = lambda w: pl.BlockSpec((None, MIX_TILE, w),
                                 lambda b, i: (b, jnp.minimum(i + 1, n_blk - 1), 0))
    halo_prev = pl.BlockSpec((None, HALO_ROWS, CONV_WIDTH),
                             lambda b, i: (b, jnp.maximum(i * halo_per_blk - 1, 0), 0))
    halo_next = pl.BlockSpec((None, HALO_ROWS, CONV_WIDTH),
                             lambda b, i: (b, jnp.minimum((i + 1) * halo_per_blk, n_halo - 1), 0))
    aw, cwid = ATTN_WIDTH, CONV_WIDTH
    in_specs = [cur(aw), cur(aw), prev(aw), cur(aw), nxt(aw), prev(aw), cur(aw), nxt(aw),
                cur(cwid), cur(cwid), halo_prev, halo_next, cur(D_MODEL), cur(D_MODEL), cur(D_MODEL),
                _const_spec(tp.shape), _const_spec(cw.shape), _const_spec(wa.shape),
                _const_spec(wb.shape), _const_spec(wo.shape)]
    k3, v3, z3 = r3(k), r3(v), r3(z)
    out = pl.pallas_call(
        functools.partial(_mixer_kernel, rows),
        grid=(batch, n_blk),
        in_specs=in_specs,
        out_specs=cur(D_MODEL),
        out_shape=jax.ShapeDtypeStruct((batch, seq, D_MODEL), F32),
        scratch_shapes=[pltpu.VMEM((3 * MIX_TILE, ATTN_WIDTH), BF16),
                        pltpu.VMEM((3 * MIX_TILE, ATTN_WIDTH), BF16),
                        pltpu.VMEM((MIX_TILE, ATTN_WIDTH), BF16),
                        pltpu.VMEM((2, HEAD_PAIRS, 2 * GRID_W, WIN_ROWS * GRID_W), F32)],
        compiler_params=pltpu.CompilerParams(
            dimension_semantics=("arbitrary", "arbitrary"), vmem_limit_bytes=VMEM_LIMIT),
        name="mixer",
    )(r3(qe), r3(qo), k3, k3, k3, v3, v3, v3, r3(cb), z3, z3, z3, r3(sga), r3(sgb), x,
      tp, cw, wa, wb, wo)
    return out


def _bias_table(rpb):
    c = np.arange(GRID_W)
    win_start = np.clip(c - WIN_COLS // 2, 0, GRID_W - WIN_COLS)
    kc = np.arange(GRID_W)
    mask = (kc[None, :] >= win_start[:, None]) & (kc[None, :] < win_start[:, None] + WIN_COLS)
    pad = GRID_W - WIN_COLS
    period = 2 * GRID_W
    g = jnp.pad(rpb, ((0, 0), (0, 0), (pad, period - N_REL_COLS - pad)))
    flat = jnp.tile(g, (1, 1, GRID_W))[:, :, GRID_W - 1:GRID_W - 1 + GRID_W * (period - 1)]
    blocks = flat.reshape(N_HEADS, N_REL_ROWS, GRID_W, period - 1)[..., :GRID_W]
    blocks = jnp.where(jnp.asarray(mask)[None, None], blocks, NEG_INF)
    tbl = jnp.concatenate([blocks[:, :-1], blocks[:, 1:]], axis=-1).astype(F32)
    tbl = tbl.reshape(HEAD_PAIRS, 2, N_REL_ROWS - 1, GRID_W, LANES)
    return tbl.transpose(0, 2, 1, 3, 4).reshape(HEAD_PAIRS, N_REL_ROWS - 1, 2 * GRID_W, LANES)


def _mlp_kernel(x_ref, g_ref, w1_ref, w2_ref, o_ref, h_buf):
    xf = x_ref[...]
    ms = jnp.mean(xf * xf, axis=-1, keepdims=True)
    u = ((xf * lax.rsqrt(ms + EPS)) * g_ref[...]).astype(BF16)
    for c in range(D_FF // FF_CHUNK):
        cols = slice(c * FF_CHUNK, (c + 1) * FF_CHUNK)
        h = jnp.dot(u, w1_ref[:, cols], preferred_element_type=F32)
        h_buf[:, cols] = jnp.square(jnp.maximum(h, 0.0)).astype(BF16)
    o_ref[...] = xf + jnp.dot(h_buf[...], w2_ref[...], preferred_element_type=F32)


def _mlp(x2d, g2, w1, w2):
    m = x2d.shape[0]
    tok = pl.BlockSpec((TOK_TILE, D_MODEL), lambda i: (i, 0))
    return pl.pallas_call(
        _mlp_kernel,
        grid=(m // TOK_TILE,),
        in_specs=[tok, _const_spec((1, D_MODEL)), _const_spec(w1.shape), _const_spec(w2.shape)],
        out_specs=tok,
        out_shape=jax.ShapeDtypeStruct((m, D_MODEL), F32),
        scratch_shapes=[pltpu.VMEM((TOK_TILE, D_FF), BF16)],
        compiler_params=pltpu.CompilerParams(
            dimension_semantics=("arbitrary",), vmem_limit_bytes=VMEM_LIMIT),
        name="mlp",
    )(x2d, g2, w1, w2)


def kernel(x, norm1_g, w_in, q_norm_g, k_norm_g, rpb, conv_w, w_attn_branch, w_conv_branch,
           w_o, norm2_g, w_mlp_in, w_mlp_out):
    batch, seq, _ = x.shape
    depth = w_in.shape[0]
    for l in range(depth):
        x2d = x.reshape(batch * seq, D_MODEL)
        gq = jnp.tile(q_norm_g[l], N_HEADS)[None, :]
        gk = jnp.tile(k_norm_g[l], N_HEADS)[None, :]
        qe, qo, k, v, cb, z, sga, sgb, wa, wb, wo, w1, w2 = _inproj(
            x2d, norm1_g[l][None, :], w_in[l].astype(BF16), gq, gk,
            (w_attn_branch[l], w_conv_branch[l], w_o[l], w_mlp_in[l], w_mlp_out[l]))
        x = _mixer(batch, seq, qe, qo, k, v, cb, z, sga, sgb, x,
                   _bias_table(rpb[l]), conv_w[l], wa, wb, wo)
        x = _mlp(x.reshape(batch * seq, D_MODEL), norm2_g[l][None, :], w1, w2
                 ).reshape(batch, seq, D_MODEL)
    return x
```

```python
import functools

import numpy as np
import jax
import jax.numpy as jnp
from jax import lax
from jax.experimental import pallas as pl
from jax.experimental.pallas import tpu as pltpu

F32 = jnp.float32
BF16 = jnp.bfloat16

D_MODEL = 1024
GRID_W = 64
N_HEADS = 8
HEAD_DIM = 64
ATTN_WIDTH = N_HEADS * HEAD_DIM
CONV_WIDTH = D_MODEL // 2
WIN_ROWS = 8
WIN_COLS = 16
D_FF = 4 * D_MODEL
EPS = 1e-6
NEG_INF = -1e30
PROJ_WIDTH = 3 * ATTN_WIDTH + 3 * CONV_WIDTH + 2 * D_MODEL

LANES = 128
HEAD_PAIRS = ATTN_WIDTH // LANES
N_REL_ROWS = 2 * WIN_ROWS - 1
N_REL_COLS = 2 * WIN_COLS - 1

TOK_TILE = 512
ROWS_PER_BLOCK = 8
MIX_TILE = ROWS_PER_BLOCK * GRID_W
FF_CHUNK = 512
BF16_SUBLANES = 16
HALO_ROWS = BF16_SUBLANES
VMEM_LIMIT = 56 * 1024 * 1024


def _const_spec(shape):
    zeros = (0,) * len(shape)
    return pl.BlockSpec(shape, lambda *_: zeros)


def _inproj_kernel(n_cast, x_ref, g1_ref, w_ref, gq_ref, gk_ref, *refs):
    cast_in, refs = refs[:n_cast], refs[n_cast:]
    qe_ref, qo_ref, k_ref, v_ref, cb_ref, z_ref, sga_ref, sgb_ref = refs[:8]
    cast_out = refs[8:]
    for src, dst in zip(cast_in, cast_out):
        dst[...] = src[...].astype(BF16)

    xf = x_ref[...]
    ms = jnp.mean(xf * xf, axis=-1, keepdims=True)
    u = ((xf * lax.rsqrt(ms + EPS)) * g1_ref[...]).astype(BF16)

    def proj(c0, width):
        return jnp.dot(u, w_ref[:, c0:c0 + width], preferred_element_type=F32)

    first_head = lax.broadcasted_iota(jnp.int32, (TOK_TILE, LANES), 1) < HEAD_DIM

    def head_norm(t, g):
        outs = []
        for p in range(HEAD_PAIRS):
            tp = t[:, p * LANES:(p + 1) * LANES]
            t2 = tp * tp
            ss_a = jnp.sum(jnp.where(first_head, t2, 0.0), axis=-1, keepdims=True)
            ss_b = jnp.sum(jnp.where(first_head, 0.0, t2), axis=-1, keepdims=True)
            r_a = lax.rsqrt(ss_a * (1.0 / HEAD_DIM) + EPS)
            r_b = lax.rsqrt(ss_b * (1.0 / HEAD_DIM) + EPS)
            outs.append(tp * jnp.where(first_head, r_a, r_b))
        return jnp.concatenate(outs, axis=-1) * g

    lane = lax.broadcasted_iota(jnp.int32, (TOK_TILE, ATTN_WIDTH), 1)
    even_head = (lane % LANES) < HEAD_DIM

    c0 = 3 * ATTN_WIDTH
    g0 = c0 + 3 * CONV_WIDTH
    sga_ref[...] = jax.nn.sigmoid(proj(g0, D_MODEL)).astype(BF16)
    sgb_ref[...] = jax.nn.sigmoid(proj(g0 + D_MODEL, D_MODEL)).astype(BF16)
    q = head_norm(proj(0, ATTN_WIDTH), gq_ref[...]) * (HEAD_DIM ** -0.5)
    qe_ref[...] = jnp.where(even_head, q, 0.0).astype(BF16)
    qo_ref[...] = jnp.where(even_head, 0.0, q).astype(BF16)
    k_ref[...] = head_norm(proj(ATTN_WIDTH, ATTN_WIDTH), gk_ref[...]).astype(BF16)
    z_ref[...] = (proj(c0 + CONV_WIDTH, CONV_WIDTH)
                  * proj(c0 + 2 * CONV_WIDTH, CONV_WIDTH)).astype(BF16)
    cb_ref[...] = proj(c0, CONV_WIDTH).astype(BF16)
    v_ref[...] = proj(2 * ATTN_WIDTH, ATTN_WIDTH).astype(BF16)


def _inproj(x2d, g1, w_in, gq, gk, layer, later_weights):
    m = x2d.shape[0]
    n_steps = m // TOK_TILE
    tok = lambda w: pl.BlockSpec((TOK_TILE, w), lambda i: (i, 0))
    out_widths = [ATTN_WIDTH] * 4 + [CONV_WIDTH] * 2 + [D_MODEL] * 2
    cast_in_specs, cast_out_specs = [], []
    for w in later_weights:
        rows = w.shape[1] // n_steps
        assert rows * n_steps == w.shape[1] and rows % BF16_SUBLANES == 0, w.shape
        cast_in_specs.append(pl.BlockSpec((None, rows, w.shape[2]), lambda i: (layer, i, 0)))
        cast_out_specs.append(pl.BlockSpec((rows, w.shape[2]), lambda i: (i, 0)))
    return pl.pallas_call(
        functools.partial(_inproj_kernel, len(later_weights)),
        grid=(n_steps,),
        in_specs=[tok(D_MODEL), _const_spec((1, D_MODEL)), _const_spec((D_MODEL, PROJ_WIDTH)),
                  _const_spec((1, ATTN_WIDTH)), _const_spec((1, ATTN_WIDTH))] + cast_in_specs,
        out_specs=[tok(w) for w in out_widths] + cast_out_specs,
        out_shape=([jax.ShapeDtypeStruct((m, w), BF16) for w in out_widths]
                   + [jax.ShapeDtypeStruct(w.shape[1:], BF16) for w in later_weights]),
        compiler_params=pltpu.CompilerParams(
            dimension_semantics=("arbitrary",), vmem_limit_bytes=VMEM_LIMIT),
        name="inproj",
    )(x2d, g1, w_in, gq, gk, *later_weights)


def _mixer_kernel(rows_per_batch,
                  qe_ref, qo_ref, kp_ref, kc_ref, kn_ref, vp_ref, vc_ref, vn_ref,
                  cb_ref, z_ref, zp_ref, zn_ref, sga_ref, sgb_ref, x_ref,
                  g_ref, cw_ref, wa_ref, wb_ref, wo_ref,
                  o_ref, kbuf, vbuf, attn_buf, s_buf, tp_ref):
    blk = pl.program_id(1)
    n_blk = pl.num_programs(1)

    @pl.when(jnp.logical_and(pl.program_id(0) == 0, blk == 0))
    def _():
        _build_bias_table(g_ref, tp_ref)

    kbuf[0:MIX_TILE] = kp_ref[...]
    kbuf[MIX_TILE:2 * MIX_TILE] = kc_ref[...]
    kbuf[2 * MIX_TILE:3 * MIX_TILE] = kn_ref[...]
    vbuf[0:MIX_TILE] = vp_ref[...]
    vbuf[MIX_TILE:2 * MIX_TILE] = vc_ref[...]
    vbuf[2 * MIX_TILE:3 * MIX_TILE] = vn_ref[...]

    lane = lax.broadcasted_iota(jnp.int32, (GRID_W, LANES), 1)
    first_head = lane < HEAD_DIM
    n_keys = WIN_ROWS * GRID_W

    def window(j):
        r = blk * ROWS_PER_BLOCK + j
        row_start = jnp.clip(r - WIN_ROWS // 2, 0, rows_per_batch - WIN_ROWS)
        ks = pl.multiple_of((row_start - (blk - 1) * ROWS_PER_BLOCK) * GRID_W, GRID_W)
        return ks, r - row_start

    def scores(j, p):
        ks, d = window(j)
        cols = slice(p * LANES, (p + 1) * LANES)
        rows = slice(j * GRID_W, (j + 1) * GRID_W)
        q2 = jnp.concatenate([qe_ref[rows, cols], qo_ref[rows, cols]], axis=0)
        s = lax.dot_general(q2, kbuf[pl.ds(ks, n_keys), cols], (((1,), (1,)), ((), ())),
                            preferred_element_type=F32)
        bias = jnp.concatenate(
            [tp_ref[p, WIN_ROWS - 1 - d + i] for i in range(0, WIN_ROWS, 2)], axis=-1)
        s_buf[j % 2, p] = s + bias

    def attend(j, p):
        ks, _ = window(j)
        cols = slice(p * LANES, (p + 1) * LANES)
        rows = slice(j * GRID_W, (j + 1) * GRID_W)
        s = s_buf[j % 2, p]
        m = jnp.max(s, axis=-1, keepdims=True)
        e = jnp.exp(s - m)
        l = jnp.sum(e, axis=-1, keepdims=True)
        o = jnp.dot(e.astype(BF16), vbuf[pl.ds(ks, n_keys), cols], preferred_element_type=F32)
        o = o / l
        attn_buf[rows, cols] = jnp.where(first_head, o[:GRID_W], o[GRID_W:]).astype(BF16)

    for p in range(HEAD_PAIRS):
        scores(0, p)
    for j in range(ROWS_PER_BLOCK):
        for p in range(HEAD_PAIRS):
            if j + 1 < ROWS_PER_BLOCK:
                scores(j + 1, p)
            attend(j, p)

    ya = jnp.dot(attn_buf[...], wa_ref[...], preferred_element_type=F32)

    zf = z_ref[...].astype(F32)
    row = lax.broadcasted_iota(jnp.int32, (MIX_TILE, CONV_WIDTH), 0)
    z_before = jnp.where(blk > 0, zp_ref[HALO_ROWS - 1:HALO_ROWS, :].astype(F32), 0.0)
    z_after = jnp.where(blk < n_blk - 1, zn_ref[0:1, :].astype(F32), 0.0)
    z_m1 = jnp.where(row == 0, z_before, pltpu.roll(zf, 1, 0))
    z_p1 = jnp.where(row == MIX_TILE - 1, z_after, pltpu.roll(zf, MIX_TILE - 1, 0))
    conv = z_m1 * cw_ref[0:1, :] + zf * cw_ref[1:2, :] + z_p1 * cw_ref[2:3, :]
    bb = (cb_ref[...].astype(F32) * conv).astype(BF16)
    yb = jnp.dot(bb, wb_ref[...], preferred_element_type=F32)

    merged = (sga_ref[...].astype(F32) * ya + sgb_ref[...].astype(F32) * yb).astype(BF16)
    o_ref[...] = x_ref[...] + jnp.dot(merged, wo_ref[...], preferred_element_type=F32)


def _mixer(batch, seq, qe, qo, k, v, cb, z, sga, sgb, x, bias_rows, cw, wa, wb, wo):
    rows = seq // GRID_W
    n_blk = rows // ROWS_PER_BLOCK
    halo_per_blk = MIX_TILE // HALO_ROWS
    n_halo = seq // HALO_ROWS

    def r3(a):
        return a.reshape(batch, seq, a.shape[-1])

    cur = lambda w: pl.BlockSpec((None, MIX_TILE, w), lambda b, i: (b, i, 0))
    prev = lambda w: pl.BlockSpec((None, MIX_TILE, w), lambda b, i: (b, jnp.maximum(i - 1, 0), 0))
    nxt = lambda w: pl.BlockSpec((None, MIX_TILE, w),
                                 lambda b, i: (b, jnp.minimum(i + 1, n_blk - 1), 0))
    halo_prev = pl.BlockSpec((None, HALO_ROWS, CONV_WIDTH),
                             lambda b, i: (b, jnp.maximum(i * halo_per_blk - 1, 0), 0))
    halo_next = pl.BlockSpec((None, HALO_ROWS, CONV_WIDTH),
                             lambda b, i: (b, jnp.minimum((i + 1) * halo_per_blk, n_halo - 1), 0))
    aw, cwid = ATTN_WIDTH, CONV_WIDTH
    in_specs = [cur(aw), cur(aw), prev(aw), cur(aw), nxt(aw), prev(aw), cur(aw), nxt(aw),
                cur(cwid), cur(cwid), halo_prev, halo_next, cur(D_MODEL), cur(D_MODEL), cur(D_MODEL),
                _const_spec(bias_rows.shape), _const_spec(cw.shape), _const_spec(wa.shape),
                _const_spec(wb.shape), _const_spec(wo.shape)]
    k3, v3, z3 = r3(k), r3(v), r3(z)
    out = pl.pallas_call(
        functools.partial(_mixer_kernel, rows),
        grid=(batch, n_blk),
        in_specs=in_specs,
        out_specs=cur(D_MODEL),
        out_shape=jax.ShapeDtypeStruct((batch, seq, D_MODEL), F32),
        scratch_shapes=[pltpu.VMEM((3 * MIX_TILE, ATTN_WIDTH), BF16),
                        pltpu.VMEM((3 * MIX_TILE, ATTN_WIDTH), BF16),
                        pltpu.VMEM((MIX_TILE, ATTN_WIDTH), BF16),
                        pltpu.VMEM((2, HEAD_PAIRS, 2 * GRID_W, WIN_ROWS * GRID_W), F32),
                        pltpu.VMEM((HEAD_PAIRS, N_REL_ROWS - 1, 2 * GRID_W, LANES), F32)],
        compiler_params=pltpu.CompilerParams(
            dimension_semantics=("arbitrary", "arbitrary"), vmem_limit_bytes=VMEM_LIMIT),
        name="mixer",
    )(r3(qe), r3(qo), k3, k3, k3, v3, v3, v3, r3(cb), z3, z3, z3, r3(sga), r3(sgb), x,
      bias_rows, cw, wa, wb, wo)
    return out


def _padded_bias_rows(rpb):
    pad = GRID_W - WIN_COLS
    return jnp.pad(rpb.astype(F32), ((0, 0), (0, 0), (pad, LANES - N_REL_COLS - pad)))


def _build_bias_table(g_ref, tbl_ref):
    c = lax.broadcasted_iota(jnp.int32, (GRID_W, LANES), 0)
    x = lax.broadcasted_iota(jnp.int32, (GRID_W, LANES), 1)
    kc = x % GRID_W
    win_start = jnp.clip(c - WIN_COLS // 2, 0, GRID_W - WIN_COLS)
    in_window = (kc >= win_start) & (kc < win_start + WIN_COLS)
    left = x < GRID_W
    for h in range(N_HEADS):
        rows = slice((h % 2) * GRID_W, (h % 2 + 1) * GRID_W)
        for rr in range(N_REL_ROWS - 1):
            g0 = jnp.broadcast_to(g_ref[h, rr:rr + 1, :], (GRID_W, LANES))
            g1 = jnp.broadcast_to(g_ref[h, rr + 1:rr + 2, :], (GRID_W, LANES))
            a = pltpu.roll(g0, LANES - (GRID_W - 1), 1, stride=1, stride_axis=0)
            b = pltpu.roll(g1, 1, 1, stride=1, stride_axis=0)
            tbl_ref[h // 2, rr, rows, :] = jnp.where(in_window, jnp.where(left, a, b), NEG_INF)


def _mlp_kernel(x_ref, g_ref, w1_ref, w2_ref, o_ref, h_buf):
    xf = x_ref[...]
    ms = jnp.mean(xf * xf, axis=-1, keepdims=True)
    u = ((xf * lax.rsqrt(ms + EPS)) * g_ref[...]).astype(BF16)
    for c in range(D_FF // FF_CHUNK):
        cols = slice(c * FF_CHUNK, (c + 1) * FF_CHUNK)
        h = jnp.dot(u, w1_ref[:, cols], preferred_element_type=F32)
        h_buf[:, cols] = jnp.square(jnp.maximum(h, 0.0)).astype(BF16)
    o_ref[...] = xf + jnp.dot(h_buf[...], w2_ref[...], preferred_element_type=F32)


def _mlp(x2d, g2, w1, w2):
    m = x2d.shape[0]
    tok = pl.BlockSpec((TOK_TILE, D_MODEL), lambda i: (i, 0))
    return pl.pallas_call(
        _mlp_kernel,
        grid=(m // TOK_TILE,),
        in_specs=[tok, _const_spec((1, D_MODEL)), _const_spec(w1.shape), _const_spec(w2.shape)],
        out_specs=tok,
        out_shape=jax.ShapeDtypeStruct((m, D_MODEL), F32),
        scratch_shapes=[pltpu.VMEM((TOK_TILE, D_FF), BF16)],
        compiler_params=pltpu.CompilerParams(
            dimension_semantics=("arbitrary",), vmem_limit_bytes=VMEM_LIMIT),
        name="mlp",
    )(x2d, g2, w1, w2)


def kernel(x, norm1_g, w_in, q_norm_g, k_norm_g, rpb, conv_w, w_attn_branch, w_conv_branch,
           w_o, norm2_g, w_mlp_in, w_mlp_out):
    batch, seq, _ = x.shape
    depth = w_in.shape[0]
    for l in range(depth):
        x2d = x.reshape(batch * seq, D_MODEL)
        gq = jnp.tile(q_norm_g[l], N_HEADS)[None, :]
        gk = jnp.tile(k_norm_g[l], N_HEADS)[None, :]
        qe, qo, k, v, cb, z, sga, sgb, wa, wb, wo, w1, w2 = _inproj(
            x2d, norm1_g[l][None, :], w_in[l].astype(BF16), gq, gk, l,
            (w_attn_branch, w_conv_branch, w_o, w_mlp_in, w_mlp_out))
        x = _mixer(batch, seq, qe, qo, k, v, cb, z, sga, sgb, x,
                   _padded_bias_rows(rpb[l]), conv_w[l], wa, wb, wo)
        x = _mlp(x.reshape(batch * seq, D_MODEL), norm2_g[l][None, :], w1, w2
                 ).reshape(batch, seq, D_MODEL)
    return x
```

```python
import functools

import numpy as np
import jax
import jax.numpy as jnp
from jax import lax
from jax.experimental import pallas as pl
from jax.experimental.pallas import tpu as pltpu

F32 = jnp.float32
BF16 = jnp.bfloat16

D_MODEL = 1024
GRID_W = 64
N_HEADS = 8
HEAD_DIM = 64
ATTN_WIDTH = N_HEADS * HEAD_DIM
CONV_WIDTH = D_MODEL // 2
WIN_ROWS = 8
WIN_COLS = 16
D_FF = 4 * D_MODEL
EPS = 1e-6
NEG_INF = -1e30
PROJ_WIDTH = 3 * ATTN_WIDTH + 3 * CONV_WIDTH + 2 * D_MODEL

LANES = 128
HEAD_PAIRS = ATTN_WIDTH // LANES
N_REL_ROWS = 2 * WIN_ROWS - 1
N_REL_COLS = 2 * WIN_COLS - 1

TOK_TILE = 1024
ROWS_PER_BLOCK = 8
MIX_TILE = ROWS_PER_BLOCK * GRID_W
FF_CHUNK = 512
BF16_SUBLANES = 16
HALO_ROWS = BF16_SUBLANES
VMEM_LIMIT = 56 * 1024 * 1024


def _const_spec(shape):
    zeros = (0,) * len(shape)
    return pl.BlockSpec(shape, lambda *_: zeros, pipeline_mode=pl.Buffered(1))


def _inproj_kernel(n_cast, x_ref, g1_ref, w_ref, gq_ref, gk_ref, *refs):
    cast_in, refs = refs[:n_cast], refs[n_cast:]
    qe_ref, qo_ref, k_ref, v_ref, cb_ref, z_ref, sga_ref, sgb_ref = refs[:8]
    cast_out = refs[8:]
    for src, dst in zip(cast_in, cast_out):
        dst[...] = src[...].astype(BF16)

    xf = x_ref[...]
    ms = jnp.mean(xf * xf, axis=-1, keepdims=True)
    u = ((xf * lax.rsqrt(ms + EPS)) * g1_ref[...]).astype(BF16)

    def proj(c0, width):
        return jnp.dot(u, w_ref[:, c0:c0 + width], preferred_element_type=F32)

    first_head = lax.broadcasted_iota(jnp.int32, (TOK_TILE, LANES), 1) < HEAD_DIM

    def head_norm(t, g):
        outs = []
        for p in range(HEAD_PAIRS):
            tp = t[:, p * LANES:(p + 1) * LANES]
            t2 = tp * tp
            ss_a = jnp.sum(jnp.where(first_head, t2, 0.0), axis=-1, keepdims=True)
            ss_b = jnp.sum(jnp.where(first_head, 0.0, t2), axis=-1, keepdims=True)
            r_a = lax.rsqrt(ss_a * (1.0 / HEAD_DIM) + EPS)
            r_b = lax.rsqrt(ss_b * (1.0 / HEAD_DIM) + EPS)
            outs.append(tp * jnp.where(first_head, r_a, r_b))
        return jnp.concatenate(outs, axis=-1) * g

    lane = lax.broadcasted_iota(jnp.int32, (TOK_TILE, ATTN_WIDTH), 1)
    even_head = (lane % LANES) < HEAD_DIM

    c0 = 3 * ATTN_WIDTH
    g0 = c0 + 3 * CONV_WIDTH
    sga_ref[...] = jax.nn.sigmoid(proj(g0, D_MODEL)).astype(BF16)
    sgb_ref[...] = jax.nn.sigmoid(proj(g0 + D_MODEL, D_MODEL)).astype(BF16)
    q = head_norm(proj(0, ATTN_WIDTH), gq_ref[...]) * (HEAD_DIM ** -0.5)
    qe_ref[...] = jnp.where(even_head, q, 0.0).astype(BF16)
    qo_ref[...] = jnp.where(even_head, 0.0, q).astype(BF16)
    k_ref[...] = head_norm(proj(ATTN_WIDTH, ATTN_WIDTH), gk_ref[...]).astype(BF16)
    z_ref[...] = (proj(c0 + CONV_WIDTH, CONV_WIDTH)
                  * proj(c0 + 2 * CONV_WIDTH, CONV_WIDTH)).astype(BF16)
    cb_ref[...] = proj(c0, CONV_WIDTH).astype(BF16)
    v_ref[...] = proj(2 * ATTN_WIDTH, ATTN_WIDTH).astype(BF16)


def _inproj(x2d, g1, w_in, gq, gk, layer, later_weights):
    m = x2d.shape[0]
    n_steps = m // TOK_TILE
    tok = lambda w: pl.BlockSpec((TOK_TILE, w), lambda i: (i, 0))
    out_widths = [ATTN_WIDTH] * 4 + [CONV_WIDTH] * 2 + [D_MODEL] * 2
    cast_in_specs, cast_out_specs = [], []
    for w in later_weights:
        rows = w.shape[1] // n_steps
        assert rows * n_steps == w.shape[1] and rows % BF16_SUBLANES == 0, w.shape
        cast_in_specs.append(pl.BlockSpec((None, rows, w.shape[2]), lambda i: (layer, i, 0)))
        cast_out_specs.append(pl.BlockSpec((rows, w.shape[2]), lambda i: (i, 0)))
    return pl.pallas_call(
        functools.partial(_inproj_kernel, len(later_weights)),
        grid=(n_steps,),
        in_specs=[tok(D_MODEL), _const_spec((1, D_MODEL)), _const_spec((D_MODEL, PROJ_WIDTH)),
                  _const_spec((1, ATTN_WIDTH)), _const_spec((1, ATTN_WIDTH))] + cast_in_specs,
        out_specs=[tok(w) for w in out_widths] + cast_out_specs,
        out_shape=([jax.ShapeDtypeStruct((m, w), BF16) for w in out_widths]
                   + [jax.ShapeDtypeStruct(w.shape[1:], BF16) for w in later_weights]),
        compiler_params=pltpu.CompilerParams(
            dimension_semantics=("arbitrary",), vmem_limit_bytes=VMEM_LIMIT),
        name="inproj",
    )(x2d, g1, w_in, gq, gk, *later_weights)


def _mixer_kernel(rows_per_batch,
                  qe_ref, qo_ref, kp_ref, kc_ref, kn_ref, vp_ref, vc_ref, vn_ref,
                  cb_ref, z_ref, zp_ref, zn_ref, sga_ref, sgb_ref, x_ref,
                  g_ref, cw_ref, wa_ref, wb_ref, wo_ref,
                  o_ref, kbuf, vbuf, attn_buf, s_buf, tp_ref):
    blk = pl.program_id(1)
    n_blk = pl.num_programs(1)

    @pl.when(jnp.logical_and(pl.program_id(0) == 0, blk == 0))
    def _():
        _build_bias_table(g_ref, tp_ref)

    kbuf[0:MIX_TILE] = kp_ref[...]
    kbuf[MIX_TILE:2 * MIX_TILE] = kc_ref[...]
    kbuf[2 * MIX_TILE:3 * MIX_TILE] = kn_ref[...]
    vbuf[0:MIX_TILE] = vp_ref[...]
    vbuf[MIX_TILE:2 * MIX_TILE] = vc_ref[...]
    vbuf[2 * MIX_TILE:3 * MIX_TILE] = vn_ref[...]

    lane = lax.broadcasted_iota(jnp.int32, (GRID_W, LANES), 1)
    first_head = lane < HEAD_DIM
    n_keys = WIN_ROWS * GRID_W

    def window(j):
        r = blk * ROWS_PER_BLOCK + j
        row_start = jnp.clip(r - WIN_ROWS // 2, 0, rows_per_batch - WIN_ROWS)
        ks = pl.multiple_of((row_start - (blk - 1) * ROWS_PER_BLOCK) * GRID_W, GRID_W)
        return ks, r - row_start

    def scores(j, p):
        ks, d = window(j)
        cols = slice(p * LANES, (p + 1) * LANES)
        rows = slice(j * GRID_W, (j + 1) * GRID_W)
        q2 = jnp.concatenate([qe_ref[rows, cols], qo_ref[rows, cols]], axis=0)
        s = lax.dot_general(q2, kbuf[pl.ds(ks, n_keys), cols], (((1,), (1,)), ((), ())),
                            preferred_element_type=F32)
        bias = jnp.concatenate(
            [tp_ref[p, WIN_ROWS - 1 - d + i] for i in range(0, WIN_ROWS, 2)], axis=-1)
        s_buf[j % 2, p] = s + bias

    def attend(j, p):
        ks, _ = window(j)
        cols = slice(p * LANES, (p + 1) * LANES)
        rows = slice(j * GRID_W, (j + 1) * GRID_W)
        s = s_buf[j % 2, p]
        m = jnp.max(s, axis=-1, keepdims=True)
        e = jnp.exp(s - m)
        l = jnp.sum(e, axis=-1, keepdims=True)
        o = jnp.dot(e.astype(BF16), vbuf[pl.ds(ks, n_keys), cols], preferred_element_type=F32)
        o = o / l
        attn_buf[rows, cols] = jnp.where(first_head, o[:GRID_W], o[GRID_W:]).astype(BF16)

    for p in range(HEAD_PAIRS):
        scores(0, p)
    for j in range(ROWS_PER_BLOCK):
        for p in range(HEAD_PAIRS):
            if j + 1 < ROWS_PER_BLOCK:
                scores(j + 1, p)
            attend(j, p)

    ya = jnp.dot(attn_buf[...], wa_ref[...], preferred_element_type=F32)

    zf = z_ref[...].astype(F32)
    row = lax.broadcasted_iota(jnp.int32, (MIX_TILE, CONV_WIDTH), 0)
    z_before = jnp.where(blk > 0, zp_ref[HALO_ROWS - 1:HALO_ROWS, :].astype(F32), 0.0)
    z_after = jnp.where(blk < n_blk - 1, zn_ref[0:1, :].astype(F32), 0.0)
    z_m1 = jnp.where(row == 0, z_before, pltpu.roll(zf, 1, 0))
    z_p1 = jnp.where(row == MIX_TILE - 1, z_after, pltpu.roll(zf, MIX_TILE - 1, 0))
    conv = z_m1 * cw_ref[0:1, :] + zf * cw_ref[1:2, :] + z_p1 * cw_ref[2:3, :]
    bb = (cb_ref[...].astype(F32) * conv).astype(BF16)
    yb = jnp.dot(bb, wb_ref[...], preferred_element_type=F32)

    merged = (sga_ref[...].astype(F32) * ya + sgb_ref[...].astype(F32) * yb).astype(BF16)
    o_ref[...] = x_ref[...] + jnp.dot(merged, wo_ref[...], preferred_element_type=F32)


def _mixer(batch, seq, qe, qo, k, v, cb, z, sga, sgb, x, bias_rows, cw, wa, wb, wo):
    rows = seq // GRID_W
    n_blk = rows // ROWS_PER_BLOCK
    halo_per_blk = MIX_TILE // HALO_ROWS
    n_halo = seq // HALO_ROWS

    def r3(a):
        return a.reshape(batch, seq, a.shape[-1])

    cur = lambda w: pl.BlockSpec((None, MIX_TILE, w), lambda b, i: (b, i, 0))
    prev = lambda w: pl.BlockSpec((None, MIX_TILE, w), lambda b, i: (b, jnp.maximum(i - 1, 0), 0))
    nxt = lambda w: pl.BlockSpec((None, MIX_TILE, w),
                                 lambda b, i: (b, jnp.minimum(i + 1, n_blk - 1), 0))
    halo_prev = pl.BlockSpec((None, HALO_ROWS, CONV_WIDTH),
                             lambda b, i: (b, jnp.maximum(i * halo_per_blk - 1, 0), 0))
    halo_next = pl.BlockSpec((None, HALO_ROWS, CONV_WIDTH),
                             lambda b, i: (b, jnp.minimum((i + 1) * halo_per_blk, n_halo - 1), 0))
    aw, cwid = ATTN_WIDTH, CONV_WIDTH
    in_specs = [cur(aw), cur(aw), prev(aw), cur(aw), nxt(aw), prev(aw), cur(aw), nxt(aw),
                cur(cwid), cur(cwid), halo_prev, halo_next, cur(D_MODEL), cur(D_MODEL), cur(D_MODEL),
                _const_spec(bias_rows.shape), _const_spec(cw.shape), _const_spec(wa.shape),
                _const_spec(wb.shape), _const_spec(wo.shape)]
    k3, v3, z3 = r3(k), r3(v), r3(z)
    out = pl.pallas_call(
        functools.partial(_mixer_kernel, rows),
        grid=(batch, n_blk),
        in_specs=in_specs,
        out_specs=cur(D_MODEL),
        out_shape=jax.ShapeDtypeStruct((batch, seq, D_MODEL), F32),
        scratch_shapes=[pltpu.VMEM((3 * MIX_TILE, ATTN_WIDTH), BF16),
                        pltpu.VMEM((3 * MIX_TILE, ATTN_WIDTH), BF16),
                        pltpu.VMEM((MIX_TILE, ATTN_WIDTH), BF16),
                        pltpu.VMEM((2, HEAD_PAIRS, 2 * GRID_W, WIN_ROWS * GRID_W), F32),
                        pltpu.VMEM((HEAD_PAIRS, N_REL_ROWS - 1, 2 * GRID_W, LANES), F32)],
        compiler_params=pltpu.CompilerParams(
            dimension_semantics=("arbitrary", "arbitrary"), vmem_limit_bytes=VMEM_LIMIT),
        name="mixer",
    )(r3(qe), r3(qo), k3, k3, k3, v3, v3, v3, r3(cb), z3, z3, z3, r3(sga), r3(sgb), x,
      bias_rows, cw, wa, wb, wo)
    return out


def _padded_bias_rows(rpb):
    pad = GRID_W - WIN_COLS
    return jnp.pad(rpb.astype(F32), ((0, 0), (0, 0), (pad, LANES - N_REL_COLS - pad)))


def _build_bias_table(g_ref, tbl_ref):
    c = lax.broadcasted_iota(jnp.int32, (GRID_W, LANES), 0)
    x = lax.broadcasted_iota(jnp.int32, (GRID_W, LANES), 1)
    kc = x % GRID_W
    win_start = jnp.clip(c - WIN_COLS // 2, 0, GRID_W - WIN_COLS)
    in_window = (kc >= win_start) & (kc < win_start + WIN_COLS)
    left = x < GRID_W
    for h in range(N_HEADS):
        rows = slice((h % 2) * GRID_W, (h % 2 + 1) * GRID_W)
        for rr in range(N_REL_ROWS - 1):
            g0 = jnp.broadcast_to(g_ref[h, rr:rr + 1, :], (GRID_W, LANES))
            g1 = jnp.broadcast_to(g_ref[h, rr + 1:rr + 2, :], (GRID_W, LANES))
            a = pltpu.roll(g0, LANES - (GRID_W - 1), 1, stride=1, stride_axis=0)
            b = pltpu.roll(g1, 1, 1, stride=1, stride_axis=0)
            tbl_ref[h // 2, rr, rows, :] = jnp.where(in_window, jnp.where(left, a, b), NEG_INF)


def _mlp_kernel(x_ref, g_ref, w1_ref, w2_ref, o_ref, h_buf):
    xf = x_ref[...]
    ms = jnp.mean(xf * xf, axis=-1, keepdims=True)
    u = ((xf * lax.rsqrt(ms + EPS)) * g_ref[...]).astype(BF16)
    for c in range(D_FF // FF_CHUNK):
        cols = slice(c * FF_CHUNK, (c + 1) * FF_CHUNK)
        h = jnp.dot(u, w1_ref[:, cols], preferred_element_type=F32)
        h_buf[:, cols] = jnp.square(jnp.maximum(h, 0.0)).astype(BF16)
    o_ref[...] = xf + jnp.dot(h_buf[...], w2_ref[...], preferred_element_type=F32)


def _mlp(x2d, g2, w1, w2):
    m = x2d.shape[0]
    tok = pl.BlockSpec((TOK_TILE, D_MODEL), lambda i: (i, 0))
    return pl.pallas_call(
        _mlp_kernel,
        grid=(m // TOK_TILE,),
        in_specs=[tok, _const_spec((1, D_MODEL)), _const_spec(w1.shape), _const_spec(w2.shape)],
        out_specs=tok,
        out_shape=jax.ShapeDtypeStruct((m, D_MODEL), F32),
        scratch_shapes=[pltpu.VMEM((TOK_TILE, D_FF), BF16)],
        compiler_params=pltpu.CompilerParams(
            dimension_semantics=("arbitrary",), vmem_limit_bytes=VMEM_LIMIT),
        name="mlp",
    )(x2d, g2, w1, w2)


def kernel(x, norm1_g, w_in, q_norm_g, k_norm_g, rpb, conv_w, w_attn_branch, w_conv_branch,
           w_o, norm2_g, w_mlp_in, w_mlp_out):
    batch, seq, _ = x.shape
    depth = w_in.shape[0]
    for l in range(depth):
        x2d = x.reshape(batch * seq, D_MODEL)
        gq = jnp.tile(q_norm_g[l], N_HEADS)[None, :]
        gk = jnp.tile(k_norm_g[l], N_HEADS)[None, :]
        qe, qo, k, v, cb, z, sga, sgb, wa, wb, wo, w1, w2 = _inproj(
            x2d, norm1_g[l][None, :], w_in[l].astype(BF16), gq, gk, l,
            (w_attn_branch, w_conv_branch, w_o, w_mlp_in, w_mlp_out))
        x = _mixer(batch, seq, qe, qo, k, v, cb, z, sga, sgb, x,
                   _padded_bias_rows(rpb[l]), conv_w[l], wa, wb, wo)
        x = _mlp(x.reshape(batch * seq, D_MODEL), norm2_g[l][None, :], w1, w2
                 ).reshape(batch, seq, D_MODEL)
    return x
```

```python
import functools

import numpy as np
import jax
import jax.numpy as jnp
from jax import lax
from jax.experimental import pallas as pl
from jax.experimental.pallas import tpu as pltpu

F32 = jnp.float32
BF16 = jnp.bfloat16

D_MODEL = 1024
GRID_W = 64
N_HEADS = 8
HEAD_DIM = 64
ATTN_WIDTH = N_HEADS * HEAD_DIM
CONV_WIDTH = D_MODEL // 2
WIN_ROWS = 8
WIN_COLS = 16
D_FF = 4 * D_MODEL
EPS = 1e-6
NEG_INF = -1e30
PROJ_WIDTH = 3 * ATTN_WIDTH + 3 * CONV_WIDTH + 2 * D_MODEL

LANES = 128
HEAD_PAIRS = ATTN_WIDTH // LANES
N_REL_ROWS = 2 * WIN_ROWS - 1
N_REL_COLS = 2 * WIN_COLS - 1

TOK_TILE = 1024
ROWS_PER_BLOCK = 8
MIX_TILE = ROWS_PER_BLOCK * GRID_W
KV_WINDOW_ROWS = ROWS_PER_BLOCK + WIN_ROWS
FF_CHUNK = 512
BF16_SUBLANES = 16
HALO_ROWS = BF16_SUBLANES
VMEM_LIMIT = 56 * 1024 * 1024


def _const_spec(shape):
    zeros = (0,) * len(shape)
    return pl.BlockSpec(shape, lambda *_: zeros, pipeline_mode=pl.Buffered(1))


def _inproj_kernel(n_cast, x_ref, g1_ref, w_ref, gq_ref, gk_ref, *refs):
    cast_in, refs = refs[:n_cast], refs[n_cast:]
    qe_ref, qo_ref, k_ref, v_ref, cb_ref, z_ref, sga_ref, sgb_ref = refs[:8]
    cast_out = refs[8:]
    for src, dst in zip(cast_in, cast_out):
        dst[...] = src[...].astype(BF16)

    xf = x_ref[...]
    ms = jnp.mean(xf * xf, axis=-1, keepdims=True)
    u = ((xf * lax.rsqrt(ms + EPS)) * g1_ref[...]).astype(BF16)

    def proj(c0, width):
        return jnp.dot(u, w_ref[:, c0:c0 + width], preferred_element_type=F32)

    first_head = lax.broadcasted_iota(jnp.int32, (TOK_TILE, LANES), 1) < HEAD_DIM

    def head_norm(t, g):
        outs = []
        for p in range(HEAD_PAIRS):
            tp = t[:, p * LANES:(p + 1) * LANES]
            t2 = tp * tp
            ss_a = jnp.sum(jnp.where(first_head, t2, 0.0), axis=-1, keepdims=True)
            ss_b = jnp.sum(jnp.where(first_head, 0.0, t2), axis=-1, keepdims=True)
            r_a = lax.rsqrt(ss_a * (1.0 / HEAD_DIM) + EPS)
            r_b = lax.rsqrt(ss_b * (1.0 / HEAD_DIM) + EPS)
            outs.append(tp * jnp.where(first_head, r_a, r_b))
        return jnp.concatenate(outs, axis=-1) * g

    lane = lax.broadcasted_iota(jnp.int32, (TOK_TILE, ATTN_WIDTH), 1)
    even_head = (lane % LANES) < HEAD_DIM

    c0 = 3 * ATTN_WIDTH
    g0 = c0 + 3 * CONV_WIDTH
    sga_ref[...] = jax.nn.sigmoid(proj(g0, D_MODEL)).astype(BF16)
    sgb_ref[...] = jax.nn.sigmoid(proj(g0 + D_MODEL, D_MODEL)).astype(BF16)
    q = head_norm(proj(0, ATTN_WIDTH), gq_ref[...]) * (HEAD_DIM ** -0.5)
    qe_ref[...] = jnp.where(even_head, q, 0.0).astype(BF16)
    qo_ref[...] = jnp.where(even_head, 0.0, q).astype(BF16)
    k_ref[...] = head_norm(proj(ATTN_WIDTH, ATTN_WIDTH), gk_ref[...]).astype(BF16)
    z_ref[...] = (proj(c0 + CONV_WIDTH, CONV_WIDTH)
                  * proj(c0 + 2 * CONV_WIDTH, CONV_WIDTH)).astype(BF16)
    cb_ref[...] = proj(c0, CONV_WIDTH).astype(BF16)
    v_ref[...] = proj(2 * ATTN_WIDTH, ATTN_WIDTH).astype(BF16)


def _inproj(x2d, g1, w_in, gq, gk, layer, later_weights):
    m = x2d.shape[0]
    n_steps = m // TOK_TILE
    tok = lambda w: pl.BlockSpec((TOK_TILE, w), lambda i: (i, 0))
    out_widths = [ATTN_WIDTH] * 4 + [CONV_WIDTH] * 2 + [D_MODEL] * 2
    cast_in_specs, cast_out_specs = [], []
    for w in later_weights:
        rows = w.shape[1] // n_steps
        assert rows * n_steps == w.shape[1] and rows % BF16_SUBLANES == 0, w.shape
        cast_in_specs.append(pl.BlockSpec((None, rows, w.shape[2]), lambda i: (layer, i, 0)))
        cast_out_specs.append(pl.BlockSpec((rows, w.shape[2]), lambda i: (i, 0)))
    return pl.pallas_call(
        functools.partial(_inproj_kernel, len(later_weights)),
        grid=(n_steps,),
        in_specs=[tok(D_MODEL), _const_spec((1, D_MODEL)), _const_spec((D_MODEL, PROJ_WIDTH)),
                  _const_spec((1, ATTN_WIDTH)), _const_spec((1, ATTN_WIDTH))] + cast_in_specs,
        out_specs=[tok(w) for w in out_widths] + cast_out_specs,
        out_shape=([jax.ShapeDtypeStruct((m, w), BF16) for w in out_widths]
                   + [jax.ShapeDtypeStruct(w.shape[1:], BF16) for w in later_weights]),
        compiler_params=pltpu.CompilerParams(
            dimension_semantics=("arbitrary",), vmem_limit_bytes=VMEM_LIMIT),
        name="inproj",
    )(x2d, g1, w_in, gq, gk, *later_weights)


def _mixer_kernel(rows_per_batch,
                  qe_ref, qo_ref, kw_ref, vw_ref,
                  cb_ref, z_ref, zp_ref, zn_ref, sga_ref, sgb_ref, x_ref,
                  g_ref, cw_ref, wa_ref, wb_ref, wo_ref,
                  o_ref, attn_buf, bb_buf, s_buf, tp_ref):
    blk = pl.program_id(1)
    n_blk = pl.num_programs(1)

    @pl.when(jnp.logical_and(pl.program_id(0) == 0, blk == 0))
    def _():
        _build_bias_table(g_ref, tp_ref)

    lane = lax.broadcasted_iota(jnp.int32, (GRID_W, LANES), 1)
    first_head = lane < HEAD_DIM
    n_keys = WIN_ROWS * GRID_W
    win_row0 = jnp.clip(blk * ROWS_PER_BLOCK - WIN_ROWS // 2, 0, rows_per_batch - KV_WINDOW_ROWS)

    def window(j):
        r = blk * ROWS_PER_BLOCK + j
        row_start = jnp.clip(r - WIN_ROWS // 2, 0, rows_per_batch - WIN_ROWS)
        ks = pl.multiple_of((row_start - win_row0) * GRID_W, GRID_W)
        return ks, r - row_start

    def scores(j, p):
        ks, d = window(j)
        cols = slice(p * LANES, (p + 1) * LANES)
        rows = slice(j * GRID_W, (j + 1) * GRID_W)
        q2 = jnp.concatenate([qe_ref[rows, cols], qo_ref[rows, cols]], axis=0)
        s = lax.dot_general(q2, kw_ref[0, pl.ds(ks, n_keys), cols], (((1,), (1,)), ((), ())),
                            preferred_element_type=F32)
        bias = jnp.concatenate(
            [tp_ref[p, WIN_ROWS - 1 - d + i] for i in range(0, WIN_ROWS, 2)], axis=-1)
        s_buf[j % 2, p] = s + bias

    def attend(j, p):
        ks, _ = window(j)
        cols = slice(p * LANES, (p + 1) * LANES)
        rows = slice(j * GRID_W, (j + 1) * GRID_W)
        s = s_buf[j % 2, p]
        m = jnp.max(s, axis=-1, keepdims=True)
        e = jnp.exp(s - m)
        l = jnp.sum(e, axis=-1, keepdims=True)
        o = jnp.dot(e.astype(BF16), vw_ref[0, pl.ds(ks, n_keys), cols],
                    preferred_element_type=F32)
        o = o / l
        attn_buf[rows, cols] = jnp.where(first_head, o[:GRID_W], o[GRID_W:]).astype(BF16)

    z_before = jnp.where(blk > 0, zp_ref[HALO_ROWS - 1:HALO_ROWS, :].astype(F32), 0.0)
    z_after = jnp.where(blk < n_blk - 1, zn_ref[0:1, :].astype(F32), 0.0)
    row_in_slice = lax.broadcasted_iota(jnp.int32, (GRID_W, CONV_WIDTH), 0)

    def gated_conv(j):
        lo, hi = j * GRID_W, (j + 1) * GRID_W
        zf = z_ref[lo:hi, :].astype(F32)
        before = z_before if j == 0 else z_ref[lo - 1:lo, :].astype(F32)
        after = z_after if j == ROWS_PER_BLOCK - 1 else z_ref[hi:hi + 1, :].astype(F32)
        z_m1 = jnp.where(row_in_slice == 0, before, pltpu.roll(zf, 1, 0))
        z_p1 = jnp.where(row_in_slice == GRID_W - 1, after, pltpu.roll(zf, GRID_W - 1, 0))
        conv = z_m1 * cw_ref[0:1, :] + zf * cw_ref[1:2, :] + z_p1 * cw_ref[2:3, :]
        bb_buf[lo:hi, :] = (cb_ref[lo:hi, :].astype(F32) * conv).astype(BF16)

    for p in range(HEAD_PAIRS):
        scores(0, p)
    for j in range(ROWS_PER_BLOCK):
        for p in range(HEAD_PAIRS):
            if j + 1 < ROWS_PER_BLOCK:
                scores(j + 1, p)
            attend(j, p)
        gated_conv(j)

    ya = jnp.dot(attn_buf[...], wa_ref[...], preferred_element_type=F32)
    yb = jnp.dot(bb_buf[...], wb_ref[...], preferred_element_type=F32)
    merged = (sga_ref[...].astype(F32) * ya + sgb_ref[...].astype(F32) * yb).astype(BF16)
    o_ref[...] = x_ref[...] + jnp.dot(merged, wo_ref[...], preferred_element_type=F32)


def _kv_window_start(blk, rows_per_batch):
    row0 = jnp.clip(blk * ROWS_PER_BLOCK - WIN_ROWS // 2, 0, rows_per_batch - KV_WINDOW_ROWS)
    return row0 * GRID_W


def _mixer(batch, seq, qe, qo, k, v, cb, z, sga, sgb, x, bias_rows, cw, wa, wb, wo):
    rows = seq // GRID_W
    n_blk = rows // ROWS_PER_BLOCK
    halo_per_blk = MIX_TILE // HALO_ROWS
    n_halo = seq // HALO_ROWS

    def r3(a):
        return a.reshape(batch, seq, a.shape[-1])

    cur = lambda w: pl.BlockSpec((None, MIX_TILE, w), lambda b, i: (b, i, 0))
    kv_window = pl.BlockSpec(
        (pl.Element(1), pl.Element(KV_WINDOW_ROWS * GRID_W), pl.Element(ATTN_WIDTH)),
        lambda b, i: (b, _kv_window_start(i, rows), 0))
    halo_prev = pl.BlockSpec((None, HALO_ROWS, CONV_WIDTH),
                             lambda b, i: (b, jnp.maximum(i * halo_per_blk - 1, 0), 0))
    halo_next = pl.BlockSpec((None, HALO_ROWS, CONV_WIDTH),
                             lambda b, i: (b, jnp.minimum((i + 1) * halo_per_blk, n_halo - 1), 0))
    aw, cwid = ATTN_WIDTH, CONV_WIDTH
    in_specs = [cur(aw), cur(aw), kv_window, kv_window,
                cur(cwid), cur(cwid), halo_prev, halo_next, cur(D_MODEL), cur(D_MODEL), cur(D_MODEL),
                _const_spec(bias_rows.shape), _const_spec(cw.shape), _const_spec(wa.shape),
                _const_spec(wb.shape), _const_spec(wo.shape)]
    z3 = r3(z)
    out = pl.pallas_call(
        functools.partial(_mixer_kernel, rows),
        grid=(batch, n_blk),
        in_specs=in_specs,
        out_specs=cur(D_MODEL),
        out_shape=jax.ShapeDtypeStruct((batch, seq, D_MODEL), F32),
        scratch_shapes=[pltpu.VMEM((MIX_TILE, ATTN_WIDTH), BF16),
                        pltpu.VMEM((MIX_TILE, CONV_WIDTH), BF16),
                        pltpu.VMEM((2, HEAD_PAIRS, 2 * GRID_W, WIN_ROWS * GRID_W), F32),
                        pltpu.VMEM((HEAD_PAIRS, N_REL_ROWS - 1, 2 * GRID_W, LANES), F32)],
        compiler_params=pltpu.CompilerParams(
            dimension_semantics=("arbitrary", "arbitrary"), vmem_limit_bytes=VMEM_LIMIT),
        name="mixer",
    )(r3(qe), r3(qo), r3(k), r3(v), r3(cb), z3, z3, z3, r3(sga), r3(sgb), x,
      bias_rows, cw, wa, wb, wo)
    return out


def _padded_bias_rows(rpb):
    pad = GRID_W - WIN_COLS
    return jnp.pad(rpb.astype(F32), ((0, 0), (0, 0), (pad, LANES - N_REL_COLS - pad)))


def _build_bias_table(g_ref, tbl_ref):
    c = lax.broadcasted_iota(jnp.int32, (GRID_W, LANES), 0)
    x = lax.broadcasted_iota(jnp.int32, (GRID_W, LANES), 1)
    kc = x % GRID_W
    win_start = jnp.clip(c - WIN_COLS // 2, 0, GRID_W - WIN_COLS)
    in_window = (kc >= win_start) & (kc < win_start + WIN_COLS)
    left = x < GRID_W
    for h in range(N_HEADS):
        rows = slice((h % 2) * GRID_W, (h % 2 + 1) * GRID_W)
        for rr in range(N_REL_ROWS - 1):
            g0 = jnp.broadcast_to(g_ref[h, rr:rr + 1, :], (GRID_W, LANES))
            g1 = jnp.broadcast_to(g_ref[h, rr + 1:rr + 2, :], (GRID_W, LANES))
            a = pltpu.roll(g0, LANES - (GRID_W - 1), 1, stride=1, stride_axis=0)
            b = pltpu.roll(g1, 1, 1, stride=1, stride_axis=0)
            tbl_ref[h // 2, rr, rows, :] = jnp.where(in_window, jnp.where(left, a, b), NEG_INF)


def _mlp_kernel(x_ref, g_ref, w1_ref, w2_ref, o_ref, h_buf):
    xf = x_ref[...]
    ms = jnp.mean(xf * xf, axis=-1, keepdims=True)
    u = ((xf * lax.rsqrt(ms + EPS)) * g_ref[...]).astype(BF16)
    for c in range(D_FF // FF_CHUNK):
        cols = slice(c * FF_CHUNK, (c + 1) * FF_CHUNK)
        h = jnp.dot(u, w1_ref[:, cols], preferred_element_type=F32)
        h_buf[:, cols] = jnp.square(jnp.maximum(h, 0.0)).astype(BF16)
    o_ref[...] = xf + jnp.dot(h_buf[...], w2_ref[...], preferred_element_type=F32)


def _mlp(x2d, g2, w1, w2):
    m = x2d.shape[0]
    tok = pl.BlockSpec((TOK_TILE, D_MODEL), lambda i: (i, 0))
    return pl.pallas_call(
        _mlp_kernel,
        grid=(m // TOK_TILE,),
        in_specs=[tok, _const_spec((1, D_MODEL)), _const_spec(w1.shape), _const_spec(w2.shape)],
        out_specs=tok,
        out_shape=jax.ShapeDtypeStruct((m, D_MODEL), F32),
        scratch_shapes=[pltpu.VMEM((TOK_TILE, D_FF), BF16)],
        compiler_params=pltpu.CompilerParams(
            dimension_semantics=("arbitrary",), vmem_limit_bytes=VMEM_LIMIT),
        name="mlp",
    )(x2d, g2, w1, w2)


def kernel(x, norm1_g, w_in, q_norm_g, k_norm_g, rpb, conv_w, w_attn_branch, w_conv_branch,
           w_o, norm2_g, w_mlp_in, w_mlp_out):
    batch, seq, _ = x.shape
    depth = w_in.shape[0]
    for l in range(depth):
        x2d = x.reshape(batch * seq, D_MODEL)
        gq = jnp.tile(q_norm_g[l], N_HEADS)[None, :]
        gk = jnp.tile(k_norm_g[l], N_HEADS)[None, :]
        qe, qo, k, v, cb, z, sga, sgb, wa, wb, wo, w1, w2 = _inproj(
            x2d, norm1_g[l][None, :], w_in[l].astype(BF16), gq, gk, l,
            (w_attn_branch, w_conv_branch, w_o, w_mlp_in, w_mlp_out))
        x = _mixer(batch, seq, qe, qo, k, v, cb, z, sga, sgb, x,
                   _padded_bias_rows(rpb[l]), conv_w[l], wa, wb, wo)
        x = _mlp(x.reshape(batch * seq, D_MODEL), norm2_g[l][None, :], w1, w2
                 ).reshape(batch, seq, D_MODEL)
    return x
```

```python
import functools

import numpy as np
import jax
import jax.numpy as jnp
from jax import lax
from jax.experimental import pallas as pl
from jax.experimental.pallas import tpu as pltpu

F32 = jnp.float32
BF16 = jnp.bfloat16

D_MODEL = 1024
GRID_W = 64
N_HEADS = 8
HEAD_DIM = 64
ATTN_WIDTH = N_HEADS * HEAD_DIM
CONV_WIDTH = D_MODEL // 2
WIN_ROWS = 8
WIN_COLS = 16
D_FF = 4 * D_MODEL
EPS = 1e-6
NEG_INF = -1e30
LOG2E = 1.4426950408889634
PROJ_WIDTH = 3 * ATTN_WIDTH + 3 * CONV_WIDTH + 2 * D_MODEL

LANES = 128
HEAD_PAIRS = ATTN_WIDTH // LANES
N_REL_ROWS = 2 * WIN_ROWS - 1
N_REL_COLS = 2 * WIN_COLS - 1

TOK_TILE = 1024
ROWS_PER_BLOCK = 8
MIX_TILE = ROWS_PER_BLOCK * GRID_W
KV_WINDOW_ROWS = ROWS_PER_BLOCK + WIN_ROWS
FF_CHUNK = 512
BF16_SUBLANES = 16
HALO_ROWS = BF16_SUBLANES
VMEM_LIMIT = 56 * 1024 * 1024


def _const_spec(shape):
    zeros = (0,) * len(shape)
    return pl.BlockSpec(shape, lambda *_: zeros, pipeline_mode=pl.Buffered(1))


def _inproj_kernel(n_cast, x_ref, g1_ref, w_ref, gq_ref, gk_ref, *refs):
    cast_in, refs = refs[:n_cast], refs[n_cast:]
    qe_ref, qo_ref, k_ref, v_ref, cb_ref, z_ref, sga_ref, sgb_ref = refs[:8]
    cast_out = refs[8:]
    for src, dst in zip(cast_in, cast_out):
        dst[...] = src[...].astype(BF16)

    xf = x_ref[...]
    ms = jnp.mean(xf * xf, axis=-1, keepdims=True)
    u = ((xf * lax.rsqrt(ms + EPS)) * g1_ref[...]).astype(BF16)

    def proj(c0, width):
        return jnp.dot(u, w_ref[:, c0:c0 + width], preferred_element_type=F32)

    first_head = lax.broadcasted_iota(jnp.int32, (TOK_TILE, LANES), 1) < HEAD_DIM

    def head_norm(t, g):
        outs = []
        for p in range(HEAD_PAIRS):
            tp = t[:, p * LANES:(p + 1) * LANES]
            t2 = tp * tp
            ss_a = jnp.sum(jnp.where(first_head, t2, 0.0), axis=-1, keepdims=True)
            ss_b = jnp.sum(jnp.where(first_head, 0.0, t2), axis=-1, keepdims=True)
            r_a = lax.rsqrt(ss_a * (1.0 / HEAD_DIM) + EPS)
            r_b = lax.rsqrt(ss_b * (1.0 / HEAD_DIM) + EPS)
            outs.append(tp * jnp.where(first_head, r_a, r_b))
        return jnp.concatenate(outs, axis=-1) * g

    lane = lax.broadcasted_iota(jnp.int32, (TOK_TILE, ATTN_WIDTH), 1)
    even_head = (lane % LANES) < HEAD_DIM

    c0 = 3 * ATTN_WIDTH
    g0 = c0 + 3 * CONV_WIDTH
    sga_ref[...] = jax.nn.sigmoid(proj(g0, D_MODEL)).astype(BF16)
    sgb_ref[...] = jax.nn.sigmoid(proj(g0 + D_MODEL, D_MODEL)).astype(BF16)
    q = head_norm(proj(0, ATTN_WIDTH), gq_ref[...]) * (LOG2E * HEAD_DIM ** -0.5)
    qe_ref[...] = jnp.where(even_head, q, 0.0).astype(BF16)
    qo_ref[...] = jnp.where(even_head, 0.0, q).astype(BF16)
    k_ref[...] = head_norm(proj(ATTN_WIDTH, ATTN_WIDTH), gk_ref[...]).astype(BF16)
    z_ref[...] = (proj(c0 + CONV_WIDTH, CONV_WIDTH)
                  * proj(c0 + 2 * CONV_WIDTH, CONV_WIDTH)).astype(BF16)
    cb_ref[...] = proj(c0, CONV_WIDTH).astype(BF16)
    v_ref[...] = proj(2 * ATTN_WIDTH, ATTN_WIDTH).astype(BF16)


def _inproj(x2d, g1, w_in, gq, gk, layer, later_weights):
    m = x2d.shape[0]
    n_steps = m // TOK_TILE
    tok = lambda w: pl.BlockSpec((TOK_TILE, w), lambda i: (i, 0))
    out_widths = [ATTN_WIDTH] * 4 + [CONV_WIDTH] * 2 + [D_MODEL] * 2
    cast_in_specs, cast_out_specs = [], []
    for w in later_weights:
        rows = w.shape[1] // n_steps
        assert rows * n_steps == w.shape[1] and rows % BF16_SUBLANES == 0, w.shape
        cast_in_specs.append(pl.BlockSpec((None, rows, w.shape[2]), lambda i: (layer, i, 0)))
        cast_out_specs.append(pl.BlockSpec((rows, w.shape[2]), lambda i: (i, 0)))
    return pl.pallas_call(
        functools.partial(_inproj_kernel, len(later_weights)),
        grid=(n_steps,),
        in_specs=[tok(D_MODEL), _const_spec((1, D_MODEL)), _const_spec((D_MODEL, PROJ_WIDTH)),
                  _const_spec((1, ATTN_WIDTH)), _const_spec((1, ATTN_WIDTH))] + cast_in_specs,
        out_specs=[tok(w) for w in out_widths] + cast_out_specs,
        out_shape=([jax.ShapeDtypeStruct((m, w), BF16) for w in out_widths]
                   + [jax.ShapeDtypeStruct(w.shape[1:], BF16) for w in later_weights]),
        compiler_params=pltpu.CompilerParams(
            dimension_semantics=("arbitrary",), vmem_limit_bytes=VMEM_LIMIT),
        name="inproj",
    )(x2d, g1, w_in, gq, gk, *later_weights)


def _mixer_kernel(rows_per_batch,
                  qe_ref, qo_ref, kw_ref, vw_ref,
                  cb_ref, z_ref, zp_ref, zn_ref, sga_ref, sgb_ref, x_ref,
                  g_ref, cw_ref, wa_ref, wb_ref, wo_ref,
                  o_ref, attn_buf, bb_buf, tp_ref):
    blk = pl.program_id(1)
    n_blk = pl.num_programs(1)

    @pl.when(jnp.logical_and(pl.program_id(0) == 0, blk == 0))
    def _():
        _build_bias_table(g_ref, tp_ref)

    lane = lax.broadcasted_iota(jnp.int32, (GRID_W, LANES), 1)
    first_head = lane < HEAD_DIM
    n_keys = WIN_ROWS * GRID_W
    win_row0 = jnp.clip(blk * ROWS_PER_BLOCK - WIN_ROWS // 2, 0, rows_per_batch - KV_WINDOW_ROWS)

    def window(j):
        r = blk * ROWS_PER_BLOCK + j
        row_start = jnp.clip(r - WIN_ROWS // 2, 0, rows_per_batch - WIN_ROWS)
        ks = pl.multiple_of((row_start - win_row0) * GRID_W, GRID_W)
        return ks, r - row_start

    def scores(j, p):
        ks, d = window(j)
        cols = slice(p * LANES, (p + 1) * LANES)
        rows = slice(j * GRID_W, (j + 1) * GRID_W)
        q2 = jnp.concatenate([qe_ref[rows, cols], qo_ref[rows, cols]], axis=0)
        s = lax.dot_general(q2, kw_ref[0, pl.ds(ks, n_keys), cols], (((1,), (1,)), ((), ())),
                            preferred_element_type=F32)
        bias = jnp.concatenate(
            [tp_ref[p, WIN_ROWS - 1 - d + i] for i in range(0, WIN_ROWS, 2)], axis=-1)
        return s + bias

    def attend(j, p, s):
        ks, _ = window(j)
        cols = slice(p * LANES, (p + 1) * LANES)
        rows = slice(j * GRID_W, (j + 1) * GRID_W)
        m = jnp.max(s, axis=-1, keepdims=True)
        e = jnp.exp2(s - m)
        l = jnp.sum(e, axis=-1, keepdims=True)
        o = jnp.dot(e.astype(BF16), vw_ref[0, pl.ds(ks, n_keys), cols],
                    preferred_element_type=F32)
        o = o / l
        attn_buf[rows, cols] = jnp.where(first_head, o[:GRID_W], o[GRID_W:]).astype(BF16)

    z_before = jnp.where(blk > 0, zp_ref[HALO_ROWS - 1:HALO_ROWS, :].astype(F32), 0.0)
    z_after = jnp.where(blk < n_blk - 1, zn_ref[0:1, :].astype(F32), 0.0)
    row_in_slice = lax.broadcasted_iota(jnp.int32, (GRID_W, CONV_WIDTH), 0)

    def gated_conv(j):
        lo, hi = j * GRID_W, (j + 1) * GRID_W
        zf = z_ref[lo:hi, :].astype(F32)
        before = z_before if j == 0 else z_ref[lo - 1:lo, :].astype(F32)
        after = z_after if j == ROWS_PER_BLOCK - 1 else z_ref[hi:hi + 1, :].astype(F32)
        z_m1 = jnp.where(row_in_slice == 0, before, pltpu.roll(zf, 1, 0))
        z_p1 = jnp.where(row_in_slice == GRID_W - 1, after, pltpu.roll(zf, GRID_W - 1, 0))
        conv = z_m1 * cw_ref[0:1, :] + zf * cw_ref[1:2, :] + z_p1 * cw_ref[2:3, :]
        bb_buf[lo:hi, :] = (cb_ref[lo:hi, :].astype(F32) * conv).astype(BF16)

    pending = [scores(0, p) for p in range(HEAD_PAIRS)]
    for j in range(ROWS_PER_BLOCK):
        for p in range(HEAD_PAIRS):
            s = pending[p]
            if j + 1 < ROWS_PER_BLOCK:
                pending[p] = scores(j + 1, p)
            attend(j, p, s)
        gated_conv(j)

    ya = jnp.dot(attn_buf[...], wa_ref[...], preferred_element_type=F32)
    yb = jnp.dot(bb_buf[...], wb_ref[...], preferred_element_type=F32)
    merged = sga_ref[...] * ya.astype(BF16) + sgb_ref[...] * yb.astype(BF16)
    o_ref[...] = x_ref[...] + jnp.dot(merged, wo_ref[...], preferred_element_type=F32)


def _kv_window_start(blk, rows_per_batch):
    row0 = jnp.clip(blk * ROWS_PER_BLOCK - WIN_ROWS // 2, 0, rows_per_batch - KV_WINDOW_ROWS)
    return row0 * GRID_W


def _mixer(batch, seq, qe, qo, k, v, cb, z, sga, sgb, x, bias_rows, cw, wa, wb, wo):
    rows = seq // GRID_W
    n_blk = rows // ROWS_PER_BLOCK
    halo_per_blk = MIX_TILE // HALO_ROWS
    n_halo = seq // HALO_ROWS

    def r3(a):
        return a.reshape(batch, seq, a.shape[-1])

    cur = lambda w: pl.BlockSpec((None, MIX_TILE, w), lambda b, i: (b, i, 0))
    kv_window = pl.BlockSpec(
        (pl.Element(1), pl.Element(KV_WINDOW_ROWS * GRID_W), pl.Element(ATTN_WIDTH)),
        lambda b, i: (b, _kv_window_start(i, rows), 0))
    halo_prev = pl.BlockSpec((None, HALO_ROWS, CONV_WIDTH),
                             lambda b, i: (b, jnp.maximum(i * halo_per_blk - 1, 0), 0))
    halo_next = pl.BlockSpec((None, HALO_ROWS, CONV_WIDTH),
                             lambda b, i: (b, jnp.minimum((i + 1) * halo_per_blk, n_halo - 1), 0))
    aw, cwid = ATTN_WIDTH, CONV_WIDTH
    in_specs = [cur(aw), cur(aw), kv_window, kv_window,
                cur(cwid), cur(cwid), halo_prev, halo_next, cur(D_MODEL), cur(D_MODEL), cur(D_MODEL),
                _const_spec(bias_rows.shape), _const_spec(cw.shape), _const_spec(wa.shape),
                _const_spec(wb.shape), _const_spec(wo.shape)]
    z3 = r3(z)
    out = pl.pallas_call(
        functools.partial(_mixer_kernel, rows),
        grid=(batch, n_blk),
        in_specs=in_specs,
        out_specs=cur(D_MODEL),
        out_shape=jax.ShapeDtypeStruct((batch, seq, D_MODEL), F32),
        scratch_shapes=[pltpu.VMEM((MIX_TILE, ATTN_WIDTH), BF16),
                        pltpu.VMEM((MIX_TILE, CONV_WIDTH), BF16),
                        pltpu.VMEM((HEAD_PAIRS, N_REL_ROWS - 1, 2 * GRID_W, LANES), F32)],
        compiler_params=pltpu.CompilerParams(
            dimension_semantics=("arbitrary", "arbitrary"), vmem_limit_bytes=VMEM_LIMIT),
        name="mixer",
    )(r3(qe), r3(qo), r3(k), r3(v), r3(cb), z3, z3, z3, r3(sga), r3(sgb), x,
      bias_rows, cw, wa, wb, wo)
    return out


def _padded_bias_rows(rpb):
    pad = GRID_W - WIN_COLS
    return jnp.pad(rpb.astype(F32), ((0, 0), (0, 0), (pad, LANES - N_REL_COLS - pad)))


def _build_bias_table(g_ref, tbl_ref):
    c = lax.broadcasted_iota(jnp.int32, (GRID_W, LANES), 0)
    x = lax.broadcasted_iota(jnp.int32, (GRID_W, LANES), 1)
    kc = x % GRID_W
    win_start = jnp.clip(c - WIN_COLS // 2, 0, GRID_W - WIN_COLS)
    in_window = (kc >= win_start) & (kc < win_start + WIN_COLS)
    left = x < GRID_W
    for h in range(N_HEADS):
        rows = slice((h % 2) * GRID_W, (h % 2 + 1) * GRID_W)
        for rr in range(N_REL_ROWS - 1):
            g0 = jnp.broadcast_to(g_ref[h, rr:rr + 1, :] * LOG2E, (GRID_W, LANES))
            g1 = jnp.broadcast_to(g_ref[h, rr + 1:rr + 2, :] * LOG2E, (GRID_W, LANES))
            a = pltpu.roll(g0, LANES - (GRID_W - 1), 1, stride=1, stride_axis=0)
            b = pltpu.roll(g1, 1, 1, stride=1, stride_axis=0)
            tbl_ref[h // 2, rr, rows, :] = jnp.where(in_window, jnp.where(left, a, b), NEG_INF)


def _mlp_kernel(x_ref, g_ref, w1_ref, w2_ref, o_ref, h_buf):
    xf = x_ref[...]
    ms = jnp.mean(xf * xf, axis=-1, keepdims=True)
    u = ((xf * lax.rsqrt(ms + EPS)) * g_ref[...]).astype(BF16)
    for c in range(D_FF // FF_CHUNK):
        cols = slice(c * FF_CHUNK, (c + 1) * FF_CHUNK)
        h = jnp.dot(u, w1_ref[:, cols], preferred_element_type=F32)
        h_buf[:, cols] = jnp.square(jnp.maximum(h, 0.0)).astype(BF16)
    o_ref[...] = xf + jnp.dot(h_buf[...], w2_ref[...], preferred_element_type=F32)


def _mlp(x2d, g2, w1, w2):
    m = x2d.shape[0]
    tok = pl.BlockSpec((TOK_TILE, D_MODEL), lambda i: (i, 0))
    return pl.pallas_call(
        _mlp_kernel,
        grid=(m // TOK_TILE,),
        in_specs=[tok, _const_spec((1, D_MODEL)), _const_spec(w1.shape), _const_spec(w2.shape)],
        out_specs=tok,
        out_shape=jax.ShapeDtypeStruct((m, D_MODEL), F32),
        scratch_shapes=[pltpu.VMEM((TOK_TILE, D_FF), BF16)],
        compiler_params=pltpu.CompilerParams(
            dimension_semantics=("arbitrary",), vmem_limit_bytes=VMEM_LIMIT),
        name="mlp",
    )(x2d, g2, w1, w2)


def kernel(x, norm1_g, w_in, q_norm_g, k_norm_g, rpb, conv_w, w_attn_branch, w_conv_branch,
           w_o, norm2_g, w_mlp_in, w_mlp_out):
    batch, seq, _ = x.shape
    depth = w_in.shape[0]
    for l in range(depth):
        x2d = x.reshape(batch * seq, D_MODEL)
        gq = jnp.tile(q_norm_g[l], N_HEADS)[None, :]
        gk = jnp.tile(k_norm_g[l], N_HEADS)[None, :]
        qe, qo, k, v, cb, z, sga, sgb, wa, wb, wo, w1, w2 = _inproj(
            x2d, norm1_g[l][None, :], w_in[l].astype(BF16), gq, gk, l,
            (w_attn_branch, w_conv_branch, w_o, w_mlp_in, w_mlp_out))
        x = _mixer(batch, seq, qe, qo, k, v, cb, z, sga, sgb, x,
                   _padded_bias_rows(rpb[l]), conv_w[l], wa, wb, wo)
        x = _mlp(x.reshape(batch * seq, D_MODEL), norm2_g[l][None, :], w1, w2
                 ).reshape(batch, seq, D_MODEL)
    return x
```

```python
import functools

import numpy as np
import jax
import jax.numpy as jnp
from jax import lax
from jax.experimental import pallas as pl
from jax.experimental.pallas import tpu as pltpu

F32 = jnp.float32
BF16 = jnp.bfloat16

D_MODEL = 1024
GRID_W = 64
N_HEADS = 8
HEAD_DIM = 64
ATTN_WIDTH = N_HEADS * HEAD_DIM
CONV_WIDTH = D_MODEL // 2
WIN_ROWS = 8
WIN_COLS = 16
D_FF = 4 * D_MODEL
EPS = 1e-6
NEG_INF = -1e30
LOG2E = 1.4426950408889634
PROJ_WIDTH = 3 * ATTN_WIDTH + 3 * CONV_WIDTH + 2 * D_MODEL

LANES = 128
HEAD_PAIRS = ATTN_WIDTH // LANES
N_REL_ROWS = 2 * WIN_ROWS - 1
N_REL_COLS = 2 * WIN_COLS - 1

TOK_TILE = 1024
ROWS_PER_BLOCK = 8
MIX_TILE = ROWS_PER_BLOCK * GRID_W
KV_WINDOW_ROWS = ROWS_PER_BLOCK + WIN_ROWS
FF_CHUNK = 512
BF16_SUBLANES = 16
HALO_ROWS = BF16_SUBLANES
VMEM_LIMIT = 56 * 1024 * 1024
INPROJ_VMEM_LIMIT = 61 * 1024 * 1024


def _const_spec(shape):
    zeros = (0,) * len(shape)
    return pl.BlockSpec(shape, lambda *_: zeros, pipeline_mode=pl.Buffered(1))


def _sigmoid(x):
    return 0.5 * jnp.tanh(0.5 * x) + 0.5


def _inproj_kernel(n_cast, x_ref, g1_ref, w_ref, gq_ref, gk_ref, *refs):
    cast_in, refs = refs[:n_cast], refs[n_cast:]
    qe_ref, qo_ref, k_ref, v_ref, cb_ref, z_ref, sga_ref, sgb_ref = refs[:8]
    cast_out = refs[8:]
    for src, dst in zip(cast_in, cast_out):
        dst[...] = src[...].astype(BF16)

    xf = x_ref[...]
    ms = jnp.mean(xf * xf, axis=-1, keepdims=True)
    u = ((xf * lax.rsqrt(ms + EPS)) * g1_ref[...]).astype(BF16)

    def proj(c0, width):
        return jnp.dot(u, w_ref[:, c0:c0 + width].astype(BF16), preferred_element_type=F32)

    first_head = lax.broadcasted_iota(jnp.int32, (TOK_TILE, LANES), 1) < HEAD_DIM

    def head_norm(t, g):
        outs = []
        for p in range(HEAD_PAIRS):
            tp = t[:, p * LANES:(p + 1) * LANES]
            t2 = tp * tp
            ss_a = jnp.sum(jnp.where(first_head, t2, 0.0), axis=-1, keepdims=True)
            ss_b = jnp.sum(jnp.where(first_head, 0.0, t2), axis=-1, keepdims=True)
            r_a = lax.rsqrt(ss_a * (1.0 / HEAD_DIM) + EPS)
            r_b = lax.rsqrt(ss_b * (1.0 / HEAD_DIM) + EPS)
            outs.append(tp * jnp.where(first_head, r_a, r_b))
        return jnp.concatenate(outs, axis=-1) * g

    lane = lax.broadcasted_iota(jnp.int32, (TOK_TILE, ATTN_WIDTH), 1)
    even_head = (lane % LANES) < HEAD_DIM

    c0 = 3 * ATTN_WIDTH
    g0 = c0 + 3 * CONV_WIDTH
    sga_ref[...] = _sigmoid(proj(g0, D_MODEL)).astype(BF16)
    sgb_ref[...] = _sigmoid(proj(g0 + D_MODEL, D_MODEL)).astype(BF16)
    q = head_norm(proj(0, ATTN_WIDTH), gq_ref[...]) * (LOG2E * HEAD_DIM ** -0.5)
    qe_ref[...] = jnp.where(even_head, q, 0.0).astype(BF16)
    qo_ref[...] = jnp.where(even_head, 0.0, q).astype(BF16)
    k_ref[...] = head_norm(proj(ATTN_WIDTH, ATTN_WIDTH), gk_ref[...]).astype(BF16)
    z_ref[...] = (proj(c0 + CONV_WIDTH, CONV_WIDTH)
                  * proj(c0 + 2 * CONV_WIDTH, CONV_WIDTH)).astype(BF16)
    cb_ref[...] = proj(c0, CONV_WIDTH).astype(BF16)
    v_ref[...] = proj(2 * ATTN_WIDTH, ATTN_WIDTH).astype(BF16)


def _inproj(x2d, g1, w_in, gq, gk, layer, later_weights):
    m = x2d.shape[0]
    n_steps = m // TOK_TILE
    tok = lambda w: pl.BlockSpec((TOK_TILE, w), lambda i: (i, 0))
    out_widths = [ATTN_WIDTH] * 4 + [CONV_WIDTH] * 2 + [D_MODEL] * 2
    cast_in_specs, cast_out_specs = [], []
    for w in later_weights:
        rows = w.shape[1] // n_steps
        assert rows * n_steps == w.shape[1] and rows % BF16_SUBLANES == 0, w.shape
        cast_in_specs.append(pl.BlockSpec((None, rows, w.shape[2]), lambda i: (layer, i, 0)))
        cast_out_specs.append(pl.BlockSpec((rows, w.shape[2]), lambda i: (i, 0)))
    return pl.pallas_call(
        functools.partial(_inproj_kernel, len(later_weights)),
        grid=(n_steps,),
        in_specs=[tok(D_MODEL), _const_spec((1, D_MODEL)),
                  pl.BlockSpec((None, D_MODEL, PROJ_WIDTH), lambda i: (layer, 0, 0),
                               pipeline_mode=pl.Buffered(1)),
                  _const_spec((1, ATTN_WIDTH)), _const_spec((1, ATTN_WIDTH))] + cast_in_specs,
        out_specs=[tok(w) for w in out_widths] + cast_out_specs,
        out_shape=([jax.ShapeDtypeStruct((m, w), BF16) for w in out_widths]
                   + [jax.ShapeDtypeStruct(w.shape[1:], BF16) for w in later_weights]),
        compiler_params=pltpu.CompilerParams(
            dimension_semantics=("arbitrary",), vmem_limit_bytes=INPROJ_VMEM_LIMIT),
        name="inproj",
    )(x2d, g1, w_in, gq, gk, *later_weights)


def _mixer_kernel(rows_per_batch,
                  qe_ref, qo_ref, kw_ref, vw_ref,
                  cb_ref, z_ref, zp_ref, zn_ref, sga_ref, sgb_ref, x_ref,
                  g_ref, cw_ref, wa_ref, wb_ref, wo_ref,
                  o_ref, attn_buf, bb_buf, tp_ref):
    blk = pl.program_id(1)
    n_blk = pl.num_programs(1)

    @pl.when(jnp.logical_and(pl.program_id(0) == 0, blk == 0))
    def _():
        _build_bias_table(g_ref, tp_ref)

    lane = lax.broadcasted_iota(jnp.int32, (GRID_W, LANES), 1)
    first_head = lane < HEAD_DIM
    n_keys = WIN_ROWS * GRID_W
    win_row0 = jnp.clip(blk * ROWS_PER_BLOCK - WIN_ROWS // 2, 0, rows_per_batch - KV_WINDOW_ROWS)

    def window(j):
        r = blk * ROWS_PER_BLOCK + j
        row_start = jnp.clip(r - WIN_ROWS // 2, 0, rows_per_batch - WIN_ROWS)
        ks = pl.multiple_of((row_start - win_row0) * GRID_W, GRID_W)
        return ks, r - row_start

    def scores(j, p):
        ks, d = window(j)
        cols = slice(p * LANES, (p + 1) * LANES)
        rows = slice(j * GRID_W, (j + 1) * GRID_W)
        q2 = jnp.concatenate([qe_ref[rows, cols], qo_ref[rows, cols]], axis=0)
        s = lax.dot_general(q2, kw_ref[0, pl.ds(ks, n_keys), cols], (((1,), (1,)), ((), ())),
                            preferred_element_type=F32)
        bias = jnp.concatenate(
            [tp_ref[p, WIN_ROWS - 1 - d + i] for i in range(0, WIN_ROWS, 2)], axis=-1)
        return s + bias

    def attend(j, p, s):
        ks, _ = window(j)
        cols = slice(p * LANES, (p + 1) * LANES)
        rows = slice(j * GRID_W, (j + 1) * GRID_W)
        m = jnp.max(s, axis=-1, keepdims=True)
        e = jnp.exp2(s - m)
        l = jnp.sum(e, axis=-1, keepdims=True)
        o = jnp.dot(e.astype(BF16), vw_ref[0, pl.ds(ks, n_keys), cols],
                    preferred_element_type=F32)
        o = o / l
        attn_buf[rows, cols] = jnp.where(first_head, o[:GRID_W], o[GRID_W:]).astype(BF16)

    z_before = jnp.where(blk > 0, zp_ref[HALO_ROWS - 1:HALO_ROWS, :].astype(F32), 0.0)
    z_after = jnp.where(blk < n_blk - 1, zn_ref[0:1, :].astype(F32), 0.0)
    row_in_slice = lax.broadcasted_iota(jnp.int32, (GRID_W, CONV_WIDTH), 0)

    def gated_conv(j):
        lo, hi = j * GRID_W, (j + 1) * GRID_W
        zf = z_ref[lo:hi, :].astype(F32)
        before = z_before if j == 0 else z_ref[lo - 1:lo, :].astype(F32)
        after = z_after if j == ROWS_PER_BLOCK - 1 else z_ref[hi:hi + 1, :].astype(F32)
        z_m1 = jnp.where(row_in_slice == 0, before, pltpu.roll(zf, 1, 0))
        z_p1 = jnp.where(row_in_slice == GRID_W - 1, after, pltpu.roll(zf, GRID_W - 1, 0))
        conv = z_m1 * cw_ref[0:1, :] + zf * cw_ref[1:2, :] + z_p1 * cw_ref[2:3, :]
        bb_buf[lo:hi, :] = (cb_ref[lo:hi, :].astype(F32) * conv).astype(BF16)

    pending = [scores(0, p) for p in range(HEAD_PAIRS)]
    for j in range(ROWS_PER_BLOCK):
        for p in range(HEAD_PAIRS):
            s = pending[p]
            if j + 1 < ROWS_PER_BLOCK:
                pending[p] = scores(j + 1, p)
            attend(j, p, s)
        gated_conv(j)

    ya = jnp.dot(attn_buf[...], wa_ref[...], preferred_element_type=F32)
    yb = jnp.dot(bb_buf[...], wb_ref[...], preferred_element_type=F32)
    merged = sga_ref[...] * ya.astype(BF16) + sgb_ref[...] * yb.astype(BF16)
    o_ref[...] = x_ref[...] + jnp.dot(merged, wo_ref[...], preferred_element_type=F32)


def _kv_window_start(blk, rows_per_batch):
    row0 = jnp.clip(blk * ROWS_PER_BLOCK - WIN_ROWS // 2, 0, rows_per_batch - KV_WINDOW_ROWS)
    return row0 * GRID_W


def _mixer(batch, seq, qe, qo, k, v, cb, z, sga, sgb, x, bias_rows, cw, wa, wb, wo):
    rows = seq // GRID_W
    n_blk = rows // ROWS_PER_BLOCK
    halo_per_blk = MIX_TILE // HALO_ROWS
    n_halo = seq // HALO_ROWS

    def r3(a):
        return a.reshape(batch, seq, a.shape[-1])

    cur = lambda w: pl.BlockSpec((None, MIX_TILE, w), lambda b, i: (b, i, 0))
    kv_window = pl.BlockSpec(
        (pl.Element(1), pl.Element(KV_WINDOW_ROWS * GRID_W), pl.Element(ATTN_WIDTH)),
        lambda b, i: (b, _kv_window_start(i, rows), 0))
    halo_prev = pl.BlockSpec((None, HALO_ROWS, CONV_WIDTH),
                             lambda b, i: (b, jnp.maximum(i * halo_per_blk - 1, 0), 0))
    halo_next = pl.BlockSpec((None, HALO_ROWS, CONV_WIDTH),
                             lambda b, i: (b, jnp.minimum((i + 1) * halo_per_blk, n_halo - 1), 0))
    aw, cwid = ATTN_WIDTH, CONV_WIDTH
    in_specs = [cur(aw), cur(aw), kv_window, kv_window,
                cur(cwid), cur(cwid), halo_prev, halo_next, cur(D_MODEL), cur(D_MODEL), cur(D_MODEL),
                _const_spec(bias_rows.shape), _const_spec(cw.shape), _const_spec(wa.shape),
                _const_spec(wb.shape), _const_spec(wo.shape)]
    z3 = r3(z)
    out = pl.pallas_call(
        functools.partial(_mixer_kernel, rows),
        grid=(batch, n_blk),
        in_specs=in_specs,
        out_specs=cur(D_MODEL),
        out_shape=jax.ShapeDtypeStruct((batch, seq, D_MODEL), F32),
        scratch_shapes=[pltpu.VMEM((MIX_TILE, ATTN_WIDTH), BF16),
                        pltpu.VMEM((MIX_TILE, CONV_WIDTH), BF16),
                        pltpu.VMEM((HEAD_PAIRS, N_REL_ROWS - 1, 2 * GRID_W, LANES), F32)],
        compiler_params=pltpu.CompilerParams(
            dimension_semantics=("arbitrary", "arbitrary"), vmem_limit_bytes=VMEM_LIMIT),
        name="mixer",
    )(r3(qe), r3(qo), r3(k), r3(v), r3(cb), z3, z3, z3, r3(sga), r3(sgb), x,
      bias_rows, cw, wa, wb, wo)
    return out


def _padded_bias_rows(rpb):
    pad = GRID_W - WIN_COLS
    return jnp.pad(rpb.astype(F32), ((0, 0), (0, 0), (pad, LANES - N_REL_COLS - pad)))


def _build_bias_table(g_ref, tbl_ref):
    c = lax.broadcasted_iota(jnp.int32, (GRID_W, LANES), 0)
    x = lax.broadcasted_iota(jnp.int32, (GRID_W, LANES), 1)
    kc = x % GRID_W
    win_start = jnp.clip(c - WIN_COLS // 2, 0, GRID_W - WIN_COLS)
    in_window = (kc >= win_start) & (kc < win_start + WIN_COLS)
    left = x < GRID_W
    for h in range(N_HEADS):
        rows = slice((h % 2) * GRID_W, (h % 2 + 1) * GRID_W)
        for rr in range(N_REL_ROWS - 1):
            g0 = jnp.broadcast_to(g_ref[h, rr:rr + 1, :] * LOG2E, (GRID_W, LANES))
            g1 = jnp.broadcast_to(g_ref[h, rr + 1:rr + 2, :] * LOG2E, (GRID_W, LANES))
            a = pltpu.roll(g0, LANES - (GRID_W - 1), 1, stride=1, stride_axis=0)
            b = pltpu.roll(g1, 1, 1, stride=1, stride_axis=0)
            tbl_ref[h // 2, rr, rows, :] = jnp.where(in_window, jnp.where(left, a, b), NEG_INF)


def _mlp_kernel(x_ref, g_ref, w1_ref, w2_ref, o_ref, h_buf):
    xf = x_ref[...]
    ms = jnp.mean(xf * xf, axis=-1, keepdims=True)
    u = ((xf * lax.rsqrt(ms + EPS)) * g_ref[...]).astype(BF16)
    for c in range(D_FF // FF_CHUNK):
        cols = slice(c * FF_CHUNK, (c + 1) * FF_CHUNK)
        h = jnp.dot(u, w1_ref[:, cols], preferred_element_type=F32)
        h_buf[:, cols] = jnp.square(jnp.maximum(h, 0.0)).astype(BF16)
    o_ref[...] = xf + jnp.dot(h_buf[...], w2_ref[...], preferred_element_type=F32)


def _mlp(x2d, g2, w1, w2):
    m = x2d.shape[0]
    tok = pl.BlockSpec((TOK_TILE, D_MODEL), lambda i: (i, 0))
    return pl.pallas_call(
        _mlp_kernel,
        grid=(m // TOK_TILE,),
        in_specs=[tok, _const_spec((1, D_MODEL)), _const_spec(w1.shape), _const_spec(w2.shape)],
        out_specs=tok,
        out_shape=jax.ShapeDtypeStruct((m, D_MODEL), F32),
        scratch_shapes=[pltpu.VMEM((TOK_TILE, D_FF), BF16)],
        compiler_params=pltpu.CompilerParams(
            dimension_semantics=("arbitrary",), vmem_limit_bytes=VMEM_LIMIT),
        name="mlp",
    )(x2d, g2, w1, w2)


def kernel(x, norm1_g, w_in, q_norm_g, k_norm_g, rpb, conv_w, w_attn_branch, w_conv_branch,
           w_o, norm2_g, w_mlp_in, w_mlp_out):
    batch, seq, _ = x.shape
    depth = w_in.shape[0]
    for l in range(depth):
        x2d = x.reshape(batch * seq, D_MODEL)
        gq = jnp.tile(q_norm_g[l], N_HEADS)[None, :]
        gk = jnp.tile(k_norm_g[l], N_HEADS)[None, :]
        qe, qo, k, v, cb, z, sga, sgb, wa, wb, wo, w1, w2 = _inproj(
            x2d, norm1_g[l][None, :], w_in, gq, gk, l,
            (w_attn_branch, w_conv_branch, w_o, w_mlp_in, w_mlp_out))
        x = _mixer(batch, seq, qe, qo, k, v, cb, z, sga, sgb, x,
                   _padded_bias_rows(rpb[l]), conv_w[l], wa, wb, wo)
        x = _mlp(x.reshape(batch * seq, D_MODEL), norm2_g[l][None, :], w1, w2
                 ).reshape(batch, seq, D_MODEL)
    return x
```

```python
import functools

import numpy as np
import jax
import jax.numpy as jnp
from jax import lax
from jax.experimental import pallas as pl
from jax.experimental.pallas import tpu as pltpu

F32 = jnp.float32
BF16 = jnp.bfloat16

D_MODEL = 1024
GRID_W = 64
N_HEADS = 8
HEAD_DIM = 64
ATTN_WIDTH = N_HEADS * HEAD_DIM
CONV_WIDTH = D_MODEL // 2
WIN_ROWS = 8
WIN_COLS = 16
D_FF = 4 * D_MODEL
EPS = 1e-6
NEG_INF = -1e30
LOG2E = 1.4426950408889634
PROJ_WIDTH = 3 * ATTN_WIDTH + 3 * CONV_WIDTH + 2 * D_MODEL

LANES = 128
HEAD_PAIRS = ATTN_WIDTH // LANES
N_REL_ROWS = 2 * WIN_ROWS - 1
N_REL_COLS = 2 * WIN_COLS - 1

TOK_TILE = 1024
ROWS_PER_BLOCK = 8
MIX_TILE = ROWS_PER_BLOCK * GRID_W
KV_WINDOW_ROWS = ROWS_PER_BLOCK + WIN_ROWS
MIX_PER_TOK_TILE = TOK_TILE // MIX_TILE
assert MIX_PER_TOK_TILE * MIX_TILE == TOK_TILE
FF_CHUNK = 512
BF16_SUBLANES = 16
EDGE_ROWS = 8
VMEM_LIMIT = 56 * 1024 * 1024
INPROJ_VMEM_LIMIT = 61 * 1024 * 1024


def _const_spec(shape):
    zeros = (0,) * len(shape)
    return pl.BlockSpec(shape, lambda *_: zeros, pipeline_mode=pl.Buffered(1))


def _sigmoid(x):
    return 0.5 * jnp.tanh(0.5 * x) + 0.5


def _inproj_kernel(n_cast, x_ref, g1_ref, w_ref, gq_ref, gk_ref, cw_ref, *refs):
    cast_in, refs = refs[:n_cast], refs[n_cast:]
    qe_ref, qo_ref, k_ref, v_ref, bb_ref, edge_ref, sga_ref, sgb_ref = refs[:8]
    cast_out = refs[8:]
    for src, dst in zip(cast_in, cast_out):
        dst[...] = src[...].astype(BF16)

    xf = x_ref[...]
    ms = jnp.mean(xf * xf, axis=-1, keepdims=True)
    u = ((xf * lax.rsqrt(ms + EPS)) * g1_ref[...]).astype(BF16)

    def proj(c0, width):
        return jnp.dot(u, w_ref[:, c0:c0 + width].astype(BF16), preferred_element_type=F32)

    first_head = lax.broadcasted_iota(jnp.int32, (TOK_TILE, LANES), 1) < HEAD_DIM

    def head_norm(t, g):
        outs = []
        for p in range(HEAD_PAIRS):
            tp = t[:, p * LANES:(p + 1) * LANES]
            t2 = tp * tp
            ss_a = jnp.sum(jnp.where(first_head, t2, 0.0), axis=-1, keepdims=True)
            ss_b = jnp.sum(jnp.where(first_head, 0.0, t2), axis=-1, keepdims=True)
            r_a = lax.rsqrt(ss_a * (1.0 / HEAD_DIM) + EPS)
            r_b = lax.rsqrt(ss_b * (1.0 / HEAD_DIM) + EPS)
            outs.append(tp * jnp.where(first_head, r_a, r_b))
        return jnp.concatenate(outs, axis=-1) * g

    lane = lax.broadcasted_iota(jnp.int32, (TOK_TILE, ATTN_WIDTH), 1)
    even_head = (lane % LANES) < HEAD_DIM

    c0 = 3 * ATTN_WIDTH
    g0 = c0 + 3 * CONV_WIDTH
    sga_ref[...] = _sigmoid(proj(g0, D_MODEL)).astype(BF16)
    sgb_ref[...] = _sigmoid(proj(g0 + D_MODEL, D_MODEL)).astype(BF16)
    q = head_norm(proj(0, ATTN_WIDTH), gq_ref[...]) * (LOG2E * HEAD_DIM ** -0.5)
    qe_ref[...] = jnp.where(even_head, q, 0.0).astype(BF16)
    qo_ref[...] = jnp.where(even_head, 0.0, q).astype(BF16)
    k_ref[...] = head_norm(proj(ATTN_WIDTH, ATTN_WIDTH), gk_ref[...]).astype(BF16)
    z = proj(c0 + CONV_WIDTH, CONV_WIDTH) * proj(c0 + 2 * CONV_WIDTH, CONV_WIDTH)
    cb = proj(c0, CONV_WIDTH)
    w_prev, w_mid, w_next = cw_ref[0:1, :], cw_ref[1:2, :], cw_ref[2:3, :]
    row = lax.broadcasted_iota(jnp.int32, (TOK_TILE, CONV_WIDTH), 0)
    z_m1 = jnp.where(row == 0, 0.0, pltpu.roll(z, 1, 0))
    z_p1 = jnp.where(row == TOK_TILE - 1, 0.0, pltpu.roll(z, TOK_TILE - 1, 0))
    bb_ref[...] = (cb * (z_m1 * w_prev + z * w_mid + z_p1 * w_next)).astype(BF16)
    edge_ref[...] = jnp.concatenate(
        [z[0:1], z[TOK_TILE - 1:TOK_TILE], cb[0:1] * w_prev, cb[TOK_TILE - 1:TOK_TILE] * w_next,
         jnp.zeros((EDGE_ROWS - 4, CONV_WIDTH), F32)], axis=0)
    v_ref[...] = proj(2 * ATTN_WIDTH, ATTN_WIDTH).astype(BF16)


def _inproj(x2d, g1, w_in, gq, gk, cw, layer, later_weights):
    m = x2d.shape[0]
    n_steps = m // TOK_TILE
    tok = lambda w: pl.BlockSpec((TOK_TILE, w), lambda i: (i, 0))
    act_specs = ([tok(ATTN_WIDTH)] * 4 + [tok(CONV_WIDTH)]
                 + [pl.BlockSpec((EDGE_ROWS, CONV_WIDTH), lambda i: (i, 0))] + [tok(D_MODEL)] * 2)
    act_shapes = ([jax.ShapeDtypeStruct((m, ATTN_WIDTH), BF16)] * 4
                  + [jax.ShapeDtypeStruct((m, CONV_WIDTH), BF16),
                     jax.ShapeDtypeStruct((n_steps * EDGE_ROWS, CONV_WIDTH), F32)]
                  + [jax.ShapeDtypeStruct((m, D_MODEL), BF16)] * 2)
    cast_in_specs, cast_out_specs = [], []
    for w in later_weights:
        rows = w.shape[1] // n_steps
        assert rows * n_steps == w.shape[1] and rows % BF16_SUBLANES == 0, w.shape
        cast_in_specs.append(pl.BlockSpec((None, rows, w.shape[2]), lambda i: (layer, i, 0)))
        cast_out_specs.append(pl.BlockSpec((rows, w.shape[2]), lambda i: (i, 0)))
    return pl.pallas_call(
        functools.partial(_inproj_kernel, len(later_weights)),
        grid=(n_steps,),
        in_specs=[tok(D_MODEL), _const_spec((1, D_MODEL)),
                  pl.BlockSpec((None, D_MODEL, PROJ_WIDTH), lambda i: (layer, 0, 0),
                               pipeline_mode=pl.Buffered(1)),
                  _const_spec((1, ATTN_WIDTH)), _const_spec((1, ATTN_WIDTH)),
                  _const_spec(cw.shape)] + cast_in_specs,
        out_specs=act_specs + cast_out_specs,
        out_shape=(act_shapes
                   + [jax.ShapeDtypeStruct(w.shape[1:], BF16) for w in later_weights]),
        compiler_params=pltpu.CompilerParams(
            dimension_semantics=("arbitrary",), vmem_limit_bytes=INPROJ_VMEM_LIMIT),
        name="inproj",
    )(x2d, g1, w_in, gq, gk, cw, *later_weights)


def _mixer_kernel(rows_per_batch,
                  qe_ref, qo_ref, kw_ref, vw_ref,
                  bb_ref, ep_ref, ec_ref, en_ref, sga_ref, sgb_ref, x_ref,
                  g_ref, wa_ref, wb_ref, wo_ref,
                  o_ref, attn_buf, bb_buf, tp_ref):
    blk = pl.program_id(1)
    n_blk = pl.num_programs(1)

    @pl.when(jnp.logical_and(pl.program_id(0) == 0, blk == 0))
    def _():
        _build_bias_table(g_ref, tp_ref)

    lane = lax.broadcasted_iota(jnp.int32, (GRID_W, LANES), 1)
    first_head = lane < HEAD_DIM
    n_keys = WIN_ROWS * GRID_W
    win_row0 = jnp.clip(blk * ROWS_PER_BLOCK - WIN_ROWS // 2, 0, rows_per_batch - KV_WINDOW_ROWS)

    def window(j):
        r = blk * ROWS_PER_BLOCK + j
        row_start = jnp.clip(r - WIN_ROWS // 2, 0, rows_per_batch - WIN_ROWS)
        ks = pl.multiple_of((row_start - win_row0) * GRID_W, GRID_W)
        return ks, r - row_start

    def scores(j, p):
        ks, d = window(j)
        cols = slice(p * LANES, (p + 1) * LANES)
        rows = slice(j * GRID_W, (j + 1) * GRID_W)
        q2 = jnp.concatenate([qe_ref[rows, cols], qo_ref[rows, cols]], axis=0)
        s = lax.dot_general(q2, kw_ref[0, pl.ds(ks, n_keys), cols], (((1,), (1,)), ((), ())),
                            preferred_element_type=F32)
        bias = jnp.concatenate(
            [tp_ref[p, WIN_ROWS - 1 - d + i] for i in range(0, WIN_ROWS, 2)], axis=-1)
        return s + bias

    def attend(j, p, s):
        ks, _ = window(j)
        cols = slice(p * LANES, (p + 1) * LANES)
        rows = slice(j * GRID_W, (j + 1) * GRID_W)
        m = jnp.max(s, axis=-1, keepdims=True)
        e = jnp.exp2(s - m)
        l = jnp.sum(e, axis=-1, keepdims=True)
        o = jnp.dot(e.astype(BF16), vw_ref[0, pl.ds(ks, n_keys), cols],
                    preferred_element_type=F32)
        o = o / l
        attn_buf[rows, cols] = jnp.where(first_head, o[:GRID_W], o[GRID_W:]).astype(BF16)

    blk_in_tile = blk % MIX_PER_TOK_TILE
    at_tile_start = jnp.logical_and(blk_in_tile == 0, blk > 0)
    at_tile_end = jnp.logical_and(blk_in_tile == MIX_PER_TOK_TILE - 1, blk < n_blk - 1)
    fix_first = jnp.where(at_tile_start, ec_ref[2:3, :] * ep_ref[1:2, :], 0.0)
    fix_last = jnp.where(at_tile_end, ec_ref[3:4, :] * en_ref[0:1, :], 0.0)
    sub_row = lax.broadcasted_iota(jnp.int32, (BF16_SUBLANES, CONV_WIDTH), 0)
    top = slice(0, BF16_SUBLANES)
    bottom = slice(MIX_TILE - BF16_SUBLANES, MIX_TILE)
    bb_buf[...] = bb_ref[...]
    bb_buf[top, :] = (bb_ref[top, :].astype(F32)
                      + jnp.where(sub_row == 0, fix_first, 0.0)).astype(BF16)
    bb_buf[bottom, :] = (bb_ref[bottom, :].astype(F32)
                         + jnp.where(sub_row == BF16_SUBLANES - 1, fix_last, 0.0)).astype(BF16)

    pending = [scores(0, p) for p in range(HEAD_PAIRS)]
    for j in range(ROWS_PER_BLOCK):
        for p in range(HEAD_PAIRS):
            s = pending[p]
            if j + 1 < ROWS_PER_BLOCK:
                pending[p] = scores(j + 1, p)
            attend(j, p, s)

    ya = jnp.dot(attn_buf[...], wa_ref[...], preferred_element_type=F32)
    yb = jnp.dot(bb_buf[...], wb_ref[...], preferred_element_type=F32)
    merged = sga_ref[...] * ya.astype(BF16) + sgb_ref[...] * yb.astype(BF16)
    o_ref[...] = x_ref[...] + jnp.dot(merged, wo_ref[...], preferred_element_type=F32)


def _kv_window_start(blk, rows_per_batch):
    row0 = jnp.clip(blk * ROWS_PER_BLOCK - WIN_ROWS // 2, 0, rows_per_batch - KV_WINDOW_ROWS)
    return row0 * GRID_W


def _mixer(batch, seq, qe, qo, k, v, bb, edges, sga, sgb, x, bias_rows, wa, wb, wo):
    rows = seq // GRID_W
    n_blk = rows // ROWS_PER_BLOCK
    tiles_per_seq = seq // TOK_TILE
    n_tiles = batch * tiles_per_seq

    def r3(a):
        return a.reshape(batch, seq, a.shape[-1])

    cur = lambda w: pl.BlockSpec((None, MIX_TILE, w), lambda b, i: (b, i, 0))
    kv_window = pl.BlockSpec(
        (pl.Element(1), pl.Element(KV_WINDOW_ROWS * GRID_W), pl.Element(ATTN_WIDTH)),
        lambda b, i: (b, _kv_window_start(i, rows), 0))
    def edge_spec(offset):
        def index(b, i):
            tile = b * tiles_per_seq + i // MIX_PER_TOK_TILE + offset
            return (jnp.clip(tile, 0, n_tiles - 1), 0)
        return pl.BlockSpec((EDGE_ROWS, CONV_WIDTH), index)

    aw = ATTN_WIDTH
    in_specs = [cur(aw), cur(aw), kv_window, kv_window,
                cur(CONV_WIDTH), edge_spec(-1), edge_spec(0), edge_spec(1),
                cur(D_MODEL), cur(D_MODEL), cur(D_MODEL),
                _const_spec(bias_rows.shape), _const_spec(wa.shape),
                _const_spec(wb.shape), _const_spec(wo.shape)]
    out = pl.pallas_call(
        functools.partial(_mixer_kernel, rows),
        grid=(batch, n_blk),
        in_specs=in_specs,
        out_specs=cur(D_MODEL),
        out_shape=jax.ShapeDtypeStruct((batch, seq, D_MODEL), F32),
        scratch_shapes=[pltpu.VMEM((MIX_TILE, ATTN_WIDTH), BF16),
                        pltpu.VMEM((MIX_TILE, CONV_WIDTH), BF16),
                        pltpu.VMEM((HEAD_PAIRS, N_REL_ROWS - 1, 2 * GRID_W, LANES), F32)],
        compiler_params=pltpu.CompilerParams(
            dimension_semantics=("arbitrary", "arbitrary"), vmem_limit_bytes=VMEM_LIMIT),
        name="mixer",
    )(r3(qe), r3(qo), r3(k), r3(v), r3(bb), edges, edges, edges, r3(sga), r3(sgb), x,
      bias_rows, wa, wb, wo)
    return out


def _padded_bias_rows(rpb):
    pad = GRID_W - WIN_COLS
    return jnp.pad(rpb.astype(F32), ((0, 0), (0, 0), (pad, LANES - N_REL_COLS - pad)))


def _build_bias_table(g_ref, tbl_ref):
    c = lax.broadcasted_iota(jnp.int32, (GRID_W, LANES), 0)
    x = lax.broadcasted_iota(jnp.int32, (GRID_W, LANES), 1)
    kc = x % GRID_W
    win_start = jnp.clip(c - WIN_COLS // 2, 0, GRID_W - WIN_COLS)
    in_window = (kc >= win_start) & (kc < win_start + WIN_COLS)
    left = x < GRID_W
    for h in range(N_HEADS):
        rows = slice((h % 2) * GRID_W, (h % 2 + 1) * GRID_W)
        for rr in range(N_REL_ROWS - 1):
            g0 = jnp.broadcast_to(g_ref[h, rr:rr + 1, :] * LOG2E, (GRID_W, LANES))
            g1 = jnp.broadcast_to(g_ref[h, rr + 1:rr + 2, :] * LOG2E, (GRID_W, LANES))
            a = pltpu.roll(g0, LANES - (GRID_W - 1), 1, stride=1, stride_axis=0)
            b = pltpu.roll(g1, 1, 1, stride=1, stride_axis=0)
            tbl_ref[h // 2, rr, rows, :] = jnp.where(in_window, jnp.where(left, a, b), NEG_INF)


def _mlp_kernel(x_ref, g_ref, w1_ref, w2_ref, o_ref, h_buf):
    xf = x_ref[...]
    ms = jnp.mean(xf * xf, axis=-1, keepdims=True)
    u = ((xf * lax.rsqrt(ms + EPS)) * g_ref[...]).astype(BF16)
    for c in range(D_FF // FF_CHUNK):
        cols = slice(c * FF_CHUNK, (c + 1) * FF_CHUNK)
        h = jnp.dot(u, w1_ref[:, cols], preferred_element_type=F32)
        h_buf[:, cols] = jnp.square(jnp.maximum(h, 0.0)).astype(BF16)
    o_ref[...] = xf + jnp.dot(h_buf[...], w2_ref[...], preferred_element_type=F32)


def _mlp(x2d, g2, w1, w2):
    m = x2d.shape[0]
    tok = pl.BlockSpec((TOK_TILE, D_MODEL), lambda i: (i, 0))
    return pl.pallas_call(
        _mlp_kernel,
        grid=(m // TOK_TILE,),
        in_specs=[tok, _const_spec((1, D_MODEL)), _const_spec(w1.shape), _const_spec(w2.shape)],
        out_specs=tok,
        out_shape=jax.ShapeDtypeStruct((m, D_MODEL), F32),
        scratch_shapes=[pltpu.VMEM((TOK_TILE, D_FF), BF16)],
        compiler_params=pltpu.CompilerParams(
            dimension_semantics=("arbitrary",), vmem_limit_bytes=VMEM_LIMIT),
        name="mlp",
    )(x2d, g2, w1, w2)


def kernel(x, norm1_g, w_in, q_norm_g, k_norm_g, rpb, conv_w, w_attn_branch, w_conv_branch,
           w_o, norm2_g, w_mlp_in, w_mlp_out):
    batch, seq, _ = x.shape
    depth = w_in.shape[0]
    for l in range(depth):
        x2d = x.reshape(batch * seq, D_MODEL)
        gq = jnp.tile(q_norm_g[l], N_HEADS)[None, :]
        gk = jnp.tile(k_norm_g[l], N_HEADS)[None, :]
        assert seq % TOK_TILE == 0
        qe, qo, k, v, bb, edges, sga, sgb, wa, wb, wo, w1, w2 = _inproj(
            x2d, norm1_g[l][None, :], w_in, gq, gk, conv_w[l], l,
            (w_attn_branch, w_conv_branch, w_o, w_mlp_in, w_mlp_out))
        x = _mixer(batch, seq, qe, qo, k, v, bb, edges, sga, sgb, x,
                   _padded_bias_rows(rpb[l]), wa, wb, wo)
        x = _mlp(x.reshape(batch * seq, D_MODEL), norm2_g[l][None, :], w1, w2
                 ).reshape(batch, seq, D_MODEL)
    return x
```

```python
import functools

import numpy as np
import jax
import jax.numpy as jnp
from jax import lax
from jax.experimental import pallas as pl
from jax.experimental.pallas import tpu as pltpu

F32 = jnp.float32
BF16 = jnp.bfloat16

D_MODEL = 1024
GRID_W = 64
N_HEADS = 8
HEAD_DIM = 64
ATTN_WIDTH = N_HEADS * HEAD_DIM
CONV_WIDTH = D_MODEL // 2
WIN_ROWS = 8
WIN_COLS = 16
D_FF = 4 * D_MODEL
EPS = 1e-6
NEG_INF = -1e30
LOG2E = 1.4426950408889634
PROJ_WIDTH = 3 * ATTN_WIDTH + 3 * CONV_WIDTH + 2 * D_MODEL

LANES = 128
HEAD_PAIRS = ATTN_WIDTH // LANES
N_REL_ROWS = 2 * WIN_ROWS - 1
N_REL_COLS = 2 * WIN_COLS - 1

TOK_TILE = 1024
ROWS_PER_BLOCK = 16
MIX_TILE = ROWS_PER_BLOCK * GRID_W
KV_WINDOW_ROWS = ROWS_PER_BLOCK + WIN_ROWS
MIX_PER_TOK_TILE = TOK_TILE // MIX_TILE
assert MIX_PER_TOK_TILE * MIX_TILE == TOK_TILE
FF_CHUNK = 512
BF16_SUBLANES = 16
EDGE_ROWS = 8
VMEM_LIMIT = 56 * 1024 * 1024
INPROJ_VMEM_LIMIT = 61 * 1024 * 1024


def _const_spec(shape):
    zeros = (0,) * len(shape)
    return pl.BlockSpec(shape, lambda *_: zeros, pipeline_mode=pl.Buffered(1))


def _sigmoid(x):
    return 0.5 * jnp.tanh(0.5 * x) + 0.5


def _inproj_kernel(n_cast, x_ref, g1_ref, w_ref, gq_ref, gk_ref, cw_ref, *refs):
    cast_in, refs = refs[:n_cast], refs[n_cast:]
    qe_ref, qo_ref, k_ref, v_ref, bb_ref, edge_ref, sga_ref, sgb_ref = refs[:8]
    cast_out = refs[8:]
    for src, dst in zip(cast_in, cast_out):
        dst[...] = src[...].astype(BF16)

    xf = x_ref[...]
    ms = jnp.mean(xf * xf, axis=-1, keepdims=True)
    u = ((xf * lax.rsqrt(ms + EPS)) * g1_ref[...]).astype(BF16)

    def proj(c0, width):
        return jnp.dot(u, w_ref[:, c0:c0 + width].astype(BF16), preferred_element_type=F32)

    first_head = lax.broadcasted_iota(jnp.int32, (TOK_TILE, LANES), 1) < HEAD_DIM

    def head_norm(t, g):
        outs = []
        for p in range(HEAD_PAIRS):
            tp = t[:, p * LANES:(p + 1) * LANES]
            t2 = tp * tp
            ss_a = jnp.sum(jnp.where(first_head, t2, 0.0), axis=-1, keepdims=True)
            ss_b = jnp.sum(jnp.where(first_head, 0.0, t2), axis=-1, keepdims=True)
            r_a = lax.rsqrt(ss_a * (1.0 / HEAD_DIM) + EPS)
            r_b = lax.rsqrt(ss_b * (1.0 / HEAD_DIM) + EPS)
            outs.append(tp * jnp.where(first_head, r_a, r_b))
        return jnp.concatenate(outs, axis=-1) * g

    lane = lax.broadcasted_iota(jnp.int32, (TOK_TILE, ATTN_WIDTH), 1)
    even_head = (lane % LANES) < HEAD_DIM

    c0 = 3 * ATTN_WIDTH
    g0 = c0 + 3 * CONV_WIDTH
    sga_ref[...] = _sigmoid(proj(g0, D_MODEL)).astype(BF16)
    sgb_ref[...] = _sigmoid(proj(g0 + D_MODEL, D_MODEL)).astype(BF16)
    q = head_norm(proj(0, ATTN_WIDTH), gq_ref[...]) * (LOG2E * HEAD_DIM ** -0.5)
    qe_ref[...] = jnp.where(even_head, q, 0.0).astype(BF16)
    qo_ref[...] = jnp.where(even_head, 0.0, q).astype(BF16)
    k_ref[...] = head_norm(proj(ATTN_WIDTH, ATTN_WIDTH), gk_ref[...]).astype(BF16)
    z = proj(c0 + CONV_WIDTH, CONV_WIDTH) * proj(c0 + 2 * CONV_WIDTH, CONV_WIDTH)
    cb = proj(c0, CONV_WIDTH)
    w_prev, w_mid, w_next = cw_ref[0:1, :], cw_ref[1:2, :], cw_ref[2:3, :]
    row = lax.broadcasted_iota(jnp.int32, (TOK_TILE, CONV_WIDTH), 0)
    z_m1 = jnp.where(row == 0, 0.0, pltpu.roll(z, 1, 0))
    z_p1 = jnp.where(row == TOK_TILE - 1, 0.0, pltpu.roll(z, TOK_TILE - 1, 0))
    bb_ref[...] = (cb * (z_m1 * w_prev + z * w_mid + z_p1 * w_next)).astype(BF16)
    edge_ref[...] = jnp.concatenate(
        [z[0:1], z[TOK_TILE - 1:TOK_TILE], cb[0:1] * w_prev, cb[TOK_TILE - 1:TOK_TILE] * w_next,
         jnp.zeros((EDGE_ROWS - 4, CONV_WIDTH), F32)], axis=0)
    v_ref[...] = proj(2 * ATTN_WIDTH, ATTN_WIDTH).astype(BF16)


def _inproj(x2d, g1, w_in, gq, gk, cw, layer, later_weights):
    m = x2d.shape[0]
    n_steps = m // TOK_TILE
    tok = lambda w: pl.BlockSpec((TOK_TILE, w), lambda i: (i, 0))
    act_specs = ([tok(ATTN_WIDTH)] * 4 + [tok(CONV_WIDTH)]
                 + [pl.BlockSpec((EDGE_ROWS, CONV_WIDTH), lambda i: (i, 0))] + [tok(D_MODEL)] * 2)
    act_shapes = ([jax.ShapeDtypeStruct((m, ATTN_WIDTH), BF16)] * 4
                  + [jax.ShapeDtypeStruct((m, CONV_WIDTH), BF16),
                     jax.ShapeDtypeStruct((n_steps * EDGE_ROWS, CONV_WIDTH), F32)]
                  + [jax.ShapeDtypeStruct((m, D_MODEL), BF16)] * 2)
    cast_in_specs, cast_out_specs = [], []
    for w in later_weights:
        rows = w.shape[1] // n_steps
        assert rows * n_steps == w.shape[1] and rows % BF16_SUBLANES == 0, w.shape
        cast_in_specs.append(pl.BlockSpec((None, rows, w.shape[2]), lambda i: (layer, i, 0)))
        cast_out_specs.append(pl.BlockSpec((rows, w.shape[2]), lambda i: (i, 0)))
    return pl.pallas_call(
        functools.partial(_inproj_kernel, len(later_weights)),
        grid=(n_steps,),
        in_specs=[tok(D_MODEL), _const_spec((1, D_MODEL)),
                  pl.BlockSpec((None, D_MODEL, PROJ_WIDTH), lambda i: (layer, 0, 0),
                               pipeline_mode=pl.Buffered(1)),
                  _const_spec((1, ATTN_WIDTH)), _const_spec((1, ATTN_WIDTH)),
                  _const_spec(cw.shape)] + cast_in_specs,
        out_specs=act_specs + cast_out_specs,
        out_shape=(act_shapes
                   + [jax.ShapeDtypeStruct(w.shape[1:], BF16) for w in later_weights]),
        compiler_params=pltpu.CompilerParams(
            dimension_semantics=("arbitrary",), vmem_limit_bytes=INPROJ_VMEM_LIMIT),
        name="inproj",
    )(x2d, g1, w_in, gq, gk, cw, *later_weights)


def _mixer_kernel(rows_per_batch,
                  qe_ref, qo_ref, kw_ref, vw_ref,
                  bb_ref, ep_ref, ec_ref, en_ref, sga_ref, sgb_ref, x_ref,
                  g_ref, wa_ref, wb_ref, wo_ref,
                  o_ref, attn_buf, bb_buf, tp_ref):
    blk = pl.program_id(1)
    n_blk = pl.num_programs(1)

    @pl.when(jnp.logical_and(pl.program_id(0) == 0, blk == 0))
    def _():
        _build_bias_table(g_ref, tp_ref)

    lane = lax.broadcasted_iota(jnp.int32, (GRID_W, LANES), 1)
    first_head = lane < HEAD_DIM
    n_keys = WIN_ROWS * GRID_W
    win_row0 = jnp.clip(blk * ROWS_PER_BLOCK - WIN_ROWS // 2, 0, rows_per_batch - KV_WINDOW_ROWS)

    def window(j):
        r = blk * ROWS_PER_BLOCK + j
        row_start = jnp.clip(r - WIN_ROWS // 2, 0, rows_per_batch - WIN_ROWS)
        ks = pl.multiple_of((row_start - win_row0) * GRID_W, GRID_W)
        return ks, r - row_start

    def scores(j, p):
        ks, d = window(j)
        cols = slice(p * LANES, (p + 1) * LANES)
        rows = slice(j * GRID_W, (j + 1) * GRID_W)
        q2 = jnp.concatenate([qe_ref[rows, cols], qo_ref[rows, cols]], axis=0)
        s = lax.dot_general(q2, kw_ref[0, pl.ds(ks, n_keys), cols], (((1,), (1,)), ((), ())),
                            preferred_element_type=F32)
        bias = jnp.concatenate(
            [tp_ref[p, WIN_ROWS - 1 - d + i] for i in range(0, WIN_ROWS, 2)], axis=-1)
        return s + bias

    def attend(j, p, s):
        ks, _ = window(j)
        cols = slice(p * LANES, (p + 1) * LANES)
        rows = slice(j * GRID_W, (j + 1) * GRID_W)
        m = jnp.max(s, axis=-1, keepdims=True)
        e = jnp.exp2(s - m)
        l = jnp.sum(e, axis=-1, keepdims=True)
        o = jnp.dot(e.astype(BF16), vw_ref[0, pl.ds(ks, n_keys), cols],
                    preferred_element_type=F32)
        o = o / l
        attn_buf[rows, cols] = jnp.where(first_head, o[:GRID_W], o[GRID_W:]).astype(BF16)

    blk_in_tile = blk % MIX_PER_TOK_TILE
    at_tile_start = jnp.logical_and(blk_in_tile == 0, blk > 0)
    at_tile_end = jnp.logical_and(blk_in_tile == MIX_PER_TOK_TILE - 1, blk < n_blk - 1)
    fix_first = jnp.where(at_tile_start, ec_ref[2:3, :] * ep_ref[1:2, :], 0.0)
    fix_last = jnp.where(at_tile_end, ec_ref[3:4, :] * en_ref[0:1, :], 0.0)
    sub_row = lax.broadcasted_iota(jnp.int32, (BF16_SUBLANES, CONV_WIDTH), 0)
    top = slice(0, BF16_SUBLANES)
    bottom = slice(MIX_TILE - BF16_SUBLANES, MIX_TILE)
    bb_buf[...] = bb_ref[...]
    bb_buf[top, :] = (bb_ref[top, :].astype(F32)
                      + jnp.where(sub_row == 0, fix_first, 0.0)).astype(BF16)
    bb_buf[bottom, :] = (bb_ref[bottom, :].astype(F32)
                         + jnp.where(sub_row == BF16_SUBLANES - 1, fix_last, 0.0)).astype(BF16)

    pending = [scores(0, p) for p in range(HEAD_PAIRS)]
    for j in range(ROWS_PER_BLOCK):
        for p in range(HEAD_PAIRS):
            s = pending[p]
            if j + 1 < ROWS_PER_BLOCK:
                pending[p] = scores(j + 1, p)
            attend(j, p, s)

    ya = jnp.dot(attn_buf[...], wa_ref[...], preferred_element_type=F32)
    yb = jnp.dot(bb_buf[...], wb_ref[...], preferred_element_type=F32)
    merged = sga_ref[...] * ya.astype(BF16) + sgb_ref[...] * yb.astype(BF16)
    o_ref[...] = x_ref[...] + jnp.dot(merged, wo_ref[...], preferred_element_type=F32)


def _kv_window_start(blk, rows_per_batch):
    row0 = jnp.clip(blk * ROWS_PER_BLOCK - WIN_ROWS // 2, 0, rows_per_batch - KV_WINDOW_ROWS)
    return row0 * GRID_W


def _mixer(batch, seq, qe, qo, k, v, bb, edges, sga, sgb, x, bias_rows, wa, wb, wo):
    rows = seq // GRID_W
    n_blk = rows // ROWS_PER_BLOCK
    tiles_per_seq = seq // TOK_TILE
    n_tiles = batch * tiles_per_seq

    def r3(a):
        return a.reshape(batch, seq, a.shape[-1])

    cur = lambda w: pl.BlockSpec((None, MIX_TILE, w), lambda b, i: (b, i, 0))
    kv_window = pl.BlockSpec(
        (pl.Element(1), pl.Element(KV_WINDOW_ROWS * GRID_W), pl.Element(ATTN_WIDTH)),
        lambda b, i: (b, _kv_window_start(i, rows), 0))
    def edge_spec(offset):
        def index(b, i):
            tile = b * tiles_per_seq + i // MIX_PER_TOK_TILE + offset
            return (jnp.clip(tile, 0, n_tiles - 1), 0)
        return pl.BlockSpec((EDGE_ROWS, CONV_WIDTH), index)

    aw = ATTN_WIDTH
    in_specs = [cur(aw), cur(aw), kv_window, kv_window,
                cur(CONV_WIDTH), edge_spec(-1), edge_spec(0), edge_spec(1),
                cur(D_MODEL), cur(D_MODEL), cur(D_MODEL),
                _const_spec(bias_rows.shape), _const_spec(wa.shape),
                _const_spec(wb.shape), _const_spec(wo.shape)]
    out = pl.pallas_call(
        functools.partial(_mixer_kernel, rows),
        grid=(batch, n_blk),
        in_specs=in_specs,
        out_specs=cur(D_MODEL),
        out_shape=jax.ShapeDtypeStruct((batch, seq, D_MODEL), F32),
        scratch_shapes=[pltpu.VMEM((MIX_TILE, ATTN_WIDTH), BF16),
                        pltpu.VMEM((MIX_TILE, CONV_WIDTH), BF16),
                        pltpu.VMEM((HEAD_PAIRS, N_REL_ROWS - 1, 2 * GRID_W, LANES), F32)],
        compiler_params=pltpu.CompilerParams(
            dimension_semantics=("arbitrary", "arbitrary"), vmem_limit_bytes=VMEM_LIMIT),
        name="mixer",
    )(r3(qe), r3(qo), r3(k), r3(v), r3(bb), edges, edges, edges, r3(sga), r3(sgb), x,
      bias_rows, wa, wb, wo)
    return out


def _padded_bias_rows(rpb):
    pad = GRID_W - WIN_COLS
    return jnp.pad(rpb.astype(F32), ((0, 0), (0, 0), (pad, LANES - N_REL_COLS - pad)))


def _build_bias_table(g_ref, tbl_ref):
    c = lax.broadcasted_iota(jnp.int32, (GRID_W, LANES), 0)
    x = lax.broadcasted_iota(jnp.int32, (GRID_W, LANES), 1)
    kc = x % GRID_W
    win_start = jnp.clip(c - WIN_COLS // 2, 0, GRID_W - WIN_COLS)
    in_window = (kc >= win_start) & (kc < win_start + WIN_COLS)
    left = x < GRID_W
    for h in range(N_HEADS):
        rows = slice((h % 2) * GRID_W, (h % 2 + 1) * GRID_W)
        for rr in range(N_REL_ROWS - 1):
            g0 = jnp.broadcast_to(g_ref[h, rr:rr + 1, :] * LOG2E, (GRID_W, LANES))
            g1 = jnp.broadcast_to(g_ref[h, rr + 1:rr + 2, :] * LOG2E, (GRID_W, LANES))
            a = pltpu.roll(g0, LANES - (GRID_W - 1), 1, stride=1, stride_axis=0)
            b = pltpu.roll(g1, 1, 1, stride=1, stride_axis=0)
            tbl_ref[h // 2, rr, rows, :] = jnp.where(in_window, jnp.where(left, a, b), NEG_INF)


def _mlp_kernel(x_ref, g_ref, w1_ref, w2_ref, o_ref, h_buf):
    xf = x_ref[...]
    ms = jnp.mean(xf * xf, axis=-1, keepdims=True)
    u = ((xf * lax.rsqrt(ms + EPS)) * g_ref[...]).astype(BF16)
    for c in range(D_FF // FF_CHUNK):
        cols = slice(c * FF_CHUNK, (c + 1) * FF_CHUNK)
        h = jnp.dot(u, w1_ref[:, cols], preferred_element_type=F32)
        h_buf[:, cols] = jnp.square(jnp.maximum(h, 0.0)).astype(BF16)
    o_ref[...] = xf + jnp.dot(h_buf[...], w2_ref[...], preferred_element_type=F32)


def _mlp(x2d, g2, w1, w2):
    m = x2d.shape[0]
    tok = pl.BlockSpec((TOK_TILE, D_MODEL), lambda i: (i, 0))
    return pl.pallas_call(
        _mlp_kernel,
        grid=(m // TOK_TILE,),
        in_specs=[tok, _const_spec((1, D_MODEL)), _const_spec(w1.shape), _const_spec(w2.shape)],
        out_specs=tok,
        out_shape=jax.ShapeDtypeStruct((m, D_MODEL), F32),
        scratch_shapes=[pltpu.VMEM((TOK_TILE, D_FF), BF16)],
        compiler_params=pltpu.CompilerParams(
            dimension_semantics=("arbitrary",), vmem_limit_bytes=VMEM_LIMIT),
        name="mlp",
    )(x2d, g2, w1, w2)


def kernel(x, norm1_g, w_in, q_norm_g, k_norm_g, rpb, conv_w, w_attn_branch, w_conv_branch,
           w_o, norm2_g, w_mlp_in, w_mlp_out):
    batch, seq, _ = x.shape
    depth = w_in.shape[0]
    for l in range(depth):
        x2d = x.reshape(batch * seq, D_MODEL)
        gq = jnp.tile(q_norm_g[l], N_HEADS)[None, :]
        gk = jnp.tile(k_norm_g[l], N_HEADS)[None, :]
        assert seq % TOK_TILE == 0
        qe, qo, k, v, bb, edges, sga, sgb, wa, wb, wo, w1, w2 = _inproj(
            x2d, norm1_g[l][None, :], w_in, gq, gk, conv_w[l], l,
            (w_attn_branch, w_conv_branch, w_o, w_mlp_in, w_mlp_out))
        x = _mixer(batch, seq, qe, qo, k, v, bb, edges, sga, sgb, x,
                   _padded_bias_rows(rpb[l]), wa, wb, wo)
        x = _mlp(x.reshape(batch * seq, D_MODEL), norm2_g[l][None, :], w1, w2
                 ).reshape(batch, seq, D_MODEL)
    return x
```

```python
import functools

import numpy as np
import jax
import jax.numpy as jnp
from jax import lax
from jax.experimental import pallas as pl
from jax.experimental.pallas import tpu as pltpu

F32 = jnp.float32
BF16 = jnp.bfloat16

D_MODEL = 1024
GRID_W = 64
N_HEADS = 8
HEAD_DIM = 64
ATTN_WIDTH = N_HEADS * HEAD_DIM
CONV_WIDTH = D_MODEL // 2
WIN_ROWS = 8
WIN_COLS = 16
D_FF = 4 * D_MODEL
EPS = 1e-6
NEG_INF = -1e30
LOG2E = 1.4426950408889634
PROJ_WIDTH = 3 * ATTN_WIDTH + 3 * CONV_WIDTH + 2 * D_MODEL

LANES = 128
HEAD_PAIRS = ATTN_WIDTH // LANES
N_REL_ROWS = 2 * WIN_ROWS - 1
N_REL_COLS = 2 * WIN_COLS - 1

TOK_TILE = 1024
NORM_ROWS = 256
ROWS_PER_BLOCK = 16
MIX_TILE = ROWS_PER_BLOCK * GRID_W
KV_WINDOW_ROWS = ROWS_PER_BLOCK + WIN_ROWS
MIX_PER_TOK_TILE = TOK_TILE // MIX_TILE
assert MIX_PER_TOK_TILE * MIX_TILE == TOK_TILE
FF_CHUNK = 512
BF16_SUBLANES = 16
EDGE_ROWS = 8
VMEM_LIMIT = 56 * 1024 * 1024
INPROJ_VMEM_LIMIT = 61 * 1024 * 1024


def _const_spec(shape):
    zeros = (0,) * len(shape)
    return pl.BlockSpec(shape, lambda *_: zeros, pipeline_mode=pl.Buffered(1))


def _sigmoid(x):
    return 0.5 * jnp.tanh(0.5 * x) + 0.5


def _inproj_kernel(n_cast, x_ref, g1_ref, w_ref, gq_ref, gk_ref, cw_ref, *refs):
    cast_in, refs = refs[:n_cast], refs[n_cast:]
    qe_ref, qo_ref, k_ref, v_ref, bb_ref, edge_ref, sga_ref, sgb_ref = refs[:8]
    cast_out = refs[8:]
    for src, dst in zip(cast_in, cast_out):
        dst[...] = src[...].astype(BF16)

    def weights(c0, width):
        return w_ref[:, c0:c0 + width].astype(BF16)

    c0 = 3 * ATTN_WIDTH
    g0 = c0 + 3 * CONV_WIDTH
    w_first = weights(g0, D_MODEL)
    u_parts = []
    for r0 in range(0, TOK_TILE, NORM_ROWS):
        xf = x_ref[r0:r0 + NORM_ROWS, :]
        ms = jnp.mean(xf * xf, axis=-1, keepdims=True)
        u_part = ((xf * lax.rsqrt(ms + EPS)) * g1_ref[...]).astype(BF16)
        sga_ref[r0:r0 + NORM_ROWS, :] = _sigmoid(
            jnp.dot(u_part, w_first, preferred_element_type=F32)).astype(BF16)
        u_parts.append(u_part)
    u = jnp.concatenate(u_parts, axis=0)

    def proj(c0, width):
        return jnp.dot(u, weights(c0, width), preferred_element_type=F32)

    first_head = lax.broadcasted_iota(jnp.int32, (TOK_TILE, LANES), 1) < HEAD_DIM

    def head_norm(t, g):
        outs = []
        for p in range(HEAD_PAIRS):
            tp = t[:, p * LANES:(p + 1) * LANES]
            t2 = tp * tp
            ss_a = jnp.sum(jnp.where(first_head, t2, 0.0), axis=-1, keepdims=True)
            ss_b = jnp.sum(jnp.where(first_head, 0.0, t2), axis=-1, keepdims=True)
            r_a = lax.rsqrt(ss_a * (1.0 / HEAD_DIM) + EPS)
            r_b = lax.rsqrt(ss_b * (1.0 / HEAD_DIM) + EPS)
            outs.append(tp * jnp.where(first_head, r_a, r_b))
        return jnp.concatenate(outs, axis=-1) * g

    lane = lax.broadcasted_iota(jnp.int32, (TOK_TILE, ATTN_WIDTH), 1)
    even_head = (lane % LANES) < HEAD_DIM

    sgb_ref[...] = _sigmoid(proj(g0 + D_MODEL, D_MODEL)).astype(BF16)
    q = head_norm(proj(0, ATTN_WIDTH), gq_ref[...]) * (LOG2E * HEAD_DIM ** -0.5)
    qe_ref[...] = jnp.where(even_head, q, 0.0).astype(BF16)
    qo_ref[...] = jnp.where(even_head, 0.0, q).astype(BF16)
    k_ref[...] = head_norm(proj(ATTN_WIDTH, ATTN_WIDTH), gk_ref[...]).astype(BF16)
    z = proj(c0 + CONV_WIDTH, CONV_WIDTH) * proj(c0 + 2 * CONV_WIDTH, CONV_WIDTH)
    cb = proj(c0, CONV_WIDTH)
    w_prev, w_mid, w_next = cw_ref[0:1, :], cw_ref[1:2, :], cw_ref[2:3, :]
    row = lax.broadcasted_iota(jnp.int32, (TOK_TILE, CONV_WIDTH), 0)
    z_m1 = jnp.where(row == 0, 0.0, pltpu.roll(z, 1, 0))
    z_p1 = jnp.where(row == TOK_TILE - 1, 0.0, pltpu.roll(z, TOK_TILE - 1, 0))
    bb_ref[...] = (cb * (z_m1 * w_prev + z * w_mid + z_p1 * w_next)).astype(BF16)
    edge_ref[...] = jnp.concatenate(
        [z[0:1], z[TOK_TILE - 1:TOK_TILE], cb[0:1] * w_prev, cb[TOK_TILE - 1:TOK_TILE] * w_next,
         jnp.zeros((EDGE_ROWS - 4, CONV_WIDTH), F32)], axis=0)
    v_ref[...] = proj(2 * ATTN_WIDTH, ATTN_WIDTH).astype(BF16)


def _inproj(x2d, g1, w_in, gq, gk, cw, layer, later_weights):
    m = x2d.shape[0]
    n_steps = m // TOK_TILE
    tok = lambda w: pl.BlockSpec((TOK_TILE, w), lambda i: (i, 0))
    act_specs = ([tok(ATTN_WIDTH)] * 4 + [tok(CONV_WIDTH)]
                 + [pl.BlockSpec((EDGE_ROWS, CONV_WIDTH), lambda i: (i, 0))] + [tok(D_MODEL)] * 2)
    act_shapes = ([jax.ShapeDtypeStruct((m, ATTN_WIDTH), BF16)] * 4
                  + [jax.ShapeDtypeStruct((m, CONV_WIDTH), BF16),
                     jax.ShapeDtypeStruct((n_steps * EDGE_ROWS, CONV_WIDTH), F32)]
                  + [jax.ShapeDtypeStruct((m, D_MODEL), BF16)] * 2)
    cast_in_specs, cast_out_specs = [], []
    for w in later_weights:
        rows = w.shape[1] // n_steps
        assert rows * n_steps == w.shape[1] and rows % BF16_SUBLANES == 0, w.shape
        cast_in_specs.append(pl.BlockSpec((None, rows, w.shape[2]), lambda i: (layer, i, 0)))
        cast_out_specs.append(pl.BlockSpec((rows, w.shape[2]), lambda i: (i, 0)))
    return pl.pallas_call(
        functools.partial(_inproj_kernel, len(later_weights)),
        grid=(n_steps,),
        in_specs=[tok(D_MODEL), _const_spec((1, D_MODEL)),
                  pl.BlockSpec((None, D_MODEL, PROJ_WIDTH), lambda i: (layer, 0, 0),
                               pipeline_mode=pl.Buffered(1)),
                  _const_spec((1, ATTN_WIDTH)), _const_spec((1, ATTN_WIDTH)),
                  _const_spec(cw.shape)] + cast_in_specs,
        out_specs=act_specs + cast_out_specs,
        out_shape=(act_shapes
                   + [jax.ShapeDtypeStruct(w.shape[1:], BF16) for w in later_weights]),
        compiler_params=pltpu.CompilerParams(
            dimension_semantics=("arbitrary",), vmem_limit_bytes=INPROJ_VMEM_LIMIT),
        name="inproj",
    )(x2d, g1, w_in, gq, gk, cw, *later_weights)


def _mixer_kernel(rows_per_batch,
                  qe_ref, qo_ref, kw_ref, vw_ref,
                  bb_ref, ep_ref, ec_ref, en_ref, sga_ref, sgb_ref, x_ref,
                  g_ref, wa_ref, wb_ref, wo_ref,
                  o_ref, attn_buf, bb_buf, tp_ref):
    blk = pl.program_id(1)
    n_blk = pl.num_programs(1)

    @pl.when(jnp.logical_and(pl.program_id(0) == 0, blk == 0))
    def _():
        _build_bias_table(g_ref, tp_ref)

    lane = lax.broadcasted_iota(jnp.int32, (GRID_W, LANES), 1)
    first_head = lane < HEAD_DIM
    n_keys = WIN_ROWS * GRID_W
    win_row0 = jnp.clip(blk * ROWS_PER_BLOCK - WIN_ROWS // 2, 0, rows_per_batch - KV_WINDOW_ROWS)

    def window(j):
        r = blk * ROWS_PER_BLOCK + j
        row_start = jnp.clip(r - WIN_ROWS // 2, 0, rows_per_batch - WIN_ROWS)
        ks = pl.multiple_of((row_start - win_row0) * GRID_W, GRID_W)
        return ks, r - row_start

    def scores(j, p):
        ks, d = window(j)
        cols = slice(p * LANES, (p + 1) * LANES)
        rows = slice(j * GRID_W, (j + 1) * GRID_W)
        q2 = jnp.concatenate([qe_ref[rows, cols], qo_ref[rows, cols]], axis=0)
        s = lax.dot_general(q2, kw_ref[0, pl.ds(ks, n_keys), cols], (((1,), (1,)), ((), ())),
                            preferred_element_type=F32)
        bias = jnp.concatenate(
            [tp_ref[p, WIN_ROWS - 1 - d + i] for i in range(0, WIN_ROWS, 2)], axis=-1)
        return s + bias

    def attend(j, p, s):
        ks, _ = window(j)
        cols = slice(p * LANES, (p + 1) * LANES)
        rows = slice(j * GRID_W, (j + 1) * GRID_W)
        m = jnp.max(s, axis=-1, keepdims=True)
        e = jnp.exp2(s - m)
        l = jnp.sum(e, axis=-1, keepdims=True)
        o = jnp.dot(e.astype(BF16), vw_ref[0, pl.ds(ks, n_keys), cols],
                    preferred_element_type=F32)
        o = o / l
        attn_buf[rows, cols] = jnp.where(first_head, o[:GRID_W], o[GRID_W:]).astype(BF16)

    blk_in_tile = blk % MIX_PER_TOK_TILE
    at_tile_start = jnp.logical_and(blk_in_tile == 0, blk > 0)
    at_tile_end = jnp.logical_and(blk_in_tile == MIX_PER_TOK_TILE - 1, blk < n_blk - 1)
    fix_first = jnp.where(at_tile_start, ec_ref[2:3, :] * ep_ref[1:2, :], 0.0)
    fix_last = jnp.where(at_tile_end, ec_ref[3:4, :] * en_ref[0:1, :], 0.0)
    sub_row = lax.broadcasted_iota(jnp.int32, (BF16_SUBLANES, CONV_WIDTH), 0)
    top = slice(0, BF16_SUBLANES)
    bottom = slice(MIX_TILE - BF16_SUBLANES, MIX_TILE)
    bb_buf[...] = bb_ref[...]
    bb_buf[top, :] = (bb_ref[top, :].astype(F32)
                      + jnp.where(sub_row == 0, fix_first, 0.0)).astype(BF16)
    bb_buf[bottom, :] = (bb_ref[bottom, :].astype(F32)
                         + jnp.where(sub_row == BF16_SUBLANES - 1, fix_last, 0.0)).astype(BF16)

    pending = [scores(0, p) for p in range(HEAD_PAIRS)]
    for j in range(ROWS_PER_BLOCK):
        for p in range(HEAD_PAIRS):
            s = pending[p]
            if j + 1 < ROWS_PER_BLOCK:
                pending[p] = scores(j + 1, p)
            attend(j, p, s)

    ya = jnp.dot(attn_buf[...], wa_ref[...], preferred_element_type=F32)
    yb = jnp.dot(bb_buf[...], wb_ref[...], preferred_element_type=F32)
    merged = sga_ref[...] * ya.astype(BF16) + sgb_ref[...] * yb.astype(BF16)
    o_ref[...] = x_ref[...] + jnp.dot(merged, wo_ref[...], preferred_element_type=F32)


def _kv_window_start(blk, rows_per_batch):
    row0 = jnp.clip(blk * ROWS_PER_BLOCK - WIN_ROWS // 2, 0, rows_per_batch - KV_WINDOW_ROWS)
    return row0 * GRID_W


def _mixer(batch, seq, qe, qo, k, v, bb, edges, sga, sgb, x, bias_rows, wa, wb, wo):
    rows = seq // GRID_W
    n_blk = rows // ROWS_PER_BLOCK
    tiles_per_seq = seq // TOK_TILE
    n_tiles = batch * tiles_per_seq

    def r3(a):
        return a.reshape(batch, seq, a.shape[-1])

    cur = lambda w: pl.BlockSpec((None, MIX_TILE, w), lambda b, i: (b, i, 0))
    kv_window = pl.BlockSpec(
        (pl.Element(1), pl.Element(KV_WINDOW_ROWS * GRID_W), pl.Element(ATTN_WIDTH)),
        lambda b, i: (b, _kv_window_start(i, rows), 0))
    def edge_spec(offset):
        def index(b, i):
            tile = b * tiles_per_seq + i // MIX_PER_TOK_TILE + offset
            return (jnp.clip(tile, 0, n_tiles - 1), 0)
        return pl.BlockSpec((EDGE_ROWS, CONV_WIDTH), index)

    aw = ATTN_WIDTH
    in_specs = [cur(aw), cur(aw), kv_window, kv_window,
                cur(CONV_WIDTH), edge_spec(-1), edge_spec(0), edge_spec(1),
                cur(D_MODEL), cur(D_MODEL), cur(D_MODEL),
                _const_spec(bias_rows.shape), _const_spec(wa.shape),
                _const_spec(wb.shape), _const_spec(wo.shape)]
    out = pl.pallas_call(
        functools.partial(_mixer_kernel, rows),
        grid=(batch, n_blk),
        in_specs=in_specs,
        out_specs=cur(D_MODEL),
        out_shape=jax.ShapeDtypeStruct((batch, seq, D_MODEL), F32),
        scratch_shapes=[pltpu.VMEM((MIX_TILE, ATTN_WIDTH), BF16),
                        pltpu.VMEM((MIX_TILE, CONV_WIDTH), BF16),
                        pltpu.VMEM((HEAD_PAIRS, N_REL_ROWS - 1, 2 * GRID_W, LANES), F32)],
        compiler_params=pltpu.CompilerParams(
            dimension_semantics=("arbitrary", "arbitrary"), vmem_limit_bytes=VMEM_LIMIT),
        name="mixer",
    )(r3(qe), r3(qo), r3(k), r3(v), r3(bb), edges, edges, edges, r3(sga), r3(sgb), x,
      bias_rows, wa, wb, wo)
    return out


def _padded_bias_rows(rpb):
    pad = GRID_W - WIN_COLS
    return jnp.pad(rpb.astype(F32), ((0, 0), (0, 0), (pad, LANES - N_REL_COLS - pad)))


def _build_bias_table(g_ref, tbl_ref):
    c = lax.broadcasted_iota(jnp.int32, (GRID_W, LANES), 0)
    x = lax.broadcasted_iota(jnp.int32, (GRID_W, LANES), 1)
    kc = x % GRID_W
    win_start = jnp.clip(c - WIN_COLS // 2, 0, GRID_W - WIN_COLS)
    in_window = (kc >= win_start) & (kc < win_start + WIN_COLS)
    left = x < GRID_W
    for h in range(N_HEADS):
        rows = slice((h % 2) * GRID_W, (h % 2 + 1) * GRID_W)
        for rr in range(N_REL_ROWS - 1):
            g0 = jnp.broadcast_to(g_ref[h, rr:rr + 1, :] * LOG2E, (GRID_W, LANES))
            g1 = jnp.broadcast_to(g_ref[h, rr + 1:rr + 2, :] * LOG2E, (GRID_W, LANES))
            a = pltpu.roll(g0, LANES - (GRID_W - 1), 1, stride=1, stride_axis=0)
            b = pltpu.roll(g1, 1, 1, stride=1, stride_axis=0)
            tbl_ref[h // 2, rr, rows, :] = jnp.where(in_window, jnp.where(left, a, b), NEG_INF)


def _mlp_kernel(x_ref, g_ref, w1_ref, w2_ref, o_ref, h_buf):
    def act(h):
        return jnp.square(jnp.maximum(h, 0.0)).astype(BF16)

    first = slice(0, FF_CHUNK)
    u_parts = []
    for r0 in range(0, TOK_TILE, NORM_ROWS):
        xf = x_ref[r0:r0 + NORM_ROWS, :]
        ms = jnp.mean(xf * xf, axis=-1, keepdims=True)
        u_part = ((xf * lax.rsqrt(ms + EPS)) * g_ref[...]).astype(BF16)
        h_buf[r0:r0 + NORM_ROWS, first] = act(
            jnp.dot(u_part, w1_ref[:, first], preferred_element_type=F32))
        u_parts.append(u_part)
    u = jnp.concatenate(u_parts, axis=0)
    for c in range(1, D_FF // FF_CHUNK):
        cols = slice(c * FF_CHUNK, (c + 1) * FF_CHUNK)
        h_buf[:, cols] = act(jnp.dot(u, w1_ref[:, cols], preferred_element_type=F32))
    o_ref[...] = x_ref[...] + jnp.dot(h_buf[...], w2_ref[...], preferred_element_type=F32)


def _mlp(x2d, g2, w1, w2):
    m = x2d.shape[0]
    tok = pl.BlockSpec((TOK_TILE, D_MODEL), lambda i: (i, 0))
    return pl.pallas_call(
        _mlp_kernel,
        grid=(m // TOK_TILE,),
        in_specs=[tok, _const_spec((1, D_MODEL)), _const_spec(w1.shape), _const_spec(w2.shape)],
        out_specs=tok,
        out_shape=jax.ShapeDtypeStruct((m, D_MODEL), F32),
        scratch_shapes=[pltpu.VMEM((TOK_TILE, D_FF), BF16)],
        compiler_params=pltpu.CompilerParams(
            dimension_semantics=("arbitrary",), vmem_limit_bytes=VMEM_LIMIT),
        name="mlp",
    )(x2d, g2, w1, w2)


def kernel(x, norm1_g, w_in, q_norm_g, k_norm_g, rpb, conv_w, w_attn_branch, w_conv_branch,
           w_o, norm2_g, w_mlp_in, w_mlp_out):
    batch, seq, _ = x.shape
    depth = w_in.shape[0]
    for l in range(depth):
        x2d = x.reshape(batch * seq, D_MODEL)
        gq = jnp.tile(q_norm_g[l], N_HEADS)[None, :]
        gk = jnp.tile(k_norm_g[l], N_HEADS)[None, :]
        assert seq % TOK_TILE == 0
        qe, qo, k, v, bb, edges, sga, sgb, wa, wb, wo, w1, w2 = _inproj(
            x2d, norm1_g[l][None, :], w_in, gq, gk, conv_w[l], l,
            (w_attn_branch, w_conv_branch, w_o, w_mlp_in, w_mlp_out))
        x = _mixer(batch, seq, qe, qo, k, v, bb, edges, sga, sgb, x,
                   _padded_bias_rows(rpb[l]), wa, wb, wo)
        x = _mlp(x.reshape(batch * seq, D_MODEL), norm2_g[l][None, :], w1, w2
                 ).reshape(batch, seq, D_MODEL)
    return x
```

```python
import functools

import numpy as np
import jax
import jax.numpy as jnp
from jax import lax
from jax.experimental import pallas as pl
from jax.experimental.pallas import tpu as pltpu

F32 = jnp.float32
BF16 = jnp.bfloat16

D_MODEL = 1024
GRID_W = 64
N_HEADS = 8
HEAD_DIM = 64
ATTN_WIDTH = N_HEADS * HEAD_DIM
CONV_WIDTH = D_MODEL // 2
WIN_ROWS = 8
WIN_COLS = 16
D_FF = 4 * D_MODEL
EPS = 1e-6
NEG_INF = -1e30
LOG2E = 1.4426950408889634
PROJ_WIDTH = 3 * ATTN_WIDTH + 3 * CONV_WIDTH + 2 * D_MODEL

LANES = 128
HEAD_PAIRS = ATTN_WIDTH // LANES
N_REL_ROWS = 2 * WIN_ROWS - 1
N_REL_COLS = 2 * WIN_COLS - 1

TOK_TILE = 1024
NORM_ROWS = 256
ROWS_PER_BLOCK = 16
MIX_TILE = ROWS_PER_BLOCK * GRID_W
KV_WINDOW_ROWS = ROWS_PER_BLOCK + WIN_ROWS
MIX_PER_TOK_TILE = TOK_TILE // MIX_TILE
assert MIX_PER_TOK_TILE * MIX_TILE == TOK_TILE
FF_CHUNK = 512
BF16_SUBLANES = 16
EDGE_ROWS = 8
VMEM_LIMIT = 56 * 1024 * 1024
INPROJ_VMEM_LIMIT = 61 * 1024 * 1024


def _const_spec(shape):
    zeros = (0,) * len(shape)
    return pl.BlockSpec(shape, lambda *_: zeros, pipeline_mode=pl.Buffered(1))


def _sigmoid(x):
    return 0.5 * jnp.tanh(0.5 * x) + 0.5


def _inproj_kernel(n_cast, x_ref, g1_ref, w_ref, gq_ref, gk_ref, cw_ref, *refs):
    cast_in, refs = refs[:n_cast], refs[n_cast:]
    q_ref, k_ref, v_ref, bb_ref, edge_ref, sga_ref, sgb_ref = refs[:7]
    cast_out = refs[7:]
    for src, dst in zip(cast_in, cast_out):
        dst[...] = src[...].astype(BF16)

    def weights(c0, width):
        return w_ref[:, c0:c0 + width].astype(BF16)

    c0 = 3 * ATTN_WIDTH
    g0 = c0 + 3 * CONV_WIDTH
    w_first = weights(g0, D_MODEL)
    u_parts = []
    for r0 in range(0, TOK_TILE, NORM_ROWS):
        xf = x_ref[r0:r0 + NORM_ROWS, :]
        ms = jnp.mean(xf * xf, axis=-1, keepdims=True)
        u_part = ((xf * lax.rsqrt(ms + EPS)) * g1_ref[...]).astype(BF16)
        sga_ref[r0:r0 + NORM_ROWS, :] = _sigmoid(
            jnp.dot(u_part, w_first, preferred_element_type=F32)).astype(BF16)
        u_parts.append(u_part)
    u = jnp.concatenate(u_parts, axis=0)

    def proj(c0, width):
        return jnp.dot(u, weights(c0, width), preferred_element_type=F32)

    first_head = lax.broadcasted_iota(jnp.int32, (TOK_TILE, LANES), 1) < HEAD_DIM

    def head_norm(t, g):
        outs = []
        for p in range(HEAD_PAIRS):
            tp = t[:, p * LANES:(p + 1) * LANES]
            t2 = tp * tp
            ss_a = jnp.sum(jnp.where(first_head, t2, 0.0), axis=-1, keepdims=True)
            ss_b = jnp.sum(jnp.where(first_head, 0.0, t2), axis=-1, keepdims=True)
            r_a = lax.rsqrt(ss_a * (1.0 / HEAD_DIM) + EPS)
            r_b = lax.rsqrt(ss_b * (1.0 / HEAD_DIM) + EPS)
            outs.append(tp * jnp.where(first_head, r_a, r_b))
        return jnp.concatenate(outs, axis=-1) * g

    sgb_ref[...] = _sigmoid(proj(g0 + D_MODEL, D_MODEL)).astype(BF16)
    q_gain = gq_ref[...] * (LOG2E * HEAD_DIM ** -0.5)
    q_ref[...] = head_norm(proj(0, ATTN_WIDTH), q_gain).astype(BF16)
    k_ref[...] = head_norm(proj(ATTN_WIDTH, ATTN_WIDTH), gk_ref[...]).astype(BF16)
    z = proj(c0 + CONV_WIDTH, CONV_WIDTH) * proj(c0 + 2 * CONV_WIDTH, CONV_WIDTH)
    cb = proj(c0, CONV_WIDTH)
    w_prev, w_mid, w_next = cw_ref[0:1, :], cw_ref[1:2, :], cw_ref[2:3, :]
    row = lax.broadcasted_iota(jnp.int32, (TOK_TILE, CONV_WIDTH), 0)
    z_m1 = jnp.where(row == 0, 0.0, pltpu.roll(z, 1, 0))
    z_p1 = jnp.where(row == TOK_TILE - 1, 0.0, pltpu.roll(z, TOK_TILE - 1, 0))
    bb_ref[...] = (cb * (z_m1 * w_prev + z * w_mid + z_p1 * w_next)).astype(BF16)
    edge_ref[...] = jnp.concatenate(
        [z[0:1], z[TOK_TILE - 1:TOK_TILE], cb[0:1] * w_prev, cb[TOK_TILE - 1:TOK_TILE] * w_next,
         jnp.zeros((EDGE_ROWS - 4, CONV_WIDTH), F32)], axis=0)
    v_ref[...] = proj(2 * ATTN_WIDTH, ATTN_WIDTH).astype(BF16)


def _inproj(x2d, g1, w_in, gq, gk, cw, layer, later_weights):
    m = x2d.shape[0]
    n_steps = m // TOK_TILE
    tok = lambda w: pl.BlockSpec((TOK_TILE, w), lambda i: (i, 0))
    act_specs = ([tok(ATTN_WIDTH)] * 3 + [tok(CONV_WIDTH)]
                 + [pl.BlockSpec((EDGE_ROWS, CONV_WIDTH), lambda i: (i, 0))] + [tok(D_MODEL)] * 2)
    act_shapes = ([jax.ShapeDtypeStruct((m, ATTN_WIDTH), BF16)] * 3
                  + [jax.ShapeDtypeStruct((m, CONV_WIDTH), BF16),
                     jax.ShapeDtypeStruct((n_steps * EDGE_ROWS, CONV_WIDTH), F32)]
                  + [jax.ShapeDtypeStruct((m, D_MODEL), BF16)] * 2)
    cast_in_specs, cast_out_specs = [], []
    for w in later_weights:
        rows = w.shape[1] // n_steps
        assert rows * n_steps == w.shape[1] and rows % BF16_SUBLANES == 0, w.shape
        cast_in_specs.append(pl.BlockSpec((None, rows, w.shape[2]), lambda i: (layer, i, 0)))
        cast_out_specs.append(pl.BlockSpec((rows, w.shape[2]), lambda i: (i, 0)))
    return pl.pallas_call(
        functools.partial(_inproj_kernel, len(later_weights)),
        grid=(n_steps,),
        in_specs=[tok(D_MODEL), _const_spec((1, D_MODEL)),
                  pl.BlockSpec((None, D_MODEL, PROJ_WIDTH), lambda i: (layer, 0, 0),
                               pipeline_mode=pl.Buffered(1)),
                  _const_spec((1, ATTN_WIDTH)), _const_spec((1, ATTN_WIDTH)),
                  _const_spec(cw.shape)] + cast_in_specs,
        out_specs=act_specs + cast_out_specs,
        out_shape=(act_shapes
                   + [jax.ShapeDtypeStruct(w.shape[1:], BF16) for w in later_weights]),
        compiler_params=pltpu.CompilerParams(
            dimension_semantics=("arbitrary",), vmem_limit_bytes=INPROJ_VMEM_LIMIT),
        name="inproj",
    )(x2d, g1, w_in, gq, gk, cw, *later_weights)


def _mixer_kernel(rows_per_batch,
                  q_ref, kw_ref, vw_ref,
                  bb_ref, ep_ref, ec_ref, en_ref, sga_ref, sgb_ref, x_ref,
                  g_ref, wa_ref, wb_ref, wo_ref,
                  o_ref, attn_buf, bb_buf, tp_ref):
    blk = pl.program_id(1)
    n_blk = pl.num_programs(1)

    @pl.when(jnp.logical_and(pl.program_id(0) == 0, blk == 0))
    def _():
        _build_bias_table(g_ref, tp_ref)

    lane = lax.broadcasted_iota(jnp.int32, (GRID_W, LANES), 1)
    first_head = lane < HEAD_DIM
    keep_first = first_head.astype(BF16)
    keep_second = 1 - keep_first
    n_keys = WIN_ROWS * GRID_W
    win_row0 = jnp.clip(blk * ROWS_PER_BLOCK - WIN_ROWS // 2, 0, rows_per_batch - KV_WINDOW_ROWS)

    def window(j):
        r = blk * ROWS_PER_BLOCK + j
        row_start = jnp.clip(r - WIN_ROWS // 2, 0, rows_per_batch - WIN_ROWS)
        ks = pl.multiple_of((row_start - win_row0) * GRID_W, GRID_W)
        return ks, r - row_start

    def scores(j, p):
        ks, d = window(j)
        cols = slice(p * LANES, (p + 1) * LANES)
        rows = slice(j * GRID_W, (j + 1) * GRID_W)
        q = q_ref[rows, cols]
        q2 = jnp.concatenate([q * keep_first, q * keep_second], axis=0)
        s = lax.dot_general(q2, kw_ref[0, pl.ds(ks, n_keys), cols], (((1,), (1,)), ((), ())),
                            preferred_element_type=F32)
        bias = jnp.concatenate(
            [tp_ref[p, WIN_ROWS - 1 - d + i] for i in range(0, WIN_ROWS, 2)], axis=-1)
        return s + bias

    def attend(j, p, s):
        ks, _ = window(j)
        cols = slice(p * LANES, (p + 1) * LANES)
        rows = slice(j * GRID_W, (j + 1) * GRID_W)
        m = jnp.max(s, axis=-1, keepdims=True)
        e = jnp.exp2(s - m)
        l = jnp.sum(e, axis=-1, keepdims=True)
        o = jnp.dot(e.astype(BF16), vw_ref[0, pl.ds(ks, n_keys), cols],
                    preferred_element_type=F32)
        o = o / l
        attn_buf[rows, cols] = jnp.where(first_head, o[:GRID_W], o[GRID_W:]).astype(BF16)

    blk_in_tile = blk % MIX_PER_TOK_TILE
    at_tile_start = jnp.logical_and(blk_in_tile == 0, blk > 0)
    at_tile_end = jnp.logical_and(blk_in_tile == MIX_PER_TOK_TILE - 1, blk < n_blk - 1)
    fix_first = jnp.where(at_tile_start, ec_ref[2:3, :] * ep_ref[1:2, :], 0.0)
    fix_last = jnp.where(at_tile_end, ec_ref[3:4, :] * en_ref[0:1, :], 0.0)
    sub_row = lax.broadcasted_iota(jnp.int32, (BF16_SUBLANES, CONV_WIDTH), 0)
    top = slice(0, BF16_SUBLANES)
    bottom = slice(MIX_TILE - BF16_SUBLANES, MIX_TILE)
    bb_buf[...] = bb_ref[...]
    bb_buf[top, :] = (bb_ref[top, :].astype(F32)
                      + jnp.where(sub_row == 0, fix_first, 0.0)).astype(BF16)
    bb_buf[bottom, :] = (bb_ref[bottom, :].astype(F32)
                         + jnp.where(sub_row == BF16_SUBLANES - 1, fix_last, 0.0)).astype(BF16)

    pending = [scores(0, p) for p in range(HEAD_PAIRS)]
    for j in range(ROWS_PER_BLOCK):
        for p in range(HEAD_PAIRS):
            s = pending[p]
            if j + 1 < ROWS_PER_BLOCK:
                pending[p] = scores(j + 1, p)
            attend(j, p, s)

    ya = jnp.dot(attn_buf[...], wa_ref[...], preferred_element_type=F32)
    yb = jnp.dot(bb_buf[...], wb_ref[...], preferred_element_type=F32)
    merged = sga_ref[...] * ya.astype(BF16) + sgb_ref[...] * yb.astype(BF16)
    o_ref[...] = x_ref[...] + jnp.dot(merged, wo_ref[...], preferred_element_type=F32)


def _kv_window_start(blk, rows_per_batch):
    row0 = jnp.clip(blk * ROWS_PER_BLOCK - WIN_ROWS // 2, 0, rows_per_batch - KV_WINDOW_ROWS)
    return row0 * GRID_W


def _mixer(batch, seq, q, k, v, bb, edges, sga, sgb, x, bias_rows, wa, wb, wo):
    rows = seq // GRID_W
    n_blk = rows // ROWS_PER_BLOCK
    tiles_per_seq = seq // TOK_TILE
    n_tiles = batch * tiles_per_seq

    def r3(a):
        return a.reshape(batch, seq, a.shape[-1])

    cur = lambda w: pl.BlockSpec((None, MIX_TILE, w), lambda b, i: (b, i, 0))
    kv_window = pl.BlockSpec(
        (pl.Element(1), pl.Element(KV_WINDOW_ROWS * GRID_W), pl.Element(ATTN_WIDTH)),
        lambda b, i: (b, _kv_window_start(i, rows), 0))
    def edge_spec(offset):
        def index(b, i):
            tile = b * tiles_per_seq + i // MIX_PER_TOK_TILE + offset
            return (jnp.clip(tile, 0, n_tiles - 1), 0)
        return pl.BlockSpec((EDGE_ROWS, CONV_WIDTH), index)

    aw = ATTN_WIDTH
    in_specs = [cur(aw), kv_window, kv_window,
                cur(CONV_WIDTH), edge_spec(-1), edge_spec(0), edge_spec(1),
                cur(D_MODEL), cur(D_MODEL), cur(D_MODEL),
                _const_spec(bias_rows.shape), _const_spec(wa.shape),
                _const_spec(wb.shape), _const_spec(wo.shape)]
    out = pl.pallas_call(
        functools.partial(_mixer_kernel, rows),
        grid=(batch, n_blk),
        in_specs=in_specs,
        out_specs=cur(D_MODEL),
        out_shape=jax.ShapeDtypeStruct((batch, seq, D_MODEL), F32),
        scratch_shapes=[pltpu.VMEM((MIX_TILE, ATTN_WIDTH), BF16),
                        pltpu.VMEM((MIX_TILE, CONV_WIDTH), BF16),
                        pltpu.VMEM((HEAD_PAIRS, N_REL_ROWS - 1, 2 * GRID_W, LANES), F32)],
        compiler_params=pltpu.CompilerParams(
            dimension_semantics=("arbitrary", "arbitrary"), vmem_limit_bytes=VMEM_LIMIT),
        name="mixer",
    )(r3(q), r3(k), r3(v), r3(bb), edges, edges, edges, r3(sga), r3(sgb), x,
      bias_rows, wa, wb, wo)
    return out


def _padded_bias_rows(rpb):
    pad = GRID_W - WIN_COLS
    return jnp.pad(rpb.astype(F32), ((0, 0), (0, 0), (pad, LANES - N_REL_COLS - pad)))


def _build_bias_table(g_ref, tbl_ref):
    c = lax.broadcasted_iota(jnp.int32, (GRID_W, LANES), 0)
    x = lax.broadcasted_iota(jnp.int32, (GRID_W, LANES), 1)
    kc = x % GRID_W
    win_start = jnp.clip(c - WIN_COLS // 2, 0, GRID_W - WIN_COLS)
    in_window = (kc >= win_start) & (kc < win_start + WIN_COLS)
    left = x < GRID_W
    for h in range(N_HEADS):
        rows = slice((h % 2) * GRID_W, (h % 2 + 1) * GRID_W)
        for rr in range(N_REL_ROWS - 1):
            g0 = jnp.broadcast_to(g_ref[h, rr:rr + 1, :] * LOG2E, (GRID_W, LANES))
            g1 = jnp.broadcast_to(g_ref[h, rr + 1:rr + 2, :] * LOG2E, (GRID_W, LANES))
            a = pltpu.roll(g0, LANES - (GRID_W - 1), 1, stride=1, stride_axis=0)
            b = pltpu.roll(g1, 1, 1, stride=1, stride_axis=0)
            tbl_ref[h // 2, rr, rows, :] = jnp.where(in_window, jnp.where(left, a, b), NEG_INF)


def _mlp_kernel(x_ref, g_ref, w1_ref, w2_ref, o_ref, h_buf):
    def act(h):
        return jnp.square(jnp.maximum(h, 0.0)).astype(BF16)

    first = slice(0, FF_CHUNK)
    u_parts = []
    for r0 in range(0, TOK_TILE, NORM_ROWS):
        xf = x_ref[r0:r0 + NORM_ROWS, :]
        ms = jnp.mean(xf * xf, axis=-1, keepdims=True)
        u_part = ((xf * lax.rsqrt(ms + EPS)) * g_ref[...]).astype(BF16)
        h_buf[r0:r0 + NORM_ROWS, first] = act(
            jnp.dot(u_part, w1_ref[:, first], preferred_element_type=F32))
        u_parts.append(u_part)
    u = jnp.concatenate(u_parts, axis=0)
    for c in range(1, D_FF // FF_CHUNK):
        cols = slice(c * FF_CHUNK, (c + 1) * FF_CHUNK)
        h_buf[:, cols] = act(jnp.dot(u, w1_ref[:, cols], preferred_element_type=F32))
    o_ref[...] = x_ref[...] + jnp.dot(h_buf[...], w2_ref[...], preferred_element_type=F32)


def _mlp(x2d, g2, w1, w2):
    m = x2d.shape[0]
    tok = pl.BlockSpec((TOK_TILE, D_MODEL), lambda i: (i, 0))
    return pl.pallas_call(
        _mlp_kernel,
        grid=(m // TOK_TILE,),
        in_specs=[tok, _const_spec((1, D_MODEL)), _const_spec(w1.shape), _const_spec(w2.shape)],
        out_specs=tok,
        out_shape=jax.ShapeDtypeStruct((m, D_MODEL), F32),
        scratch_shapes=[pltpu.VMEM((TOK_TILE, D_FF), BF16)],
        compiler_params=pltpu.CompilerParams(
            dimension_semantics=("arbitrary",), vmem_limit_bytes=VMEM_LIMIT),
        name="mlp",
    )(x2d, g2, w1, w2)


def kernel(x, norm1_g, w_in, q_norm_g, k_norm_g, rpb, conv_w, w_attn_branch, w_conv_branch,
           w_o, norm2_g, w_mlp_in, w_mlp_out):
    batch, seq, _ = x.shape
    depth = w_in.shape[0]
    for l in range(depth):
        x2d = x.reshape(batch * seq, D_MODEL)
        gq = jnp.tile(q_norm_g[l], N_HEADS)[None, :]
        gk = jnp.tile(k_norm_g[l], N_HEADS)[None, :]
        assert seq % TOK_TILE == 0
        q, k, v, bb, edges, sga, sgb, wa, wb, wo, w1, w2 = _inproj(
            x2d, norm1_g[l][None, :], w_in, gq, gk, conv_w[l], l,
            (w_attn_branch, w_conv_branch, w_o, w_mlp_in, w_mlp_out))
        x = _mixer(batch, seq, q, k, v, bb, edges, sga, sgb, x,
                   _padded_bias_rows(rpb[l]), wa, wb, wo)
        x = _mlp(x.reshape(batch * seq, D_MODEL), norm2_g[l][None, :], w1, w2
                 ).reshape(batch, seq, D_MODEL)
    return x
```

```python
import functools

import numpy as np
import jax
import jax.numpy as jnp
from jax import lax
from jax.experimental import pallas as pl
from jax.experimental.pallas import tpu as pltpu

F32 = jnp.float32
BF16 = jnp.bfloat16

D_MODEL = 1024
GRID_W = 64
N_HEADS = 8
HEAD_DIM = 64
ATTN_WIDTH = N_HEADS * HEAD_DIM
CONV_WIDTH = D_MODEL // 2
WIN_ROWS = 8
WIN_COLS = 16
D_FF = 4 * D_MODEL
EPS = 1e-6
NEG_INF = -1e30
LOG2E = 1.4426950408889634
PROJ_WIDTH = 3 * ATTN_WIDTH + 3 * CONV_WIDTH + 2 * D_MODEL

LANES = 128
HEAD_PAIRS = ATTN_WIDTH // LANES
N_REL_ROWS = 2 * WIN_ROWS - 1
N_REL_COLS = 2 * WIN_COLS - 1

TOK_TILE = 1024
NORM_ROWS = 256
ROWS_PER_BLOCK = 16
MIX_TILE = ROWS_PER_BLOCK * GRID_W
KV_WINDOW_ROWS = ROWS_PER_BLOCK + WIN_ROWS
MIX_PER_TOK_TILE = TOK_TILE // MIX_TILE
assert MIX_PER_TOK_TILE * MIX_TILE == TOK_TILE
FF_CHUNK = 512
BF16_SUBLANES = 16
EDGE_ROWS = 8
VMEM_LIMIT = 56 * 1024 * 1024
INPROJ_VMEM_LIMIT = 61 * 1024 * 1024


def _const_spec(shape):
    zeros = (0,) * len(shape)
    return pl.BlockSpec(shape, lambda *_: zeros, pipeline_mode=pl.Buffered(1))


def _sigmoid(x):
    return 0.5 * jnp.tanh(0.5 * x) + 0.5


def _inproj_kernel(n_cast, x_ref, g1_ref, w_ref, gq_ref, gk_ref, cw_ref, *refs):
    cast_in, refs = refs[:n_cast], refs[n_cast:]
    q_ref, k_ref, v_ref, bb_ref, edge_ref, sga_ref, sgb_ref = refs[:7]
    cast_out = refs[7:]
    for src, dst in zip(cast_in, cast_out):
        dst[...] = src[...].astype(BF16)

    def weights(c0, width):
        return w_ref[:, c0:c0 + width].astype(BF16)

    c0 = 3 * ATTN_WIDTH
    g0 = c0 + 3 * CONV_WIDTH
    w_first = weights(g0, D_MODEL)
    u_parts = []
    for r0 in range(0, TOK_TILE, NORM_ROWS):
        xf = x_ref[r0:r0 + NORM_ROWS, :]
        ms = jnp.mean(xf * xf, axis=-1, keepdims=True)
        u_part = ((xf * lax.rsqrt(ms + EPS)) * g1_ref[...]).astype(BF16)
        sga_ref[r0:r0 + NORM_ROWS, :] = _sigmoid(
            jnp.dot(u_part, w_first, preferred_element_type=F32)).astype(BF16)
        u_parts.append(u_part)
    u = jnp.concatenate(u_parts, axis=0)

    def proj(c0, width):
        return jnp.dot(u, weights(c0, width), preferred_element_type=F32)

    first_head = lax.broadcasted_iota(jnp.int32, (TOK_TILE, LANES), 1) < HEAD_DIM

    def head_norm(t, g):
        outs = []
        for p in range(HEAD_PAIRS):
            tp = t[:, p * LANES:(p + 1) * LANES]
            t2 = tp * tp
            ss_a = jnp.sum(jnp.where(first_head, t2, 0.0), axis=-1, keepdims=True)
            ss_b = jnp.sum(jnp.where(first_head, 0.0, t2), axis=-1, keepdims=True)
            r_a = lax.rsqrt(ss_a * (1.0 / HEAD_DIM) + EPS)
            r_b = lax.rsqrt(ss_b * (1.0 / HEAD_DIM) + EPS)
            outs.append(tp * jnp.where(first_head, r_a, r_b))
        return jnp.concatenate(outs, axis=-1) * g

    sgb_ref[...] = _sigmoid(proj(g0 + D_MODEL, D_MODEL)).astype(BF16)
    q_gain = gq_ref[...] * (LOG2E * HEAD_DIM ** -0.5)
    q_ref[...] = head_norm(proj(0, ATTN_WIDTH), q_gain).astype(BF16)
    k_ref[...] = head_norm(proj(ATTN_WIDTH, ATTN_WIDTH), gk_ref[...]).astype(BF16)
    z = proj(c0 + CONV_WIDTH, CONV_WIDTH) * proj(c0 + 2 * CONV_WIDTH, CONV_WIDTH)
    cb = proj(c0, CONV_WIDTH)
    w_prev, w_mid, w_next = cw_ref[0:1, :], cw_ref[1:2, :], cw_ref[2:3, :]
    conv = (pltpu.roll(z, 1, 0) * w_prev + z * w_mid + pltpu.roll(z, TOK_TILE - 1, 0) * w_next)
    bb_ref[...] = (cb * conv).astype(BF16)
    edge_ref[...] = jnp.concatenate(
        [z[0:1], z[TOK_TILE - 1:TOK_TILE], cb[0:1] * w_prev, cb[TOK_TILE - 1:TOK_TILE] * w_next,
         jnp.zeros((EDGE_ROWS - 4, CONV_WIDTH), F32)], axis=0)
    w_last = weights(2 * ATTN_WIDTH, ATTN_WIDTH)
    for r0, u_part in zip(range(0, TOK_TILE, NORM_ROWS), u_parts):
        v_ref[r0:r0 + NORM_ROWS, :] = jnp.dot(
            u_part, w_last, preferred_element_type=F32).astype(BF16)


def _inproj(x2d, g1, w_in, gq, gk, cw, layer, later_weights):
    m = x2d.shape[0]
    n_steps = m // TOK_TILE
    tok = lambda w: pl.BlockSpec((TOK_TILE, w), lambda i: (i, 0))
    act_specs = ([tok(ATTN_WIDTH)] * 3 + [tok(CONV_WIDTH)]
                 + [pl.BlockSpec((EDGE_ROWS, CONV_WIDTH), lambda i: (i, 0))] + [tok(D_MODEL)] * 2)
    act_shapes = ([jax.ShapeDtypeStruct((m, ATTN_WIDTH), BF16)] * 3
                  + [jax.ShapeDtypeStruct((m, CONV_WIDTH), BF16),
                     jax.ShapeDtypeStruct((n_steps * EDGE_ROWS, CONV_WIDTH), F32)]
                  + [jax.ShapeDtypeStruct((m, D_MODEL), BF16)] * 2)
    cast_in_specs, cast_out_specs = [], []
    for w in later_weights:
        rows = w.shape[1] // n_steps
        assert rows * n_steps == w.shape[1] and rows % BF16_SUBLANES == 0, w.shape
        cast_in_specs.append(pl.BlockSpec((None, rows, w.shape[2]), lambda i: (layer, i, 0)))
        cast_out_specs.append(pl.BlockSpec((rows, w.shape[2]), lambda i: (i, 0)))
    return pl.pallas_call(
        functools.partial(_inproj_kernel, len(later_weights)),
        grid=(n_steps,),
        in_specs=[tok(D_MODEL), _const_spec((1, D_MODEL)),
                  pl.BlockSpec((None, D_MODEL, PROJ_WIDTH), lambda i: (layer, 0, 0),
                               pipeline_mode=pl.Buffered(1)),
                  _const_spec((1, ATTN_WIDTH)), _const_spec((1, ATTN_WIDTH)),
                  _const_spec(cw.shape)] + cast_in_specs,
        out_specs=act_specs + cast_out_specs,
        out_shape=(act_shapes
                   + [jax.ShapeDtypeStruct(w.shape[1:], BF16) for w in later_weights]),
        compiler_params=pltpu.CompilerParams(
            dimension_semantics=("arbitrary",), vmem_limit_bytes=INPROJ_VMEM_LIMIT),
        name="inproj",
    )(x2d, g1, w_in, gq, gk, cw, *later_weights)


def _mixer_kernel(rows_per_batch,
                  q_ref, kw_ref, vw_ref,
                  bb_ref, ep_ref, ec_ref, en_ref, sga_ref, sgb_ref, x_ref,
                  g_ref, wa_ref, wb_ref, wo_ref,
                  o_ref, attn_buf, bb_buf, tp_ref):
    blk = pl.program_id(1)
    n_blk = pl.num_programs(1)

    @pl.when(jnp.logical_and(pl.program_id(0) == 0, blk == 0))
    def _():
        _build_bias_table(g_ref, tp_ref)

    lane = lax.broadcasted_iota(jnp.int32, (GRID_W, LANES), 1)
    first_head = lane < HEAD_DIM
    keep_first = first_head.astype(BF16)
    keep_second = 1 - keep_first
    n_keys = WIN_ROWS * GRID_W
    win_row0 = jnp.clip(blk * ROWS_PER_BLOCK - WIN_ROWS // 2, 0, rows_per_batch - KV_WINDOW_ROWS)

    def window(j):
        r = blk * ROWS_PER_BLOCK + j
        row_start = jnp.clip(r - WIN_ROWS // 2, 0, rows_per_batch - WIN_ROWS)
        ks = pl.multiple_of((row_start - win_row0) * GRID_W, GRID_W)
        return ks, r - row_start

    def scores(j, p):
        ks, d = window(j)
        cols = slice(p * LANES, (p + 1) * LANES)
        rows = slice(j * GRID_W, (j + 1) * GRID_W)
        q = q_ref[rows, cols]
        q2 = jnp.concatenate([q * keep_first, q * keep_second], axis=0)
        s = lax.dot_general(q2, kw_ref[0, pl.ds(ks, n_keys), cols], (((1,), (1,)), ((), ())),
                            preferred_element_type=F32)
        bias = jnp.concatenate(
            [tp_ref[p, WIN_ROWS - 1 - d + i] for i in range(0, WIN_ROWS, 2)], axis=-1)
        return s + bias

    def attend(j, p, s):
        ks, _ = window(j)
        cols = slice(p * LANES, (p + 1) * LANES)
        rows = slice(j * GRID_W, (j + 1) * GRID_W)
        m = jnp.max(s, axis=-1, keepdims=True)
        e = jnp.exp2(s - m)
        l = jnp.sum(e, axis=-1, keepdims=True)
        o = jnp.dot(e.astype(BF16), vw_ref[0, pl.ds(ks, n_keys), cols],
                    preferred_element_type=F32)
        o = o / l
        attn_buf[rows, cols] = jnp.where(first_head, o[:GRID_W], o[GRID_W:]).astype(BF16)

    blk_in_tile = blk % MIX_PER_TOK_TILE
    z_before = jnp.where(blk > 0, ep_ref[1:2, :], 0.0)
    z_after = jnp.where(blk < n_blk - 1, en_ref[0:1, :], 0.0)
    fix_first = jnp.where(blk_in_tile == 0,
                          ec_ref[2:3, :] * (z_before - ec_ref[1:2, :]), 0.0)
    fix_last = jnp.where(blk_in_tile == MIX_PER_TOK_TILE - 1,
                         ec_ref[3:4, :] * (z_after - ec_ref[0:1, :]), 0.0)
    sub_row = lax.broadcasted_iota(jnp.int32, (BF16_SUBLANES, CONV_WIDTH), 0)
    top = slice(0, BF16_SUBLANES)
    bottom = slice(MIX_TILE - BF16_SUBLANES, MIX_TILE)
    bb_buf[...] = bb_ref[...]
    bb_buf[top, :] = (bb_ref[top, :].astype(F32)
                      + jnp.where(sub_row == 0, fix_first, 0.0)).astype(BF16)
    bb_buf[bottom, :] = (bb_ref[bottom, :].astype(F32)
                         + jnp.where(sub_row == BF16_SUBLANES - 1, fix_last, 0.0)).astype(BF16)

    pending = [scores(0, p) for p in range(HEAD_PAIRS)]
    for j in range(ROWS_PER_BLOCK):
        for p in range(HEAD_PAIRS):
            s = pending[p]
            if j + 1 < ROWS_PER_BLOCK:
                pending[p] = scores(j + 1, p)
            attend(j, p, s)

    ya = jnp.dot(attn_buf[...], wa_ref[...], preferred_element_type=F32)
    yb = jnp.dot(bb_buf[...], wb_ref[...], preferred_element_type=F32)
    merged = sga_ref[...] * ya.astype(BF16) + sgb_ref[...] * yb.astype(BF16)
    o_ref[...] = x_ref[...] + jnp.dot(merged, wo_ref[...], preferred_element_type=F32)


def _kv_window_start(blk, rows_per_batch):
    row0 = jnp.clip(blk * ROWS_PER_BLOCK - WIN_ROWS // 2, 0, rows_per_batch - KV_WINDOW_ROWS)
    return row0 * GRID_W


def _mixer(batch, seq, q, k, v, bb, edges, sga, sgb, x, bias_rows, wa, wb, wo):
    rows = seq // GRID_W
    n_blk = rows // ROWS_PER_BLOCK
    tiles_per_seq = seq // TOK_TILE
    n_tiles = batch * tiles_per_seq

    def r3(a):
        return a.reshape(batch, seq, a.shape[-1])

    cur = lambda w: pl.BlockSpec((None, MIX_TILE, w), lambda b, i: (b, i, 0))
    kv_window = pl.BlockSpec(
        (pl.Element(1), pl.Element(KV_WINDOW_ROWS * GRID_W), pl.Element(ATTN_WIDTH)),
        lambda b, i: (b, _kv_window_start(i, rows), 0))
    def edge_spec(offset):
        def index(b, i):
            tile = b * tiles_per_seq + i // MIX_PER_TOK_TILE + offset
            return (jnp.clip(tile, 0, n_tiles - 1), 0)
        return pl.BlockSpec((EDGE_ROWS, CONV_WIDTH), index)

    aw = ATTN_WIDTH
    in_specs = [cur(aw), kv_window, kv_window,
                cur(CONV_WIDTH), edge_spec(-1), edge_spec(0), edge_spec(1),
                cur(D_MODEL), cur(D_MODEL), cur(D_MODEL),
                _const_spec(bias_rows.shape), _const_spec(wa.shape),
                _const_spec(wb.shape), _const_spec(wo.shape)]
    out = pl.pallas_call(
        functools.partial(_mixer_kernel, rows),
        grid=(batch, n_blk),
        in_specs=in_specs,
        out_specs=cur(D_MODEL),
        out_shape=jax.ShapeDtypeStruct((batch, seq, D_MODEL), F32),
        scratch_shapes=[pltpu.VMEM((MIX_TILE, ATTN_WIDTH), BF16),
                        pltpu.VMEM((MIX_TILE, CONV_WIDTH), BF16),
                        pltpu.VMEM((HEAD_PAIRS, N_REL_ROWS - 1, 2 * GRID_W, LANES), F32)],
        compiler_params=pltpu.CompilerParams(
            dimension_semantics=("arbitrary", "arbitrary"), vmem_limit_bytes=VMEM_LIMIT),
        name="mixer",
    )(r3(q), r3(k), r3(v), r3(bb), edges, edges, edges, r3(sga), r3(sgb), x,
      bias_rows, wa, wb, wo)
    return out


def _padded_bias_rows(rpb):
    pad = GRID_W - WIN_COLS
    return jnp.pad(rpb.astype(F32), ((0, 0), (0, 0), (pad, LANES - N_REL_COLS - pad)))


def _build_bias_table(g_ref, tbl_ref):
    c = lax.broadcasted_iota(jnp.int32, (GRID_W, LANES), 0)
    x = lax.broadcasted_iota(jnp.int32, (GRID_W, LANES), 1)
    kc = x % GRID_W
    win_start = jnp.clip(c - WIN_COLS // 2, 0, GRID_W - WIN_COLS)
    in_window = (kc >= win_start) & (kc < win_start + WIN_COLS)
    left = x < GRID_W
    for h in range(N_HEADS):
        rows = slice((h % 2) * GRID_W, (h % 2 + 1) * GRID_W)
        for rr in range(N_REL_ROWS - 1):
            g0 = jnp.broadcast_to(g_ref[h, rr:rr + 1, :] * LOG2E, (GRID_W, LANES))
            g1 = jnp.broadcast_to(g_ref[h, rr + 1:rr + 2, :] * LOG2E, (GRID_W, LANES))
            a = pltpu.roll(g0, LANES - (GRID_W - 1), 1, stride=1, stride_axis=0)
            b = pltpu.roll(g1, 1, 1, stride=1, stride_axis=0)
            tbl_ref[h // 2, rr, rows, :] = jnp.where(in_window, jnp.where(left, a, b), NEG_INF)


def _mlp_kernel(x_ref, g_ref, w1_ref, w2_ref, o_ref, h_buf):
    def act(h):
        return jnp.square(jnp.maximum(h, 0.0)).astype(BF16)

    first = slice(0, FF_CHUNK)
    u_parts = []
    for r0 in range(0, TOK_TILE, NORM_ROWS):
        xf = x_ref[r0:r0 + NORM_ROWS, :]
        ms = jnp.mean(xf * xf, axis=-1, keepdims=True)
        u_part = ((xf * lax.rsqrt(ms + EPS)) * g_ref[...]).astype(BF16)
        h_buf[r0:r0 + NORM_ROWS, first] = act(
            jnp.dot(u_part, w1_ref[:, first], preferred_element_type=F32))
        u_parts.append(u_part)
    u = jnp.concatenate(u_parts, axis=0)
    for c in range(1, D_FF // FF_CHUNK):
        cols = slice(c * FF_CHUNK, (c + 1) * FF_CHUNK)
        h_buf[:, cols] = act(jnp.dot(u, w1_ref[:, cols], preferred_element_type=F32))
    o_ref[...] = x_ref[...] + jnp.dot(h_buf[...], w2_ref[...], preferred_element_type=F32)


def _mlp(x2d, g2, w1, w2):
    m = x2d.shape[0]
    tok = pl.BlockSpec((TOK_TILE, D_MODEL), lambda i: (i, 0))
    return pl.pallas_call(
        _mlp_kernel,
        grid=(m // TOK_TILE,),
        in_specs=[tok, _const_spec((1, D_MODEL)), _const_spec(w1.shape), _const_spec(w2.shape)],
        out_specs=tok,
        out_shape=jax.ShapeDtypeStruct((m, D_MODEL), F32),
        scratch_shapes=[pltpu.VMEM((TOK_TILE, D_FF), BF16)],
        compiler_params=pltpu.CompilerParams(
            dimension_semantics=("arbitrary",), vmem_limit_bytes=VMEM_LIMIT),
        name="mlp",
    )(x2d, g2, w1, w2)


def kernel(x, norm1_g, w_in, q_norm_g, k_norm_g, rpb, conv_w, w_attn_branch, w_conv_branch,
           w_o, norm2_g, w_mlp_in, w_mlp_out):
    batch, seq, _ = x.shape
    depth = w_in.shape[0]
    for l in range(depth):
        x2d = x.reshape(batch * seq, D_MODEL)
        gq = jnp.tile(q_norm_g[l], N_HEADS)[None, :]
        gk = jnp.tile(k_norm_g[l], N_HEADS)[None, :]
        assert seq % TOK_TILE == 0
        q, k, v, bb, edges, sga, sgb, wa, wb, wo, w1, w2 = _inproj(
            x2d, norm1_g[l][None, :], w_in, gq, gk, conv_w[l], l,
            (w_attn_branch, w_conv_branch, w_o, w_mlp_in, w_mlp_out))
        x = _mixer(batch, seq, q, k, v, bb, edges, sga, sgb, x,
                   _padded_bias_rows(rpb[l]), wa, wb, wo)
        x = _mlp(x.reshape(batch * seq, D_MODEL), norm2_g[l][None, :], w1, w2
                 ).reshape(batch, seq, D_MODEL)
    return x
```

```python
import functools

import numpy as np
import jax
import jax.numpy as jnp
from jax import lax
from jax.experimental import pallas as pl
from jax.experimental.pallas import tpu as pltpu

F32 = jnp.float32
BF16 = jnp.bfloat16

D_MODEL = 1024
GRID_W = 64
N_HEADS = 8
HEAD_DIM = 64
ATTN_WIDTH = N_HEADS * HEAD_DIM
CONV_WIDTH = D_MODEL // 2
WIN_ROWS = 8
WIN_COLS = 16
D_FF = 4 * D_MODEL
EPS = 1e-6
NEG_INF = -1e30
LOG2E = 1.4426950408889634
PROJ_WIDTH = 3 * ATTN_WIDTH + 3 * CONV_WIDTH + 2 * D_MODEL

LANES = 128
HEAD_PAIRS = ATTN_WIDTH // LANES
N_REL_ROWS = 2 * WIN_ROWS - 1
N_REL_COLS = 2 * WIN_COLS - 1

TOK_TILE = 1024
NORM_ROWS = 256
ROWS_PER_BLOCK = 16
MIX_TILE = ROWS_PER_BLOCK * GRID_W
KV_WINDOW_ROWS = ROWS_PER_BLOCK + WIN_ROWS
MIX_PER_TOK_TILE = TOK_TILE // MIX_TILE
assert MIX_PER_TOK_TILE * MIX_TILE == TOK_TILE
FF_CHUNK = 512
BF16_SUBLANES = 16
EDGE_ROWS = 8
VMEM_LIMIT = 56 * 1024 * 1024
INPROJ_VMEM_LIMIT = 61 * 1024 * 1024


def _const_spec(shape):
    zeros = (0,) * len(shape)
    return pl.BlockSpec(shape, lambda *_: zeros, pipeline_mode=pl.Buffered(1))


def _sigmoid(x):
    return 0.5 * jnp.tanh(0.5 * x) + 0.5


_GATE0 = 3 * ATTN_WIDTH + 3 * CONV_WIDTH
_CONV0 = 3 * ATTN_WIDTH
_W_GROUPS = {
    "gate_a": (_GATE0, D_MODEL), "gate_b": (_GATE0 + D_MODEL, D_MODEL),
    "q": (0, ATTN_WIDTH), "k": (ATTN_WIDTH, ATTN_WIDTH),
    "cc": (_CONV0 + CONV_WIDTH, CONV_WIDTH), "ch": (_CONV0 + 2 * CONV_WIDTH, CONV_WIDTH),
    "cb": (_CONV0, CONV_WIDTH), "v": (2 * ATTN_WIDTH, ATTN_WIDTH),
}


def _inproj_kernel(n_cast, layer, x_ref, g1_ref, w_hbm, gq_ref, gk_ref, cw_ref, *refs):
    cast_in, refs = refs[:n_cast], refs[n_cast:]
    q_ref, k_ref, v_ref, bb_ref, edge_ref, sga_ref, sgb_ref = refs[:7]
    cast_out = refs[7:7 + n_cast]
    w_buf, sems = refs[7 + n_cast:]

    def group_copy(name):
        c0, width = _W_GROUPS[name]
        cols = pl.ds(c0, width)
        return pltpu.make_async_copy(w_hbm.at[layer, :, cols], w_buf.at[:, cols],
                                     sems.at[list(_W_GROUPS).index(name)])

    def body(first_step):
        if first_step:
            for name in _W_GROUPS:
                group_copy(name).start()

        for src, dst in zip(cast_in, cast_out):
            dst[...] = src[...].astype(BF16)

        def weights(name):
            if first_step:
                group_copy(name).wait()
            c0, width = _W_GROUPS[name]
            return w_buf[:, c0:c0 + width].astype(BF16)

        u_parts = []
        w_first = None
        for r0 in range(0, TOK_TILE, NORM_ROWS):
            xf = x_ref[r0:r0 + NORM_ROWS, :]
            ms = jnp.mean(xf * xf, axis=-1, keepdims=True)
            u_part = ((xf * lax.rsqrt(ms + EPS)) * g1_ref[...]).astype(BF16)
            if w_first is None:
                w_first = weights("gate_a")
            sga_ref[r0:r0 + NORM_ROWS, :] = _sigmoid(
                jnp.dot(u_part, w_first, preferred_element_type=F32)).astype(BF16)
            u_parts.append(u_part)
        u = jnp.concatenate(u_parts, axis=0)

        def proj(name):
            return jnp.dot(u, weights(name), preferred_element_type=F32)

        first_head = lax.broadcasted_iota(jnp.int32, (TOK_TILE, LANES), 1) < HEAD_DIM

        def head_norm(t, g):
            outs = []
            for p in range(HEAD_PAIRS):
                tp = t[:, p * LANES:(p + 1) * LANES]
                t2 = tp * tp
                ss_a = jnp.sum(jnp.where(first_head, t2, 0.0), axis=-1, keepdims=True)
                ss_b = jnp.sum(jnp.where(first_head, 0.0, t2), axis=-1, keepdims=True)
                r_a = lax.rsqrt(ss_a * (1.0 / HEAD_DIM) + EPS)
                r_b = lax.rsqrt(ss_b * (1.0 / HEAD_DIM) + EPS)
                outs.append(tp * jnp.where(first_head, r_a, r_b))
            return jnp.concatenate(outs, axis=-1) * g

        sgb_ref[...] = _sigmoid(proj("gate_b")).astype(BF16)
        q_gain = gq_ref[...] * (LOG2E * HEAD_DIM ** -0.5)
        q_ref[...] = head_norm(proj("q"), q_gain).astype(BF16)
        k_ref[...] = head_norm(proj("k"), gk_ref[...]).astype(BF16)
        z = proj("cc") * proj("ch")
        cb = proj("cb")
        w_prev, w_mid, w_next = cw_ref[0:1, :], cw_ref[1:2, :], cw_ref[2:3, :]
        row = lax.broadcasted_iota(jnp.int32, (TOK_TILE, CONV_WIDTH), 0)
        z_m1 = jnp.where(row == 0, 0.0, pltpu.roll(z, 1, 0))
        z_p1 = jnp.where(row == TOK_TILE - 1, 0.0, pltpu.roll(z, TOK_TILE - 1, 0))
        bb_ref[...] = (cb * (z_m1 * w_prev + z * w_mid + z_p1 * w_next)).astype(BF16)
        edge_ref[...] = jnp.concatenate(
            [z[0:1], z[TOK_TILE - 1:TOK_TILE], cb[0:1] * w_prev,
             cb[TOK_TILE - 1:TOK_TILE] * w_next,
             jnp.zeros((EDGE_ROWS - 4, CONV_WIDTH), F32)], axis=0)
        v_ref[...] = proj("v").astype(BF16)

    step = pl.program_id(0)
    pl.when(step == 0)(functools.partial(body, True))
    pl.when(step > 0)(functools.partial(body, False))


def _inproj(x2d, g1, w_in, gq, gk, cw, layer, later_weights):
    m = x2d.shape[0]
    n_steps = m // TOK_TILE
    tok = lambda w: pl.BlockSpec((TOK_TILE, w), lambda i: (i, 0))
    act_specs = ([tok(ATTN_WIDTH)] * 3 + [tok(CONV_WIDTH)]
                 + [pl.BlockSpec((EDGE_ROWS, CONV_WIDTH), lambda i: (i, 0))] + [tok(D_MODEL)] * 2)
    act_shapes = ([jax.ShapeDtypeStruct((m, ATTN_WIDTH), BF16)] * 3
                  + [jax.ShapeDtypeStruct((m, CONV_WIDTH), BF16),
                     jax.ShapeDtypeStruct((n_steps * EDGE_ROWS, CONV_WIDTH), F32)]
                  + [jax.ShapeDtypeStruct((m, D_MODEL), BF16)] * 2)
    cast_in_specs, cast_out_specs = [], []
    for w in later_weights:
        rows = w.shape[1] // n_steps
        assert rows * n_steps == w.shape[1] and rows % BF16_SUBLANES == 0, w.shape
        cast_in_specs.append(pl.BlockSpec((None, rows, w.shape[2]), lambda i: (layer, i, 0)))
        cast_out_specs.append(pl.BlockSpec((rows, w.shape[2]), lambda i: (i, 0)))
    return pl.pallas_call(
        functools.partial(_inproj_kernel, len(later_weights), layer),
        grid=(n_steps,),
        in_specs=[tok(D_MODEL), _const_spec((1, D_MODEL)),
                  pl.BlockSpec(memory_space=pl.ANY),
                  _const_spec((1, ATTN_WIDTH)), _const_spec((1, ATTN_WIDTH)),
                  _const_spec(cw.shape)] + cast_in_specs,
        out_specs=act_specs + cast_out_specs,
        out_shape=(act_shapes
                   + [jax.ShapeDtypeStruct(w.shape[1:], BF16) for w in later_weights]),
        compiler_params=pltpu.CompilerParams(
            dimension_semantics=("arbitrary",), vmem_limit_bytes=INPROJ_VMEM_LIMIT),
        scratch_shapes=[pltpu.VMEM((D_MODEL, PROJ_WIDTH), F32),
                        pltpu.SemaphoreType.DMA((len(_W_GROUPS),))],
        name="inproj",
    )(x2d, g1, w_in, gq, gk, cw, *later_weights)


def _mixer_kernel(rows_per_batch,
                  q_ref, kw_ref, vw_ref,
                  bb_ref, ep_ref, ec_ref, en_ref, sga_ref, sgb_ref, x_ref,
                  g_ref, wa_ref, wb_ref, wo_ref,
                  o_ref, attn_buf, bb_buf, tp_ref):
    blk = pl.program_id(1)
    n_blk = pl.num_programs(1)

    @pl.when(jnp.logical_and(pl.program_id(0) == 0, blk == 0))
    def _():
        _build_bias_table(g_ref, tp_ref)

    lane = lax.broadcasted_iota(jnp.int32, (GRID_W, LANES), 1)
    first_head = lane < HEAD_DIM
    keep_first = first_head.astype(BF16)
    keep_second = 1 - keep_first
    n_keys = WIN_ROWS * GRID_W
    win_row0 = jnp.clip(blk * ROWS_PER_BLOCK - WIN_ROWS // 2, 0, rows_per_batch - KV_WINDOW_ROWS)

    def window(j):
        r = blk * ROWS_PER_BLOCK + j
        row_start = jnp.clip(r - WIN_ROWS // 2, 0, rows_per_batch - WIN_ROWS)
        ks = pl.multiple_of((row_start - win_row0) * GRID_W, GRID_W)
        return ks, r - row_start

    def scores(j, p):
        ks, d = window(j)
        cols = slice(p * LANES, (p + 1) * LANES)
        rows = slice(j * GRID_W, (j + 1) * GRID_W)
        q = q_ref[rows, cols]
        q2 = jnp.concatenate([q * keep_first, q * keep_second], axis=0)
        s = lax.dot_general(q2, kw_ref[0, pl.ds(ks, n_keys), cols], (((1,), (1,)), ((), ())),
                            preferred_element_type=F32)
        bias = jnp.concatenate(
            [tp_ref[p, WIN_ROWS - 1 - d + i] for i in range(0, WIN_ROWS, 2)], axis=-1)
        return s + bias

    def attend(j, p, s):
        ks, _ = window(j)
        cols = slice(p * LANES, (p + 1) * LANES)
        rows = slice(j * GRID_W, (j + 1) * GRID_W)
        m = jnp.max(s, axis=-1, keepdims=True)
        e = jnp.exp2(s - m)
        l = jnp.sum(e, axis=-1, keepdims=True)
        o = jnp.dot(e.astype(BF16), vw_ref[0, pl.ds(ks, n_keys), cols],
                    preferred_element_type=F32)
        o = o / l
        attn_buf[rows, cols] = jnp.where(first_head, o[:GRID_W], o[GRID_W:]).astype(BF16)

    blk_in_tile = blk % MIX_PER_TOK_TILE
    at_tile_start = jnp.logical_and(blk_in_tile == 0, blk > 0)
    at_tile_end = jnp.logical_and(blk_in_tile == MIX_PER_TOK_TILE - 1, blk < n_blk - 1)
    fix_first = jnp.where(at_tile_start, ec_ref[2:3, :] * ep_ref[1:2, :], 0.0)
    fix_last = jnp.where(at_tile_end, ec_ref[3:4, :] * en_ref[0:1, :], 0.0)
    sub_row = lax.broadcasted_iota(jnp.int32, (BF16_SUBLANES, CONV_WIDTH), 0)
    top = slice(0, BF16_SUBLANES)
    bottom = slice(MIX_TILE - BF16_SUBLANES, MIX_TILE)
    bb_buf[...] = bb_ref[...]
    bb_buf[top, :] = (bb_ref[top, :].astype(F32)
                      + jnp.where(sub_row == 0, fix_first, 0.0)).astype(BF16)
    bb_buf[bottom, :] = (bb_ref[bottom, :].astype(F32)
                         + jnp.where(sub_row == BF16_SUBLANES - 1, fix_last, 0.0)).astype(BF16)

    pending = [scores(0, p) for p in range(HEAD_PAIRS)]
    for j in range(ROWS_PER_BLOCK):
        for p in range(HEAD_PAIRS):
            s = pending[p]
            if j + 1 < ROWS_PER_BLOCK:
                pending[p] = scores(j + 1, p)
            attend(j, p, s)

    ya = jnp.dot(attn_buf[...], wa_ref[...], preferred_element_type=F32)
    yb = jnp.dot(bb_buf[...], wb_ref[...], preferred_element_type=F32)
    merged = sga_ref[...] * ya.astype(BF16) + sgb_ref[...] * yb.astype(BF16)
    o_ref[...] = x_ref[...] + jnp.dot(merged, wo_ref[...], preferred_element_type=F32)


def _kv_window_start(blk, rows_per_batch):
    row0 = jnp.clip(blk * ROWS_PER_BLOCK - WIN_ROWS // 2, 0, rows_per_batch - KV_WINDOW_ROWS)
    return row0 * GRID_W


def _mixer(batch, seq, q, k, v, bb, edges, sga, sgb, x, bias_rows, wa, wb, wo):
    rows = seq // GRID_W
    n_blk = rows // ROWS_PER_BLOCK
    tiles_per_seq = seq // TOK_TILE
    n_tiles = batch * tiles_per_seq

    def r3(a):
        return a.reshape(batch, seq, a.shape[-1])

    cur = lambda w: pl.BlockSpec((None, MIX_TILE, w), lambda b, i: (b, i, 0))
    kv_window = pl.BlockSpec(
        (pl.Element(1), pl.Element(KV_WINDOW_ROWS * GRID_W), pl.Element(ATTN_WIDTH)),
        lambda b, i: (b, _kv_window_start(i, rows), 0))
    def edge_spec(offset):
        def index(b, i):
            tile = b * tiles_per_seq + i // MIX_PER_TOK_TILE + offset
            return (jnp.clip(tile, 0, n_tiles - 1), 0)
        return pl.BlockSpec((EDGE_ROWS, CONV_WIDTH), index)

    aw = ATTN_WIDTH
    in_specs = [cur(aw), kv_window, kv_window,
                cur(CONV_WIDTH), edge_spec(-1), edge_spec(0), edge_spec(1),
                cur(D_MODEL), cur(D_MODEL), cur(D_MODEL),
                _const_spec(bias_rows.shape), _const_spec(wa.shape),
                _const_spec(wb.shape), _const_spec(wo.shape)]
    out = pl.pallas_call(
        functools.partial(_mixer_kernel, rows),
        grid=(batch, n_blk),
        in_specs=in_specs,
        out_specs=cur(D_MODEL),
        out_shape=jax.ShapeDtypeStruct((batch, seq, D_MODEL), F32),
        scratch_shapes=[pltpu.VMEM((MIX_TILE, ATTN_WIDTH), BF16),
                        pltpu.VMEM((MIX_TILE, CONV_WIDTH), BF16),
                        pltpu.VMEM((HEAD_PAIRS, N_REL_ROWS - 1, 2 * GRID_W, LANES), F32)],
        compiler_params=pltpu.CompilerParams(
            dimension_semantics=("arbitrary", "arbitrary"), vmem_limit_bytes=VMEM_LIMIT),
        name="mixer",
    )(r3(q), r3(k), r3(v), r3(bb), edges, edges, edges, r3(sga), r3(sgb), x,
      bias_rows, wa, wb, wo)
    return out


def _padded_bias_rows(rpb):
    pad = GRID_W - WIN_COLS
    return jnp.pad(rpb.astype(F32), ((0, 0), (0, 0), (pad, LANES - N_REL_COLS - pad)))


def _build_bias_table(g_ref, tbl_ref):
    c = lax.broadcasted_iota(jnp.int32, (GRID_W, LANES), 0)
    x = lax.broadcasted_iota(jnp.int32, (GRID_W, LANES), 1)
    kc = x % GRID_W
    win_start = jnp.clip(c - WIN_COLS // 2, 0, GRID_W - WIN_COLS)
    in_window = (kc >= win_start) & (kc < win_start + WIN_COLS)
    left = x < GRID_W
    for h in range(N_HEADS):
        rows = slice((h % 2) * GRID_W, (h % 2 + 1) * GRID_W)
        for rr in range(N_REL_ROWS - 1):
            g0 = jnp.broadcast_to(g_ref[h, rr:rr + 1, :] * LOG2E, (GRID_W, LANES))
            g1 = jnp.broadcast_to(g_ref[h, rr + 1:rr + 2, :] * LOG2E, (GRID_W, LANES))
            a = pltpu.roll(g0, LANES - (GRID_W - 1), 1, stride=1, stride_axis=0)
            b = pltpu.roll(g1, 1, 1, stride=1, stride_axis=0)
            tbl_ref[h // 2, rr, rows, :] = jnp.where(in_window, jnp.where(left, a, b), NEG_INF)


def _mlp_kernel(x_ref, g_ref, w1_hbm, w2_hbm, o_ref, h_buf, w1_buf, w2_buf, sems):
    n_chunks = D_FF // FF_CHUNK

    def w1_copy(c):
        cols = pl.ds(c * FF_CHUNK, FF_CHUNK)
        return pltpu.make_async_copy(w1_hbm.at[:, cols], w1_buf.at[:, cols], sems.at[c])

    def w2_copy():
        return pltpu.make_async_copy(w2_hbm, w2_buf, sems.at[n_chunks])

    def act(h):
        return jnp.square(jnp.maximum(h, 0.0)).astype(BF16)

    def body(first_step):
        if first_step:
            for c in range(n_chunks):
                w1_copy(c).start()
            w2_copy().start()
        first = slice(0, FF_CHUNK)
        u_parts = []
        for r0 in range(0, TOK_TILE, NORM_ROWS):
            xf = x_ref[r0:r0 + NORM_ROWS, :]
            ms = jnp.mean(xf * xf, axis=-1, keepdims=True)
            u_part = ((xf * lax.rsqrt(ms + EPS)) * g_ref[...]).astype(BF16)
            if first_step and r0 == 0:
                w1_copy(0).wait()
            h_buf[r0:r0 + NORM_ROWS, first] = act(
                jnp.dot(u_part, w1_buf[:, first], preferred_element_type=F32))
            u_parts.append(u_part)
        u = jnp.concatenate(u_parts, axis=0)
        for c in range(1, n_chunks):
            if first_step:
                w1_copy(c).wait()
            cols = slice(c * FF_CHUNK, (c + 1) * FF_CHUNK)
            h_buf[:, cols] = act(jnp.dot(u, w1_buf[:, cols], preferred_element_type=F32))
        if first_step:
            w2_copy().wait()
        o_ref[...] = x_ref[...] + jnp.dot(h_buf[...], w2_buf[...], preferred_element_type=F32)

    step = pl.program_id(0)
    pl.when(step == 0)(functools.partial(body, True))
    pl.when(step > 0)(functools.partial(body, False))


def _mlp(x2d, g2, w1, w2):
    m = x2d.shape[0]
    tok = pl.BlockSpec((TOK_TILE, D_MODEL), lambda i: (i, 0))
    whole = pl.BlockSpec(memory_space=pl.ANY)
    return pl.pallas_call(
        _mlp_kernel,
        grid=(m // TOK_TILE,),
        in_specs=[tok, _const_spec((1, D_MODEL)), whole, whole],
        out_specs=tok,
        out_shape=jax.ShapeDtypeStruct((m, D_MODEL), F32),
        scratch_shapes=[pltpu.VMEM((TOK_TILE, D_FF), BF16),
                        pltpu.VMEM(w1.shape, BF16),
                        pltpu.VMEM(w2.shape, BF16),
                        pltpu.SemaphoreType.DMA((D_FF // FF_CHUNK + 1,))],
        compiler_params=pltpu.CompilerParams(
            dimension_semantics=("arbitrary",), vmem_limit_bytes=VMEM_LIMIT),
        name="mlp",
    )(x2d, g2, w1, w2)


def kernel(x, norm1_g, w_in, q_norm_g, k_norm_g, rpb, conv_w, w_attn_branch, w_conv_branch,
           w_o, norm2_g, w_mlp_in, w_mlp_out):
    batch, seq, _ = x.shape
    depth = w_in.shape[0]
    for l in range(depth):
        x2d = x.reshape(batch * seq, D_MODEL)
        gq = jnp.tile(q_norm_g[l], N_HEADS)[None, :]
        gk = jnp.tile(k_norm_g[l], N_HEADS)[None, :]
        assert seq % TOK_TILE == 0
        q, k, v, bb, edges, sga, sgb, wa, wb, wo, w1, w2 = _inproj(
            x2d, norm1_g[l][None, :], w_in, gq, gk, conv_w[l], l,
            (w_attn_branch, w_conv_branch, w_o, w_mlp_in, w_mlp_out))
        x = _mixer(batch, seq, q, k, v, bb, edges, sga, sgb, x,
                   _padded_bias_rows(rpb[l]), wa, wb, wo)
        x = _mlp(x.reshape(batch * seq, D_MODEL), norm2_g[l][None, :], w1, w2
                 ).reshape(batch, seq, D_MODEL)
    return x
```

```python
import functools

import numpy as np
import jax
import jax.numpy as jnp
from jax import lax
from jax.experimental import pallas as pl
from jax.experimental.pallas import tpu as pltpu

F32 = jnp.float32
BF16 = jnp.bfloat16

D_MODEL = 1024
GRID_W = 64
N_HEADS = 8
HEAD_DIM = 64
ATTN_WIDTH = N_HEADS * HEAD_DIM
CONV_WIDTH = D_MODEL // 2
WIN_ROWS = 8
WIN_COLS = 16
D_FF = 4 * D_MODEL
EPS = 1e-6
NEG_INF = -1e30
LOG2E = 1.4426950408889634
PROJ_WIDTH = 3 * ATTN_WIDTH + 3 * CONV_WIDTH + 2 * D_MODEL

LANES = 128
HEAD_PAIRS = ATTN_WIDTH // LANES
N_REL_ROWS = 2 * WIN_ROWS - 1
N_REL_COLS = 2 * WIN_COLS - 1

TOK_TILE = 1024
NORM_ROWS = 256
ROWS_PER_BLOCK = 16
MIX_TILE = ROWS_PER_BLOCK * GRID_W
KV_WINDOW_ROWS = ROWS_PER_BLOCK + WIN_ROWS
MIX_PER_TOK_TILE = TOK_TILE // MIX_TILE
assert MIX_PER_TOK_TILE * MIX_TILE == TOK_TILE
FF_CHUNK = 512
BF16_SUBLANES = 16
EDGE_ROWS = 8
VMEM_LIMIT = 56 * 1024 * 1024
INPROJ_VMEM_LIMIT = 61 * 1024 * 1024


def _const_spec(shape):
    zeros = (0,) * len(shape)
    return pl.BlockSpec(shape, lambda *_: zeros, pipeline_mode=pl.Buffered(1))


def _sigmoid(x):
    return 0.5 * jnp.tanh(0.5 * x) + 0.5


def _inproj_kernel(n_cast, x_ref, g1_ref, w_ref, gq_ref, gk_ref, cw_ref, *refs):
    cast_in, refs = refs[:n_cast], refs[n_cast:]
    q_ref, k_ref, v_ref, bb_ref, edge_ref, sga_ref, sgb_ref = refs[:7]
    cast_out = refs[7:]
    for src, dst in zip(cast_in, cast_out):
        if len(dst.shape) == 2:
            dst[...] = src[...].astype(BF16)
        else:
            width = dst.shape[2]
            for c in range(dst.shape[0]):
                dst[c] = src[:, c * width:(c + 1) * width].astype(BF16)

    def weights(c0, width):
        return w_ref[:, c0:c0 + width].astype(BF16)

    c0 = 3 * ATTN_WIDTH
    g0 = c0 + 3 * CONV_WIDTH
    w_first = weights(g0, D_MODEL)
    u_parts = []
    for r0 in range(0, TOK_TILE, NORM_ROWS):
        xf = x_ref[r0:r0 + NORM_ROWS, :]
        ms = jnp.mean(xf * xf, axis=-1, keepdims=True)
        u_part = ((xf * lax.rsqrt(ms + EPS)) * g1_ref[...]).astype(BF16)
        sga_ref[r0:r0 + NORM_ROWS, :] = _sigmoid(
            jnp.dot(u_part, w_first, preferred_element_type=F32)).astype(BF16)
        u_parts.append(u_part)
    u = jnp.concatenate(u_parts, axis=0)

    def proj(c0, width):
        return jnp.dot(u, weights(c0, width), preferred_element_type=F32)

    first_head = lax.broadcasted_iota(jnp.int32, (TOK_TILE, LANES), 1) < HEAD_DIM

    def head_norm(t, g):
        outs = []
        for p in range(HEAD_PAIRS):
            tp = t[:, p * LANES:(p + 1) * LANES]
            t2 = tp * tp
            ss_a = jnp.sum(jnp.where(first_head, t2, 0.0), axis=-1, keepdims=True)
            ss_b = jnp.sum(jnp.where(first_head, 0.0, t2), axis=-1, keepdims=True)
            r_a = lax.rsqrt(ss_a * (1.0 / HEAD_DIM) + EPS)
            r_b = lax.rsqrt(ss_b * (1.0 / HEAD_DIM) + EPS)
            outs.append(tp * jnp.where(first_head, r_a, r_b))
        return jnp.concatenate(outs, axis=-1) * g

    sgb_ref[...] = _sigmoid(proj(g0 + D_MODEL, D_MODEL)).astype(BF16)
    q_gain = gq_ref[...] * (LOG2E * HEAD_DIM ** -0.5)
    q_ref[...] = head_norm(proj(0, ATTN_WIDTH), q_gain).astype(BF16)
    k_ref[...] = head_norm(proj(ATTN_WIDTH, ATTN_WIDTH), gk_ref[...]).astype(BF16)
    z = proj(c0 + CONV_WIDTH, CONV_WIDTH) * proj(c0 + 2 * CONV_WIDTH, CONV_WIDTH)
    cb = proj(c0, CONV_WIDTH)
    w_prev, w_mid, w_next = cw_ref[0:1, :], cw_ref[1:2, :], cw_ref[2:3, :]
    row = lax.broadcasted_iota(jnp.int32, (TOK_TILE, CONV_WIDTH), 0)
    z_m1 = jnp.where(row == 0, 0.0, pltpu.roll(z, 1, 0))
    z_p1 = jnp.where(row == TOK_TILE - 1, 0.0, pltpu.roll(z, TOK_TILE - 1, 0))
    bb_ref[...] = (cb * (z_m1 * w_prev + z * w_mid + z_p1 * w_next)).astype(BF16)
    edge_ref[...] = jnp.concatenate(
        [z[0:1], z[TOK_TILE - 1:TOK_TILE], cb[0:1] * w_prev, cb[TOK_TILE - 1:TOK_TILE] * w_next,
         jnp.zeros((EDGE_ROWS - 4, CONV_WIDTH), F32)], axis=0)
    v_ref[...] = proj(2 * ATTN_WIDTH, ATTN_WIDTH).astype(BF16)


def _inproj(x2d, g1, w_in, gq, gk, cw, layer, later_weights):
    m = x2d.shape[0]
    n_steps = m // TOK_TILE
    tok = lambda w: pl.BlockSpec((TOK_TILE, w), lambda i: (i, 0))
    act_specs = ([tok(ATTN_WIDTH)] * 3 + [tok(CONV_WIDTH)]
                 + [pl.BlockSpec((EDGE_ROWS, CONV_WIDTH), lambda i: (i, 0))] + [tok(D_MODEL)] * 2)
    act_shapes = ([jax.ShapeDtypeStruct((m, ATTN_WIDTH), BF16)] * 3
                  + [jax.ShapeDtypeStruct((m, CONV_WIDTH), BF16),
                     jax.ShapeDtypeStruct((n_steps * EDGE_ROWS, CONV_WIDTH), F32)]
                  + [jax.ShapeDtypeStruct((m, D_MODEL), BF16)] * 2)
    cast_in_specs, cast_out_specs, cast_shapes = [], [], []
    for w, chunk in later_weights:
        rows = w.shape[1] // n_steps
        assert rows * n_steps == w.shape[1] and rows % BF16_SUBLANES == 0, w.shape
        cast_in_specs.append(pl.BlockSpec((None, rows, w.shape[2]), lambda i: (layer, i, 0)))
        if chunk is None:
            cast_out_specs.append(pl.BlockSpec((rows, w.shape[2]), lambda i: (i, 0)))
            cast_shapes.append(jax.ShapeDtypeStruct(w.shape[1:], BF16))
        else:
            n_chunks = w.shape[2] // chunk
            cast_out_specs.append(pl.BlockSpec((n_chunks, rows, chunk), lambda i: (0, i, 0)))
            cast_shapes.append(jax.ShapeDtypeStruct((n_chunks, w.shape[1], chunk), BF16))
    return pl.pallas_call(
        functools.partial(_inproj_kernel, len(later_weights)),
        grid=(n_steps,),
        in_specs=[tok(D_MODEL), _const_spec((1, D_MODEL)),
                  pl.BlockSpec((None, D_MODEL, PROJ_WIDTH), lambda i: (layer, 0, 0),
                               pipeline_mode=pl.Buffered(1)),
                  _const_spec((1, ATTN_WIDTH)), _const_spec((1, ATTN_WIDTH)),
                  _const_spec(cw.shape)] + cast_in_specs,
        out_specs=act_specs + cast_out_specs,
        out_shape=act_shapes + cast_shapes,
        compiler_params=pltpu.CompilerParams(
            dimension_semantics=("arbitrary",), vmem_limit_bytes=INPROJ_VMEM_LIMIT),
        name="inproj",
    )(x2d, g1, w_in, gq, gk, cw, *[w for w, _ in later_weights])


def _mixer_kernel(rows_per_batch,
                  q_ref, kw_ref, vw_ref,
                  bb_ref, ep_ref, ec_ref, en_ref, sga_ref, sgb_ref, x_ref,
                  g_ref, wa_ref, wb_ref, wo_ref,
                  o_ref, attn_buf, bb_buf, tp_ref):
    blk = pl.program_id(1)
    n_blk = pl.num_programs(1)

    @pl.when(jnp.logical_and(pl.program_id(0) == 0, blk == 0))
    def _():
        _build_bias_table(g_ref, tp_ref)

    lane = lax.broadcasted_iota(jnp.int32, (GRID_W, LANES), 1)
    first_head = lane < HEAD_DIM
    keep_first = first_head.astype(BF16)
    keep_second = 1 - keep_first
    n_keys = WIN_ROWS * GRID_W
    win_row0 = jnp.clip(blk * ROWS_PER_BLOCK - WIN_ROWS // 2, 0, rows_per_batch - KV_WINDOW_ROWS)

    def window(j):
        r = blk * ROWS_PER_BLOCK + j
        row_start = jnp.clip(r - WIN_ROWS // 2, 0, rows_per_batch - WIN_ROWS)
        ks = pl.multiple_of((row_start - win_row0) * GRID_W, GRID_W)
        return ks, r - row_start

    def scores(j, p):
        ks, d = window(j)
        cols = slice(p * LANES, (p + 1) * LANES)
        rows = slice(j * GRID_W, (j + 1) * GRID_W)
        q = q_ref[rows, cols]
        q2 = jnp.concatenate([q * keep_first, q * keep_second], axis=0)
        s = lax.dot_general(q2, kw_ref[0, pl.ds(ks, n_keys), cols], (((1,), (1,)), ((), ())),
                            preferred_element_type=F32)
        bias = jnp.concatenate(
            [tp_ref[p, WIN_ROWS - 1 - d + i] for i in range(0, WIN_ROWS, 2)], axis=-1)
        return s + bias

    def attend(j, p, s):
        ks, _ = window(j)
        cols = slice(p * LANES, (p + 1) * LANES)
        rows = slice(j * GRID_W, (j + 1) * GRID_W)
        m = jnp.max(s, axis=-1, keepdims=True)
        e = jnp.exp2(s - m)
        l = jnp.sum(e, axis=-1, keepdims=True)
        o = jnp.dot(e.astype(BF16), vw_ref[0, pl.ds(ks, n_keys), cols],
                    preferred_element_type=F32)
        o = o / l
        attn_buf[rows, cols] = jnp.where(first_head, o[:GRID_W], o[GRID_W:]).astype(BF16)

    blk_in_tile = blk % MIX_PER_TOK_TILE
    at_tile_start = jnp.logical_and(blk_in_tile == 0, blk > 0)
    at_tile_end = jnp.logical_and(blk_in_tile == MIX_PER_TOK_TILE - 1, blk < n_blk - 1)
    fix_first = jnp.where(at_tile_start, ec_ref[2:3, :] * ep_ref[1:2, :], 0.0)
    fix_last = jnp.where(at_tile_end, ec_ref[3:4, :] * en_ref[0:1, :], 0.0)
    sub_row = lax.broadcasted_iota(jnp.int32, (BF16_SUBLANES, CONV_WIDTH), 0)
    top = slice(0, BF16_SUBLANES)
    bottom = slice(MIX_TILE - BF16_SUBLANES, MIX_TILE)
    bb_buf[...] = bb_ref[...]
    bb_buf[top, :] = (bb_ref[top, :].astype(F32)
                      + jnp.where(sub_row == 0, fix_first, 0.0)).astype(BF16)
    bb_buf[bottom, :] = (bb_ref[bottom, :].astype(F32)
                         + jnp.where(sub_row == BF16_SUBLANES - 1, fix_last, 0.0)).astype(BF16)

    pending = [scores(0, p) for p in range(HEAD_PAIRS)]
    for j in range(ROWS_PER_BLOCK):
        for p in range(HEAD_PAIRS):
            s = pending[p]
            if j + 1 < ROWS_PER_BLOCK:
                pending[p] = scores(j + 1, p)
            attend(j, p, s)

    ya = jnp.dot(attn_buf[...], wa_ref[...], preferred_element_type=F32)
    yb = jnp.dot(bb_buf[...], wb_ref[...], preferred_element_type=F32)
    merged = sga_ref[...] * ya.astype(BF16) + sgb_ref[...] * yb.astype(BF16)
    o_ref[...] = x_ref[...] + jnp.dot(merged, wo_ref[...], preferred_element_type=F32)


def _kv_window_start(blk, rows_per_batch):
    row0 = jnp.clip(blk * ROWS_PER_BLOCK - WIN_ROWS // 2, 0, rows_per_batch - KV_WINDOW_ROWS)
    return row0 * GRID_W


def _mixer(batch, seq, q, k, v, bb, edges, sga, sgb, x, bias_rows, wa, wb, wo):
    rows = seq // GRID_W
    n_blk = rows // ROWS_PER_BLOCK
    tiles_per_seq = seq // TOK_TILE
    n_tiles = batch * tiles_per_seq

    def r3(a):
        return a.reshape(batch, seq, a.shape[-1])

    cur = lambda w: pl.BlockSpec((None, MIX_TILE, w), lambda b, i: (b, i, 0))
    kv_window = pl.BlockSpec(
        (pl.Element(1), pl.Element(KV_WINDOW_ROWS * GRID_W), pl.Element(ATTN_WIDTH)),
        lambda b, i: (b, _kv_window_start(i, rows), 0))
    def edge_spec(offset):
        def index(b, i):
            tile = b * tiles_per_seq + i // MIX_PER_TOK_TILE + offset
            return (jnp.clip(tile, 0, n_tiles - 1), 0)
        return pl.BlockSpec((EDGE_ROWS, CONV_WIDTH), index)

    aw = ATTN_WIDTH
    in_specs = [cur(aw), kv_window, kv_window,
                cur(CONV_WIDTH), edge_spec(-1), edge_spec(0), edge_spec(1),
                cur(D_MODEL), cur(D_MODEL), cur(D_MODEL),
                _const_spec(bias_rows.shape), _const_spec(wa.shape),
                _const_spec(wb.shape), _const_spec(wo.shape)]
    out = pl.pallas_call(
        functools.partial(_mixer_kernel, rows),
        grid=(batch, n_blk),
        in_specs=in_specs,
        out_specs=cur(D_MODEL),
        out_shape=jax.ShapeDtypeStruct((batch, seq, D_MODEL), F32),
        scratch_shapes=[pltpu.VMEM((MIX_TILE, ATTN_WIDTH), BF16),
                        pltpu.VMEM((MIX_TILE, CONV_WIDTH), BF16),
                        pltpu.VMEM((HEAD_PAIRS, N_REL_ROWS - 1, 2 * GRID_W, LANES), F32)],
        compiler_params=pltpu.CompilerParams(
            dimension_semantics=("arbitrary", "arbitrary"), vmem_limit_bytes=VMEM_LIMIT),
        name="mixer",
    )(r3(q), r3(k), r3(v), r3(bb), edges, edges, edges, r3(sga), r3(sgb), x,
      bias_rows, wa, wb, wo)
    return out


def _padded_bias_rows(rpb):
    pad = GRID_W - WIN_COLS
    return jnp.pad(rpb.astype(F32), ((0, 0), (0, 0), (pad, LANES - N_REL_COLS - pad)))


def _build_bias_table(g_ref, tbl_ref):
    c = lax.broadcasted_iota(jnp.int32, (GRID_W, LANES), 0)
    x = lax.broadcasted_iota(jnp.int32, (GRID_W, LANES), 1)
    kc = x % GRID_W
    win_start = jnp.clip(c - WIN_COLS // 2, 0, GRID_W - WIN_COLS)
    in_window = (kc >= win_start) & (kc < win_start + WIN_COLS)
    left = x < GRID_W
    for h in range(N_HEADS):
        rows = slice((h % 2) * GRID_W, (h % 2 + 1) * GRID_W)
        for rr in range(N_REL_ROWS - 1):
            g0 = jnp.broadcast_to(g_ref[h, rr:rr + 1, :] * LOG2E, (GRID_W, LANES))
            g1 = jnp.broadcast_to(g_ref[h, rr + 1:rr + 2, :] * LOG2E, (GRID_W, LANES))
            a = pltpu.roll(g0, LANES - (GRID_W - 1), 1, stride=1, stride_axis=0)
            b = pltpu.roll(g1, 1, 1, stride=1, stride_axis=0)
            tbl_ref[h // 2, rr, rows, :] = jnp.where(in_window, jnp.where(left, a, b), NEG_INF)


def _mlp_kernel(x_ref, g_ref, w1_hbm, w2_hbm, o_ref, h_buf, w1_buf, w2_buf, sems):
    n_chunks = D_FF // FF_CHUNK

    def w1_copy(c):
        return pltpu.make_async_copy(w1_hbm.at[c], w1_buf.at[c], sems.at[c])

    def w2_copy():
        return pltpu.make_async_copy(w2_hbm, w2_buf, sems.at[n_chunks])

    def act(h):
        return jnp.square(jnp.maximum(h, 0.0)).astype(BF16)

    def body(first_step):
        if first_step:
            for c in range(n_chunks):
                w1_copy(c).start()
            w2_copy().start()
        first = slice(0, FF_CHUNK)
        u_parts = []
        for r0 in range(0, TOK_TILE, NORM_ROWS):
            xf = x_ref[r0:r0 + NORM_ROWS, :]
            ms = jnp.mean(xf * xf, axis=-1, keepdims=True)
            u_part = ((xf * lax.rsqrt(ms + EPS)) * g_ref[...]).astype(BF16)
            if first_step and r0 == 0:
                w1_copy(0).wait()
            h_buf[r0:r0 + NORM_ROWS, first] = act(
                jnp.dot(u_part, w1_buf[0], preferred_element_type=F32))
            u_parts.append(u_part)
        u = jnp.concatenate(u_parts, axis=0)
        for c in range(1, n_chunks):
            if first_step:
                w1_copy(c).wait()
            cols = slice(c * FF_CHUNK, (c + 1) * FF_CHUNK)
            h_buf[:, cols] = act(jnp.dot(u, w1_buf[c], preferred_element_type=F32))
        if first_step:
            w2_copy().wait()
        o_ref[...] = x_ref[...] + jnp.dot(h_buf[...], w2_buf[...], preferred_element_type=F32)

    step = pl.program_id(0)
    pl.when(step == 0)(functools.partial(body, True))
    pl.when(step > 0)(functools.partial(body, False))


def _mlp(x2d, g2, w1, w2):
    m = x2d.shape[0]
    tok = pl.BlockSpec((TOK_TILE, D_MODEL), lambda i: (i, 0))
    whole = pl.BlockSpec(memory_space=pl.ANY)
    return pl.pallas_call(
        _mlp_kernel,
        grid=(m // TOK_TILE,),
        in_specs=[tok, _const_spec((1, D_MODEL)), whole, whole],
        out_specs=tok,
        out_shape=jax.ShapeDtypeStruct((m, D_MODEL), F32),
        scratch_shapes=[pltpu.VMEM((TOK_TILE, D_FF), BF16),
                        pltpu.VMEM(w1.shape, BF16),
                        pltpu.VMEM(w2.shape, BF16),
                        pltpu.SemaphoreType.DMA((D_FF // FF_CHUNK + 1,))],
        compiler_params=pltpu.CompilerParams(
            dimension_semantics=("arbitrary",), vmem_limit_bytes=VMEM_LIMIT),
        name="mlp",
    )(x2d, g2, w1, w2)


def kernel(x, norm1_g, w_in, q_norm_g, k_norm_g, rpb, conv_w, w_attn_branch, w_conv_branch,
           w_o, norm2_g, w_mlp_in, w_mlp_out):
    batch, seq, _ = x.shape
    depth = w_in.shape[0]
    for l in range(depth):
        x2d = x.reshape(batch * seq, D_MODEL)
        gq = jnp.tile(q_norm_g[l], N_HEADS)[None, :]
        gk = jnp.tile(k_norm_g[l], N_HEADS)[None, :]
        assert seq % TOK_TILE == 0
        q, k, v, bb, edges, sga, sgb, wa, wb, wo, w1, w2 = _inproj(
            x2d, norm1_g[l][None, :], w_in, gq, gk, conv_w[l], l,
            ((w_attn_branch, None), (w_conv_branch, None), (w_o, None),
             (w_mlp_in, FF_CHUNK), (w_mlp_out, None)))
        x = _mixer(batch, seq, q, k, v, bb, edges, sga, sgb, x,
                   _padded_bias_rows(rpb[l]), wa, wb, wo)
        x = _mlp(x.reshape(batch * seq, D_MODEL), norm2_g[l][None, :], w1, w2
                 ).reshape(batch, seq, D_MODEL)
    return x
```

```python
import functools

import numpy as np
import jax
import jax.numpy as jnp
from jax import lax
from jax.experimental import pallas as pl
from jax.experimental.pallas import tpu as pltpu

F32 = jnp.float32
BF16 = jnp.bfloat16

D_MODEL = 1024
GRID_W = 64
N_HEADS = 8
HEAD_DIM = 64
ATTN_WIDTH = N_HEADS * HEAD_DIM
CONV_WIDTH = D_MODEL // 2
WIN_ROWS = 8
WIN_COLS = 16
D_FF = 4 * D_MODEL
EPS = 1e-6
NEG_INF = -1e30
LOG2E = 1.4426950408889634
PROJ_WIDTH = 3 * ATTN_WIDTH + 3 * CONV_WIDTH + 2 * D_MODEL

LANES = 128
HEAD_PAIRS = ATTN_WIDTH // LANES
N_REL_ROWS = 2 * WIN_ROWS - 1
N_REL_COLS = 2 * WIN_COLS - 1

TOK_TILE = 1024
NORM_ROWS = 256
ROWS_PER_BLOCK = 16
MIX_TILE = ROWS_PER_BLOCK * GRID_W
KV_WINDOW_ROWS = ROWS_PER_BLOCK + WIN_ROWS
MIX_PER_TOK_TILE = TOK_TILE // MIX_TILE
assert MIX_PER_TOK_TILE * MIX_TILE == TOK_TILE
FF_CHUNK = 512
MLP_TILE = 2 * TOK_TILE
BF16_SUBLANES = 16
EDGE_ROWS = 8
VMEM_LIMIT = 56 * 1024 * 1024
INPROJ_VMEM_LIMIT = 61 * 1024 * 1024


def _const_spec(shape):
    zeros = (0,) * len(shape)
    return pl.BlockSpec(shape, lambda *_: zeros, pipeline_mode=pl.Buffered(1))


def _sigmoid(x):
    return 0.5 * jnp.tanh(0.5 * x) + 0.5


def _inproj_kernel(n_cast, x_ref, g1_ref, w_ref, gq_ref, gk_ref, cw_ref, *refs):
    cast_in, refs = refs[:n_cast], refs[n_cast:]
    q_ref, k_ref, v_ref, bb_ref, edge_ref, sga_ref, sgb_ref = refs[:7]
    cast_out = refs[7:]
    for src, dst in zip(cast_in, cast_out):
        dst[...] = src[...].astype(BF16)

    def weights(c0, width):
        return w_ref[:, c0:c0 + width].astype(BF16)

    c0 = 3 * ATTN_WIDTH
    g0 = c0 + 3 * CONV_WIDTH
    w_first = weights(g0, D_MODEL)
    u_parts = []
    for r0 in range(0, TOK_TILE, NORM_ROWS):
        xf = x_ref[r0:r0 + NORM_ROWS, :]
        ms = jnp.mean(xf * xf, axis=-1, keepdims=True)
        u_part = ((xf * lax.rsqrt(ms + EPS)) * g1_ref[...]).astype(BF16)
        sga_ref[r0:r0 + NORM_ROWS, :] = _sigmoid(
            jnp.dot(u_part, w_first, preferred_element_type=F32)).astype(BF16)
        u_parts.append(u_part)
    u = jnp.concatenate(u_parts, axis=0)

    def proj(c0, width):
        return jnp.dot(u, weights(c0, width), preferred_element_type=F32)

    first_head = lax.broadcasted_iota(jnp.int32, (TOK_TILE, LANES), 1) < HEAD_DIM

    def head_norm(t, g):
        outs = []
        for p in range(HEAD_PAIRS):
            tp = t[:, p * LANES:(p + 1) * LANES]
            t2 = tp * tp
            ss_a = jnp.sum(jnp.where(first_head, t2, 0.0), axis=-1, keepdims=True)
            ss_b = jnp.sum(jnp.where(first_head, 0.0, t2), axis=-1, keepdims=True)
            r_a = lax.rsqrt(ss_a * (1.0 / HEAD_DIM) + EPS)
            r_b = lax.rsqrt(ss_b * (1.0 / HEAD_DIM) + EPS)
            outs.append(tp * jnp.where(first_head, r_a, r_b))
        return jnp.concatenate(outs, axis=-1) * g

    sgb_ref[...] = _sigmoid(proj(g0 + D_MODEL, D_MODEL)).astype(BF16)
    q_gain = gq_ref[...] * (LOG2E * HEAD_DIM ** -0.5)
    q_ref[...] = head_norm(proj(0, ATTN_WIDTH), q_gain).astype(BF16)
    k_ref[...] = head_norm(proj(ATTN_WIDTH, ATTN_WIDTH), gk_ref[...]).astype(BF16)
    z = proj(c0 + CONV_WIDTH, CONV_WIDTH) * proj(c0 + 2 * CONV_WIDTH, CONV_WIDTH)
    cb = proj(c0, CONV_WIDTH)
    w_prev, w_mid, w_next = cw_ref[0:1, :], cw_ref[1:2, :], cw_ref[2:3, :]
    row = lax.broadcasted_iota(jnp.int32, (TOK_TILE, CONV_WIDTH), 0)
    z_m1 = jnp.where(row == 0, 0.0, pltpu.roll(z, 1, 0))
    z_p1 = jnp.where(row == TOK_TILE - 1, 0.0, pltpu.roll(z, TOK_TILE - 1, 0))
    bb_ref[...] = (cb * (z_m1 * w_prev + z * w_mid + z_p1 * w_next)).astype(BF16)
    edge_ref[...] = jnp.concatenate(
        [z[0:1], z[TOK_TILE - 1:TOK_TILE], cb[0:1] * w_prev, cb[TOK_TILE - 1:TOK_TILE] * w_next,
         jnp.zeros((EDGE_ROWS - 4, CONV_WIDTH), F32)], axis=0)
    v_ref[...] = proj(2 * ATTN_WIDTH, ATTN_WIDTH).astype(BF16)


def _inproj(x2d, g1, w_in, gq, gk, cw, layer, later_weights):
    m = x2d.shape[0]
    n_steps = m // TOK_TILE
    tok = lambda w: pl.BlockSpec((TOK_TILE, w), lambda i: (i, 0))
    act_specs = ([tok(ATTN_WIDTH)] * 3 + [tok(CONV_WIDTH)]
                 + [pl.BlockSpec((EDGE_ROWS, CONV_WIDTH), lambda i: (i, 0))] + [tok(D_MODEL)] * 2)
    act_shapes = ([jax.ShapeDtypeStruct((m, ATTN_WIDTH), BF16)] * 3
                  + [jax.ShapeDtypeStruct((m, CONV_WIDTH), BF16),
                     jax.ShapeDtypeStruct((n_steps * EDGE_ROWS, CONV_WIDTH), F32)]
                  + [jax.ShapeDtypeStruct((m, D_MODEL), BF16)] * 2)
    cast_in_specs, cast_out_specs = [], []
    for w in later_weights:
        rows = w.shape[1] // n_steps
        assert rows * n_steps == w.shape[1] and rows % BF16_SUBLANES == 0, w.shape
        cast_in_specs.append(pl.BlockSpec((None, rows, w.shape[2]), lambda i: (layer, i, 0)))
        cast_out_specs.append(pl.BlockSpec((rows, w.shape[2]), lambda i: (i, 0)))
    return pl.pallas_call(
        functools.partial(_inproj_kernel, len(later_weights)),
        grid=(n_steps,),
        in_specs=[tok(D_MODEL), _const_spec((1, D_MODEL)),
                  pl.BlockSpec((None, D_MODEL, PROJ_WIDTH), lambda i: (layer, 0, 0),
                               pipeline_mode=pl.Buffered(1)),
                  _const_spec((1, ATTN_WIDTH)), _const_spec((1, ATTN_WIDTH)),
                  _const_spec(cw.shape)] + cast_in_specs,
        out_specs=act_specs + cast_out_specs,
        out_shape=(act_shapes
                   + [jax.ShapeDtypeStruct(w.shape[1:], BF16) for w in later_weights]),
        compiler_params=pltpu.CompilerParams(
            dimension_semantics=("arbitrary",), vmem_limit_bytes=INPROJ_VMEM_LIMIT),
        name="inproj",
    )(x2d, g1, w_in, gq, gk, cw, *later_weights)


def _mixer_kernel(rows_per_batch,
                  q_ref, kw_ref, vw_ref,
                  bb_ref, ep_ref, ec_ref, en_ref, sga_ref, sgb_ref, x_ref,
                  g_ref, wa_ref, wb_ref, wo_ref,
                  o_ref, attn_buf, bb_buf, tp_ref):
    blk = pl.program_id(1)
    n_blk = pl.num_programs(1)

    @pl.when(jnp.logical_and(pl.program_id(0) == 0, blk == 0))
    def _():
        _build_bias_table(g_ref, tp_ref)

    lane = lax.broadcasted_iota(jnp.int32, (GRID_W, LANES), 1)
    first_head = lane < HEAD_DIM
    keep_first = first_head.astype(BF16)
    keep_second = 1 - keep_first
    n_keys = WIN_ROWS * GRID_W
    win_row0 = jnp.clip(blk * ROWS_PER_BLOCK - WIN_ROWS // 2, 0, rows_per_batch - KV_WINDOW_ROWS)

    def window(j):
        r = blk * ROWS_PER_BLOCK + j
        row_start = jnp.clip(r - WIN_ROWS // 2, 0, rows_per_batch - WIN_ROWS)
        ks = pl.multiple_of((row_start - win_row0) * GRID_W, GRID_W)
        return ks, r - row_start

    def scores(j, p):
        ks, d = window(j)
        cols = slice(p * LANES, (p + 1) * LANES)
        rows = slice(j * GRID_W, (j + 1) * GRID_W)
        q = q_ref[rows, cols]
        q2 = jnp.concatenate([q * keep_first, q * keep_second], axis=0)
        s = lax.dot_general(q2, kw_ref[0, pl.ds(ks, n_keys), cols], (((1,), (1,)), ((), ())),
                            preferred_element_type=F32)
        bias = jnp.concatenate(
            [tp_ref[p, WIN_ROWS - 1 - d + i] for i in range(0, WIN_ROWS, 2)], axis=-1)
        return s + bias

    def attend(j, p, s):
        ks, _ = window(j)
        cols = slice(p * LANES, (p + 1) * LANES)
        rows = slice(j * GRID_W, (j + 1) * GRID_W)
        m = jnp.max(s, axis=-1, keepdims=True)
        e = jnp.exp2(s - m)
        l = jnp.sum(e, axis=-1, keepdims=True)
        o = jnp.dot(e.astype(BF16), vw_ref[0, pl.ds(ks, n_keys), cols],
                    preferred_element_type=F32)
        o = o / l
        attn_buf[rows, cols] = jnp.where(first_head, o[:GRID_W], o[GRID_W:]).astype(BF16)

    blk_in_tile = blk % MIX_PER_TOK_TILE
    at_tile_start = jnp.logical_and(blk_in_tile == 0, blk > 0)
    at_tile_end = jnp.logical_and(blk_in_tile == MIX_PER_TOK_TILE - 1, blk < n_blk - 1)
    fix_first = jnp.where(at_tile_start, ec_ref[2:3, :] * ep_ref[1:2, :], 0.0)
    fix_last = jnp.where(at_tile_end, ec_ref[3:4, :] * en_ref[0:1, :], 0.0)
    sub_row = lax.broadcasted_iota(jnp.int32, (BF16_SUBLANES, CONV_WIDTH), 0)
    top = slice(0, BF16_SUBLANES)
    bottom = slice(MIX_TILE - BF16_SUBLANES, MIX_TILE)
    bb_buf[...] = bb_ref[...]
    bb_buf[top, :] = (bb_ref[top, :].astype(F32)
                      + jnp.where(sub_row == 0, fix_first, 0.0)).astype(BF16)
    bb_buf[bottom, :] = (bb_ref[bottom, :].astype(F32)
                         + jnp.where(sub_row == BF16_SUBLANES - 1, fix_last, 0.0)).astype(BF16)

    pending = [scores(0, p) for p in range(HEAD_PAIRS)]
    for j in range(ROWS_PER_BLOCK):
        for p in range(HEAD_PAIRS):
            s = pending[p]
            if j + 1 < ROWS_PER_BLOCK:
                pending[p] = scores(j + 1, p)
            attend(j, p, s)

    ya = jnp.dot(attn_buf[...], wa_ref[...], preferred_element_type=F32)
    yb = jnp.dot(bb_buf[...], wb_ref[...], preferred_element_type=F32)
    merged = sga_ref[...] * ya.astype(BF16) + sgb_ref[...] * yb.astype(BF16)
    o_ref[...] = x_ref[...] + jnp.dot(merged, wo_ref[...], preferred_element_type=F32)


def _kv_window_start(blk, rows_per_batch):
    row0 = jnp.clip(blk * ROWS_PER_BLOCK - WIN_ROWS // 2, 0, rows_per_batch - KV_WINDOW_ROWS)
    return row0 * GRID_W


def _mixer(batch, seq, q, k, v, bb, edges, sga, sgb, x, bias_rows, wa, wb, wo):
    rows = seq // GRID_W
    n_blk = rows // ROWS_PER_BLOCK
    tiles_per_seq = seq // TOK_TILE
    n_tiles = batch * tiles_per_seq

    def r3(a):
        return a.reshape(batch, seq, a.shape[-1])

    cur = lambda w: pl.BlockSpec((None, MIX_TILE, w), lambda b, i: (b, i, 0))
    kv_window = pl.BlockSpec(
        (pl.Element(1), pl.Element(KV_WINDOW_ROWS * GRID_W), pl.Element(ATTN_WIDTH)),
        lambda b, i: (b, _kv_window_start(i, rows), 0))
    def edge_spec(offset):
        def index(b, i):
            tile = b * tiles_per_seq + i // MIX_PER_TOK_TILE + offset
            return (jnp.clip(tile, 0, n_tiles - 1), 0)
        return pl.BlockSpec((EDGE_ROWS, CONV_WIDTH), index)

    aw = ATTN_WIDTH
    in_specs = [cur(aw), kv_window, kv_window,
                cur(CONV_WIDTH), edge_spec(-1), edge_spec(0), edge_spec(1),
                cur(D_MODEL), cur(D_MODEL), cur(D_MODEL),
                _const_spec(bias_rows.shape), _const_spec(wa.shape),
                _const_spec(wb.shape), _const_spec(wo.shape)]
    out = pl.pallas_call(
        functools.partial(_mixer_kernel, rows),
        grid=(batch, n_blk),
        in_specs=in_specs,
        out_specs=cur(D_MODEL),
        out_shape=jax.ShapeDtypeStruct((batch, seq, D_MODEL), F32),
        scratch_shapes=[pltpu.VMEM((MIX_TILE, ATTN_WIDTH), BF16),
                        pltpu.VMEM((MIX_TILE, CONV_WIDTH), BF16),
                        pltpu.VMEM((HEAD_PAIRS, N_REL_ROWS - 1, 2 * GRID_W, LANES), F32)],
        compiler_params=pltpu.CompilerParams(
            dimension_semantics=("arbitrary", "arbitrary"), vmem_limit_bytes=VMEM_LIMIT),
        name="mixer",
    )(r3(q), r3(k), r3(v), r3(bb), edges, edges, edges, r3(sga), r3(sgb), x,
      bias_rows, wa, wb, wo)
    return out


def _padded_bias_rows(rpb):
    pad = GRID_W - WIN_COLS
    return jnp.pad(rpb.astype(F32), ((0, 0), (0, 0), (pad, LANES - N_REL_COLS - pad)))


def _build_bias_table(g_ref, tbl_ref):
    c = lax.broadcasted_iota(jnp.int32, (GRID_W, LANES), 0)
    x = lax.broadcasted_iota(jnp.int32, (GRID_W, LANES), 1)
    kc = x % GRID_W
    win_start = jnp.clip(c - WIN_COLS // 2, 0, GRID_W - WIN_COLS)
    in_window = (kc >= win_start) & (kc < win_start + WIN_COLS)
    left = x < GRID_W
    for h in range(N_HEADS):
        rows = slice((h % 2) * GRID_W, (h % 2 + 1) * GRID_W)
        for rr in range(N_REL_ROWS - 1):
            g0 = jnp.broadcast_to(g_ref[h, rr:rr + 1, :] * LOG2E, (GRID_W, LANES))
            g1 = jnp.broadcast_to(g_ref[h, rr + 1:rr + 2, :] * LOG2E, (GRID_W, LANES))
            a = pltpu.roll(g0, LANES - (GRID_W - 1), 1, stride=1, stride_axis=0)
            b = pltpu.roll(g1, 1, 1, stride=1, stride_axis=0)
            tbl_ref[h // 2, rr, rows, :] = jnp.where(in_window, jnp.where(left, a, b), NEG_INF)


def _mlp_kernel(x_ref, g_ref, w1_ref, w2_ref, o_ref, h_buf):
    def act(h):
        return jnp.square(jnp.maximum(h, 0.0)).astype(BF16)

    first = slice(0, FF_CHUNK)
    for t0 in range(0, MLP_TILE, TOK_TILE):
        u_parts = []
        for r0 in range(0, TOK_TILE, NORM_ROWS):
            xf = x_ref[t0 + r0:t0 + r0 + NORM_ROWS, :]
            ms = jnp.mean(xf * xf, axis=-1, keepdims=True)
            u_part = ((xf * lax.rsqrt(ms + EPS)) * g_ref[...]).astype(BF16)
            h_buf[r0:r0 + NORM_ROWS, first] = act(
                jnp.dot(u_part, w1_ref[:, first], preferred_element_type=F32))
            u_parts.append(u_part)
        u = jnp.concatenate(u_parts, axis=0)
        for c in range(1, D_FF // FF_CHUNK):
            cols = slice(c * FF_CHUNK, (c + 1) * FF_CHUNK)
            h_buf[:, cols] = act(jnp.dot(u, w1_ref[:, cols], preferred_element_type=F32))
        o_ref[t0:t0 + TOK_TILE, :] = x_ref[t0:t0 + TOK_TILE, :] + jnp.dot(
            h_buf[...], w2_ref[...], preferred_element_type=F32)


def _mlp(x2d, g2, w1, w2):
    m = x2d.shape[0]
    tok = pl.BlockSpec((MLP_TILE, D_MODEL), lambda i: (i, 0))
    return pl.pallas_call(
        _mlp_kernel,
        grid=(m // MLP_TILE,),
        in_specs=[tok, _const_spec((1, D_MODEL)), _const_spec(w1.shape), _const_spec(w2.shape)],
        out_specs=tok,
        out_shape=jax.ShapeDtypeStruct((m, D_MODEL), F32),
        scratch_shapes=[pltpu.VMEM((TOK_TILE, D_FF), BF16)],
        compiler_params=pltpu.CompilerParams(
            dimension_semantics=("arbitrary",), vmem_limit_bytes=INPROJ_VMEM_LIMIT),
        name="mlp",
    )(x2d, g2, w1, w2)


def kernel(x, norm1_g, w_in, q_norm_g, k_norm_g, rpb, conv_w, w_attn_branch, w_conv_branch,
           w_o, norm2_g, w_mlp_in, w_mlp_out):
    batch, seq, _ = x.shape
    depth = w_in.shape[0]
    for l in range(depth):
        x2d = x.reshape(batch * seq, D_MODEL)
        gq = jnp.tile(q_norm_g[l], N_HEADS)[None, :]
        gk = jnp.tile(k_norm_g[l], N_HEADS)[None, :]
        assert seq % TOK_TILE == 0
        q, k, v, bb, edges, sga, sgb, wa, wb, wo, w1, w2 = _inproj(
            x2d, norm1_g[l][None, :], w_in, gq, gk, conv_w[l], l,
            (w_attn_branch, w_conv_branch, w_o, w_mlp_in, w_mlp_out))
        x = _mixer(batch, seq, q, k, v, bb, edges, sga, sgb, x,
                   _padded_bias_rows(rpb[l]), wa, wb, wo)
        x = _mlp(x.reshape(batch * seq, D_MODEL), norm2_g[l][None, :], w1, w2
                 ).reshape(batch, seq, D_MODEL)
    return x
```

```python
import functools

import jax
import jax.numpy as jnp
from jax import lax
from jax.experimental import pallas as pl
from jax.experimental.pallas import tpu as pltpu

F32 = jnp.float32
BF16 = jnp.bfloat16

D_MODEL = 1024
GRID_W = 64
N_HEADS = 8
HEAD_DIM = 64
ATTN_WIDTH = N_HEADS * HEAD_DIM
CONV_WIDTH = D_MODEL // 2
WIN_ROWS = 8
WIN_COLS = 16
D_FF = 4 * D_MODEL
EPS = 1e-6
NEG_INF = -1e30
LOG2E = 1.4426950408889634
PROJ_WIDTH = 3 * ATTN_WIDTH + 3 * CONV_WIDTH + 2 * D_MODEL

LANES = 128
HEAD_PAIRS = ATTN_WIDTH // LANES
N_REL_ROWS = 2 * WIN_ROWS - 1
N_REL_COLS = 2 * WIN_COLS - 1

TOK_TILE = 1024
NORM_ROWS = 256
ROWS_PER_BLOCK = 16
MIX_TILE = ROWS_PER_BLOCK * GRID_W
KV_WINDOW_ROWS = ROWS_PER_BLOCK + WIN_ROWS
MIX_PER_TOK_TILE = TOK_TILE // MIX_TILE
assert MIX_PER_TOK_TILE * MIX_TILE == TOK_TILE
FF_CHUNK = 512
BF16_SUBLANES = 16
EDGE_ROWS = 8
MIB = 1024 * 1024
V7X_VMEM_BYTES = 64 * MIB
VMEM_LIMIT = 56 * MIB
INPROJ_VMEM_LIMIT = 61 * MIB
assert VMEM_LIMIT < INPROJ_VMEM_LIMIT < V7X_VMEM_BYTES


def _const_spec(shape):
    zeros = (0,) * len(shape)
    return pl.BlockSpec(shape, lambda *_: zeros, pipeline_mode=pl.Buffered(1))


def _layer_spec(stacked, layer):
    return pl.BlockSpec((None,) + stacked.shape[1:], lambda *_: (layer, 0, 0),
                        pipeline_mode=pl.Buffered(1))


def _sigmoid(x):
    return 0.5 * jnp.tanh(0.5 * x) + 0.5


def _inproj_kernel(n_cast, x_ref, g1_ref, w_ref, gq_ref, gk_ref, cw_ref, *refs):
    cast_in, refs = refs[:n_cast], refs[n_cast:]
    q_ref, k_ref, v_ref, bb_ref, edge_ref, sga_ref, sgb_ref = refs[:7]
    cast_out = refs[7:]
    for src, dst in zip(cast_in, cast_out):
        dst[...] = src[...].astype(BF16)

    def weights(c0, width):
        return w_ref[:, c0:c0 + width].astype(BF16)

    c0 = 3 * ATTN_WIDTH
    g0 = c0 + 3 * CONV_WIDTH
    w_first = weights(g0, D_MODEL)
    u_parts = []
    for r0 in range(0, TOK_TILE, NORM_ROWS):
        xf = x_ref[r0:r0 + NORM_ROWS, :]
        ms = jnp.mean(xf * xf, axis=-1, keepdims=True)
        u_part = ((xf * lax.rsqrt(ms + EPS)) * g1_ref[...]).astype(BF16)
        sga_ref[r0:r0 + NORM_ROWS, :] = _sigmoid(
            jnp.dot(u_part, w_first, preferred_element_type=F32)).astype(BF16)
        u_parts.append(u_part)
    u = jnp.concatenate(u_parts, axis=0)

    def proj(c0, width):
        return jnp.dot(u, weights(c0, width), preferred_element_type=F32)

    first_head = lax.broadcasted_iota(jnp.int32, (TOK_TILE, LANES), 1) < HEAD_DIM

    def head_norm(t, g):
        outs = []
        for p in range(HEAD_PAIRS):
            tp = t[:, p * LANES:(p + 1) * LANES]
            t2 = tp * tp
            ss_a = jnp.sum(jnp.where(first_head, t2, 0.0), axis=-1, keepdims=True)
            ss_b = jnp.sum(jnp.where(first_head, 0.0, t2), axis=-1, keepdims=True)
            r_a = lax.rsqrt(ss_a * (1.0 / HEAD_DIM) + EPS)
            r_b = lax.rsqrt(ss_b * (1.0 / HEAD_DIM) + EPS)
            outs.append(tp * jnp.where(first_head, r_a, r_b))
        return jnp.concatenate(outs, axis=-1) * g

    sgb_ref[...] = _sigmoid(proj(g0 + D_MODEL, D_MODEL)).astype(BF16)
    gq = jnp.concatenate([gq_ref[...]] * N_HEADS, axis=1)
    gk = jnp.concatenate([gk_ref[...]] * N_HEADS, axis=1)
    q_ref[...] = head_norm(proj(0, ATTN_WIDTH), gq * (LOG2E * HEAD_DIM ** -0.5)).astype(BF16)
    k_ref[...] = head_norm(proj(ATTN_WIDTH, ATTN_WIDTH), gk).astype(BF16)
    z = proj(c0 + CONV_WIDTH, CONV_WIDTH) * proj(c0 + 2 * CONV_WIDTH, CONV_WIDTH)
    cb = proj(c0, CONV_WIDTH)
    w_prev, w_mid, w_next = cw_ref[0:1, :], cw_ref[1:2, :], cw_ref[2:3, :]
    row = lax.broadcasted_iota(jnp.int32, (TOK_TILE, CONV_WIDTH), 0)
    z_m1 = jnp.where(row == 0, 0.0, pltpu.roll(z, 1, 0))
    z_p1 = jnp.where(row == TOK_TILE - 1, 0.0, pltpu.roll(z, TOK_TILE - 1, 0))
    bb_ref[...] = (cb * (z_m1 * w_prev + z * w_mid + z_p1 * w_next)).astype(BF16)
    edge_ref[...] = jnp.concatenate(
        [z[0:1], z[TOK_TILE - 1:TOK_TILE], cb[0:1] * w_prev, cb[TOK_TILE - 1:TOK_TILE] * w_next,
         jnp.zeros((EDGE_ROWS - 4, CONV_WIDTH), F32)], axis=0)
    v_ref[...] = proj(2 * ATTN_WIDTH, ATTN_WIDTH).astype(BF16)


def _inproj(x2d, g1, w_in, gq, gk, cw, layer, later_weights):
    m = x2d.shape[0]
    n_steps = m // TOK_TILE
    tok = lambda w: pl.BlockSpec((TOK_TILE, w), lambda i: (i, 0))
    act_specs = ([tok(ATTN_WIDTH)] * 3 + [tok(CONV_WIDTH)]
                 + [pl.BlockSpec((EDGE_ROWS, CONV_WIDTH), lambda i: (i, 0))] + [tok(D_MODEL)] * 2)
    act_shapes = ([jax.ShapeDtypeStruct((m, ATTN_WIDTH), BF16)] * 3
                  + [jax.ShapeDtypeStruct((m, CONV_WIDTH), BF16),
                     jax.ShapeDtypeStruct((n_steps * EDGE_ROWS, CONV_WIDTH), F32)]
                  + [jax.ShapeDtypeStruct((m, D_MODEL), BF16)] * 2)
    cast_in_specs, cast_out_specs = [], []
    for w in later_weights:
        rows = w.shape[1] // n_steps
        assert rows * n_steps == w.shape[1] and rows % BF16_SUBLANES == 0, w.shape
        cast_in_specs.append(pl.BlockSpec((None, rows, w.shape[2]), lambda i: (layer, i, 0)))
        cast_out_specs.append(pl.BlockSpec((rows, w.shape[2]), lambda i: (i, 0)))
    return pl.pallas_call(
        functools.partial(_inproj_kernel, len(later_weights)),
        grid=(n_steps,),
        in_specs=[tok(D_MODEL), _layer_spec(g1, layer), _layer_spec(w_in, layer),
                  _layer_spec(gq, layer), _layer_spec(gk, layer), _layer_spec(cw, layer)]
        + cast_in_specs,
        out_specs=act_specs + cast_out_specs,
        out_shape=(act_shapes
                   + [jax.ShapeDtypeStruct(w.shape[1:], BF16) for w in later_weights]),
        compiler_params=pltpu.CompilerParams(
            dimension_semantics=("arbitrary",), vmem_limit_bytes=INPROJ_VMEM_LIMIT),
        name="inproj",
    )(x2d, g1, w_in, gq, gk, cw, *later_weights)


def _mixer_kernel(rows_per_batch,
                  q_ref, kw_ref, vw_ref,
                  bb_ref, ep_ref, ec_ref, en_ref, sga_ref, sgb_ref, x_ref,
                  rpb_ref, wa_ref, wb_ref, wo_ref,
                  o_ref, attn_buf, bb_buf, tp_ref):
    blk = pl.program_id(1)
    n_blk = pl.num_programs(1)

    @pl.when(jnp.logical_and(pl.program_id(0) == 0, blk == 0))
    def _():
        _build_bias_table(rpb_ref, tp_ref)

    lane = lax.broadcasted_iota(jnp.int32, (GRID_W, LANES), 1)
    first_head = lane < HEAD_DIM
    keep_first = first_head.astype(BF16)
    keep_second = 1 - keep_first
    n_keys = WIN_ROWS * GRID_W
    ones_block = jnp.ones((n_keys, LANES), BF16)
    win_row0 = jnp.clip(blk * ROWS_PER_BLOCK - WIN_ROWS // 2, 0, rows_per_batch - KV_WINDOW_ROWS)

    def window(j):
        r = blk * ROWS_PER_BLOCK + j
        row_start = jnp.clip(r - WIN_ROWS // 2, 0, rows_per_batch - WIN_ROWS)
        ks = pl.multiple_of((row_start - win_row0) * GRID_W, GRID_W)
        return ks, r - row_start

    def scores(j, p):
        ks, d = window(j)
        cols = slice(p * LANES, (p + 1) * LANES)
        rows = slice(j * GRID_W, (j + 1) * GRID_W)
        q = q_ref[rows, cols]
        q2 = jnp.concatenate([q * keep_first, q * keep_second], axis=0)
        s = lax.dot_general(q2, kw_ref[0, pl.ds(ks, n_keys), cols], (((1,), (1,)), ((), ())),
                            preferred_element_type=F32)
        bias = jnp.concatenate(
            [tp_ref[p, WIN_ROWS - 1 - d + i] for i in range(0, WIN_ROWS, 2)], axis=-1)
        return s + bias

    def attend(j, p, s):
        ks, _ = window(j)
        cols = slice(p * LANES, (p + 1) * LANES)
        rows = slice(j * GRID_W, (j + 1) * GRID_W)
        m = jnp.max(s, axis=-1, keepdims=True)
        e = jnp.exp2(s - m)
        v_ext = jnp.concatenate([vw_ref[0, pl.ds(ks, n_keys), cols], ones_block], axis=1)
        o_ext = jnp.dot(e.astype(BF16), v_ext, preferred_element_type=F32)
        o = o_ext[:, :LANES] / o_ext[:, LANES:]
        attn_buf[rows, cols] = jnp.where(first_head, o[:GRID_W], o[GRID_W:]).astype(BF16)

    blk_in_tile = blk % MIX_PER_TOK_TILE
    at_tile_start = jnp.logical_and(blk_in_tile == 0, blk > 0)
    at_tile_end = jnp.logical_and(blk_in_tile == MIX_PER_TOK_TILE - 1, blk < n_blk - 1)
    fix_first = jnp.where(at_tile_start, ec_ref[2:3, :] * ep_ref[1:2, :], 0.0)
    fix_last = jnp.where(at_tile_end, ec_ref[3:4, :] * en_ref[0:1, :], 0.0)
    sub_row = lax.broadcasted_iota(jnp.int32, (BF16_SUBLANES, CONV_WIDTH), 0)
    top = slice(0, BF16_SUBLANES)
    bottom = slice(MIX_TILE - BF16_SUBLANES, MIX_TILE)
    bb_buf[...] = bb_ref[...]
    bb_buf[top, :] = (bb_ref[top, :].astype(F32)
                      + jnp.where(sub_row == 0, fix_first, 0.0)).astype(BF16)
    bb_buf[bottom, :] = (bb_ref[bottom, :].astype(F32)
                         + jnp.where(sub_row == BF16_SUBLANES - 1, fix_last, 0.0)).astype(BF16)

    pending = [scores(0, p) for p in range(HEAD_PAIRS)]
    for j in range(ROWS_PER_BLOCK):
        for p in range(HEAD_PAIRS):
            s = pending[p]
            if j + 1 < ROWS_PER_BLOCK:
                pending[p] = scores(j + 1, p)
            attend(j, p, s)

    ya = jnp.dot(attn_buf[...], wa_ref[...], preferred_element_type=F32)
    yb = jnp.dot(bb_buf[...], wb_ref[...], preferred_element_type=F32)
    merged = sga_ref[...] * ya.astype(BF16) + sgb_ref[...] * yb.astype(BF16)
    o_ref[...] = x_ref[...] + jnp.dot(merged, wo_ref[...], preferred_element_type=F32)


def _kv_window_start(blk, rows_per_batch):
    row0 = jnp.clip(blk * ROWS_PER_BLOCK - WIN_ROWS // 2, 0, rows_per_batch - KV_WINDOW_ROWS)
    return row0 * GRID_W


def _mixer(batch, seq, q, k, v, bb, edges, sga, sgb, x, rpb, layer, wa, wb, wo):
    rows = seq // GRID_W
    n_blk = rows // ROWS_PER_BLOCK
    tiles_per_seq = seq // TOK_TILE
    n_tiles = batch * tiles_per_seq

    def r3(a):
        return a.reshape(batch, seq, a.shape[-1])

    cur = lambda w: pl.BlockSpec((None, MIX_TILE, w), lambda b, i: (b, i, 0))
    kv_window = pl.BlockSpec(
        (pl.Element(1), pl.Element(KV_WINDOW_ROWS * GRID_W), pl.Element(ATTN_WIDTH)),
        lambda b, i: (b, _kv_window_start(i, rows), 0))
    def edge_spec(offset):
        def index(b, i):
            tile = b * tiles_per_seq + i // MIX_PER_TOK_TILE + offset
            return (jnp.clip(tile, 0, n_tiles - 1), 0)
        return pl.BlockSpec((EDGE_ROWS, CONV_WIDTH), index)

    aw = ATTN_WIDTH
    in_specs = [cur(aw), kv_window, kv_window,
                cur(CONV_WIDTH), edge_spec(-1), edge_spec(0), edge_spec(1),
                cur(D_MODEL), cur(D_MODEL), cur(D_MODEL),
                pl.BlockSpec((None,) + rpb.shape[1:], lambda b, i: (layer, 0, 0, 0),
                             pipeline_mode=pl.Buffered(1)),
                _const_spec(wa.shape),
                _const_spec(wb.shape), _const_spec(wo.shape)]
    out = pl.pallas_call(
        functools.partial(_mixer_kernel, rows),
        grid=(batch, n_blk),
        in_specs=in_specs,
        out_specs=cur(D_MODEL),
        out_shape=jax.ShapeDtypeStruct((batch, seq, D_MODEL), F32),
        scratch_shapes=[pltpu.VMEM((MIX_TILE, ATTN_WIDTH), BF16),
                        pltpu.VMEM((MIX_TILE, CONV_WIDTH), BF16),
                        pltpu.VMEM((HEAD_PAIRS, N_REL_ROWS - 1, 2 * GRID_W, LANES), F32)],
        compiler_params=pltpu.CompilerParams(
            dimension_semantics=("arbitrary", "arbitrary"), vmem_limit_bytes=VMEM_LIMIT),
        name="mixer",
    )(r3(q), r3(k), r3(v), r3(bb), edges, edges, edges, r3(sga), r3(sgb), x,
      rpb, wa, wb, wo)
    return out


def _build_bias_table(rpb_ref, tbl_ref):
    c = lax.broadcasted_iota(jnp.int32, (GRID_W, LANES), 0)
    x = lax.broadcasted_iota(jnp.int32, (GRID_W, LANES), 1)
    kc = x % GRID_W
    win_start = jnp.clip(c - WIN_COLS // 2, 0, GRID_W - WIN_COLS)
    in_window = (kc >= win_start) & (kc < win_start + WIN_COLS)
    left = x < GRID_W
    pad = GRID_W - WIN_COLS

    for h in range(N_HEADS):
        rows = slice((h % 2) * GRID_W, (h % 2 + 1) * GRID_W)
        g = jnp.concatenate(
            [jnp.zeros((N_REL_ROWS, pad), F32), rpb_ref[h] * LOG2E,
             jnp.zeros((N_REL_ROWS, LANES - N_REL_COLS - pad), F32)], axis=1)
        for rr in range(N_REL_ROWS - 1):
            g0 = jnp.broadcast_to(g[rr:rr + 1, :], (GRID_W, LANES))
            g1 = jnp.broadcast_to(g[rr + 1:rr + 2, :], (GRID_W, LANES))
            a = pltpu.roll(g0, LANES - (GRID_W - 1), 1, stride=1, stride_axis=0)
            b = pltpu.roll(g1, 1, 1, stride=1, stride_axis=0)
            tbl_ref[h // 2, rr, rows, :] = jnp.where(in_window, jnp.where(left, a, b), NEG_INF)


def _mlp_kernel(x_ref, g_ref, w1_ref, w2_ref, o_ref, h_buf):
    def act(h):
        return jnp.square(jnp.maximum(h, 0.0)).astype(BF16)

    first = slice(0, FF_CHUNK)
    u_parts = []
    for r0 in range(0, TOK_TILE, NORM_ROWS):
        xf = x_ref[r0:r0 + NORM_ROWS, :]
        ms = jnp.mean(xf * xf, axis=-1, keepdims=True)
        u_part = ((xf * lax.rsqrt(ms + EPS)) * g_ref[...]).astype(BF16)
        h_buf[r0:r0 + NORM_ROWS, first] = act(
            jnp.dot(u_part, w1_ref[:, first], preferred_element_type=F32))
        u_parts.append(u_part)
    u = jnp.concatenate(u_parts, axis=0)
    for c in range(1, D_FF // FF_CHUNK):
        cols = slice(c * FF_CHUNK, (c + 1) * FF_CHUNK)
        h_buf[:, cols] = act(jnp.dot(u, w1_ref[:, cols], preferred_element_type=F32))
    o_ref[...] = x_ref[...] + jnp.dot(h_buf[...], w2_ref[...], preferred_element_type=F32)


def _mlp(x2d, g2, layer, w1, w2):
    m = x2d.shape[0]
    tok = pl.BlockSpec((TOK_TILE, D_MODEL), lambda i: (i, 0))
    return pl.pallas_call(
        _mlp_kernel,
        grid=(m // TOK_TILE,),
        in_specs=[tok, _layer_spec(g2, layer), _const_spec(w1.shape), _const_spec(w2.shape)],
        out_specs=tok,
        out_shape=jax.ShapeDtypeStruct((m, D_MODEL), F32),
        scratch_shapes=[pltpu.VMEM((TOK_TILE, D_FF), BF16)],
        compiler_params=pltpu.CompilerParams(
            dimension_semantics=("arbitrary",), vmem_limit_bytes=VMEM_LIMIT),
        name="mlp",
    )(x2d, g2, w1, w2)


def kernel(x, norm1_g, w_in, q_norm_g, k_norm_g, rpb, conv_w, w_attn_branch, w_conv_branch,
           w_o, norm2_g, w_mlp_in, w_mlp_out):
    batch, seq, d_model = x.shape
    depth = w_in.shape[0]
    assert d_model == D_MODEL and w_in.shape[1:] == (D_MODEL, PROJ_WIDTH), (x.shape, w_in.shape)
    assert rpb.shape[1:] == (N_HEADS, N_REL_ROWS, N_REL_COLS), rpb.shape
    assert w_mlp_in.shape[1:] == (D_MODEL, D_FF) and w_mlp_out.shape[1:] == (D_FF, D_MODEL)
    assert seq % TOK_TILE == 0
    assert (seq // GRID_W) % ROWS_PER_BLOCK == 0 and seq // GRID_W >= KV_WINDOW_ROWS
    row3 = lambda a: a.reshape(depth, 1, a.shape[-1])
    for l in range(depth):
        x2d = x.reshape(batch * seq, D_MODEL)
        q, k, v, bb, edges, sga, sgb, wa, wb, wo, w1, w2 = _inproj(
            x2d, row3(norm1_g), w_in, row3(q_norm_g), row3(k_norm_g), conv_w, l,
            (w_attn_branch, w_conv_branch, w_o, w_mlp_in, w_mlp_out))
        x = _mixer(batch, seq, q, k, v, bb, edges, sga, sgb, x, rpb, l, wa, wb, wo)
        x = _mlp(x.reshape(batch * seq, D_MODEL), row3(norm2_g), l, w1, w2
                 ).reshape(batch, seq, D_MODEL)
    return x
```

```python
import functools

import jax
import jax.numpy as jnp
from jax import lax
from jax.experimental import pallas as pl
from jax.experimental.pallas import tpu as pltpu

F32 = jnp.float32
BF16 = jnp.bfloat16

D_MODEL = 1024
GRID_W = 64
N_HEADS = 8
HEAD_DIM = 64
ATTN_WIDTH = N_HEADS * HEAD_DIM
CONV_WIDTH = D_MODEL // 2
WIN_ROWS = 8
WIN_COLS = 16
D_FF = 4 * D_MODEL
EPS = 1e-6
NEG_INF = -1e30
LOG2E = 1.4426950408889634
PROJ_WIDTH = 3 * ATTN_WIDTH + 3 * CONV_WIDTH + 2 * D_MODEL

LANES = 128
HEAD_PAIRS = ATTN_WIDTH // LANES
N_REL_ROWS = 2 * WIN_ROWS - 1
N_REL_COLS = 2 * WIN_COLS - 1

TOK_TILE = 1024
NORM_ROWS = 256
ROWS_PER_BLOCK = 16
MIX_TILE = ROWS_PER_BLOCK * GRID_W
KV_WINDOW_ROWS = ROWS_PER_BLOCK + WIN_ROWS
MIX_PER_TOK_TILE = TOK_TILE // MIX_TILE
assert MIX_PER_TOK_TILE * MIX_TILE == TOK_TILE
FF_CHUNK = 512
BF16_SUBLANES = 16
EDGE_ROWS = 8
MIB = 1024 * 1024
V7X_VMEM_BYTES = 64 * MIB
VMEM_LIMIT = 56 * MIB
INPROJ_VMEM_LIMIT = 61 * MIB
assert VMEM_LIMIT < INPROJ_VMEM_LIMIT < V7X_VMEM_BYTES


def _const_spec(shape):
    zeros = (0,) * len(shape)
    return pl.BlockSpec(shape, lambda *_: zeros, pipeline_mode=pl.Buffered(1))


def _layer_spec(stacked, layer):
    return pl.BlockSpec((None,) + stacked.shape[1:], lambda *_: (layer, 0, 0),
                        pipeline_mode=pl.Buffered(1))


def _sigmoid(x):
    return 0.5 * jnp.tanh(0.5 * x) + 0.5


def _inproj_kernel(n_cast, x_ref, g1_ref, w_ref, gq_ref, gk_ref, cw_ref, *refs):
    cast_in, refs = refs[:n_cast], refs[n_cast:]
    q_ref, k_ref, v_ref, bb_ref, edge_ref, sga_ref, sgb_ref = refs[:7]
    cast_out = refs[7:-1]
    z_buf = refs[-1]
    for src, dst in zip(cast_in, cast_out):
        dst[...] = src[...].astype(BF16)

    def weights(c0, width):
        return w_ref[:, c0:c0 + width].astype(BF16)

    c0 = 3 * ATTN_WIDTH
    g0 = c0 + 3 * CONV_WIDTH
    w_first = weights(g0, D_MODEL)
    u_parts = []
    for r0 in range(0, TOK_TILE, NORM_ROWS):
        xf = x_ref[r0:r0 + NORM_ROWS, :]
        ms = jnp.mean(xf * xf, axis=-1, keepdims=True)
        u_part = ((xf * lax.rsqrt(ms + EPS)) * g1_ref[...]).astype(BF16)
        sga_ref[r0:r0 + NORM_ROWS, :] = _sigmoid(
            jnp.dot(u_part, w_first, preferred_element_type=F32)).astype(BF16)
        u_parts.append(u_part)
    u = jnp.concatenate(u_parts, axis=0)

    def proj(c0, width):
        return jnp.dot(u, weights(c0, width), preferred_element_type=F32)

    first_head = lax.broadcasted_iota(jnp.int32, (TOK_TILE, LANES), 1) < HEAD_DIM

    def head_norm(t, g):
        outs = []
        for p in range(HEAD_PAIRS):
            tp = t[:, p * LANES:(p + 1) * LANES]
            t2 = tp * tp
            ss_a = jnp.sum(jnp.where(first_head, t2, 0.0), axis=-1, keepdims=True)
            ss_b = jnp.sum(jnp.where(first_head, 0.0, t2), axis=-1, keepdims=True)
            r_a = lax.rsqrt(ss_a * (1.0 / HEAD_DIM) + EPS)
            r_b = lax.rsqrt(ss_b * (1.0 / HEAD_DIM) + EPS)
            outs.append(tp * jnp.where(first_head, r_a, r_b))
        return jnp.concatenate(outs, axis=-1) * g

    sgb_ref[...] = _sigmoid(proj(g0 + D_MODEL, D_MODEL)).astype(BF16)
    gq = jnp.concatenate([gq_ref[...]] * N_HEADS, axis=1)
    gk = jnp.concatenate([gk_ref[...]] * N_HEADS, axis=1)
    q_ref[...] = head_norm(proj(0, ATTN_WIDTH), gq * (LOG2E * HEAD_DIM ** -0.5)).astype(BF16)
    k_ref[...] = head_norm(proj(ATTN_WIDTH, ATTN_WIDTH), gk).astype(BF16)
    z = proj(c0 + CONV_WIDTH, CONV_WIDTH) * proj(c0 + 2 * CONV_WIDTH, CONV_WIDTH)
    cb = proj(c0, CONV_WIDTH)
    w_prev, w_mid, w_next = cw_ref[0:1, :], cw_ref[1:2, :], cw_ref[2:3, :]
    guard = jnp.zeros((EDGE_ROWS, CONV_WIDTH), F32)
    z_buf[0:EDGE_ROWS, :] = guard
    z_buf[EDGE_ROWS + TOK_TILE:, :] = guard
    z_buf[EDGE_ROWS:EDGE_ROWS + TOK_TILE, :] = z
    z_m1 = z_buf[EDGE_ROWS - 1:EDGE_ROWS - 1 + TOK_TILE, :]
    z_p1 = z_buf[EDGE_ROWS + 1:EDGE_ROWS + 1 + TOK_TILE, :]
    bb_ref[...] = (cb * (z_m1 * w_prev + z * w_mid + z_p1 * w_next)).astype(BF16)
    edge_ref[...] = jnp.concatenate(
        [z[0:1], z[TOK_TILE - 1:TOK_TILE], cb[0:1] * w_prev, cb[TOK_TILE - 1:TOK_TILE] * w_next,
         jnp.zeros((EDGE_ROWS - 4, CONV_WIDTH), F32)], axis=0)
    v_ref[...] = proj(2 * ATTN_WIDTH, ATTN_WIDTH).astype(BF16)


def _inproj(x2d, g1, w_in, gq, gk, cw, layer, later_weights):
    m = x2d.shape[0]
    n_steps = m // TOK_TILE
    tok = lambda w: pl.BlockSpec((TOK_TILE, w), lambda i: (i, 0))
    act_specs = ([tok(ATTN_WIDTH)] * 3 + [tok(CONV_WIDTH)]
                 + [pl.BlockSpec((EDGE_ROWS, CONV_WIDTH), lambda i: (i, 0))] + [tok(D_MODEL)] * 2)
    act_shapes = ([jax.ShapeDtypeStruct((m, ATTN_WIDTH), BF16)] * 3
                  + [jax.ShapeDtypeStruct((m, CONV_WIDTH), BF16),
                     jax.ShapeDtypeStruct((n_steps * EDGE_ROWS, CONV_WIDTH), F32)]
                  + [jax.ShapeDtypeStruct((m, D_MODEL), BF16)] * 2)
    cast_in_specs, cast_out_specs = [], []
    for w in later_weights:
        rows = w.shape[1] // n_steps
        assert rows * n_steps == w.shape[1] and rows % BF16_SUBLANES == 0, w.shape
        cast_in_specs.append(pl.BlockSpec((None, rows, w.shape[2]), lambda i: (layer, i, 0)))
        cast_out_specs.append(pl.BlockSpec((rows, w.shape[2]), lambda i: (i, 0)))
    return pl.pallas_call(
        functools.partial(_inproj_kernel, len(later_weights)),
        grid=(n_steps,),
        in_specs=[tok(D_MODEL), _layer_spec(g1, layer), _layer_spec(w_in, layer),
                  _layer_spec(gq, layer), _layer_spec(gk, layer), _layer_spec(cw, layer)]
        + cast_in_specs,
        out_specs=act_specs + cast_out_specs,
        out_shape=(act_shapes
                   + [jax.ShapeDtypeStruct(w.shape[1:], BF16) for w in later_weights]),
        compiler_params=pltpu.CompilerParams(
            dimension_semantics=("arbitrary",), vmem_limit_bytes=INPROJ_VMEM_LIMIT),
        scratch_shapes=[pltpu.VMEM((TOK_TILE + 2 * EDGE_ROWS, CONV_WIDTH), F32)],
        name="inproj",
    )(x2d, g1, w_in, gq, gk, cw, *later_weights)


def _mixer_kernel(rows_per_batch,
                  q_ref, kw_ref, vw_ref,
                  bb_ref, ep_ref, ec_ref, en_ref, sga_ref, sgb_ref, x_ref,
                  rpb_ref, wa_ref, wb_ref, wo_ref,
                  o_ref, attn_buf, bb_buf, tp_ref):
    blk = pl.program_id(1)
    n_blk = pl.num_programs(1)

    @pl.when(jnp.logical_and(pl.program_id(0) == 0, blk == 0))
    def _():
        _build_bias_table(rpb_ref, tp_ref)

    lane = lax.broadcasted_iota(jnp.int32, (GRID_W, LANES), 1)
    first_head = lane < HEAD_DIM
    keep_first = first_head.astype(BF16)
    keep_second = 1 - keep_first
    n_keys = WIN_ROWS * GRID_W
    ones_block = jnp.ones((n_keys, LANES), BF16)
    win_row0 = jnp.clip(blk * ROWS_PER_BLOCK - WIN_ROWS // 2, 0, rows_per_batch - KV_WINDOW_ROWS)

    def window(j):
        r = blk * ROWS_PER_BLOCK + j
        row_start = jnp.clip(r - WIN_ROWS // 2, 0, rows_per_batch - WIN_ROWS)
        ks = pl.multiple_of((row_start - win_row0) * GRID_W, GRID_W)
        return ks, r - row_start

    def scores(j, p):
        ks, d = window(j)
        cols = slice(p * LANES, (p + 1) * LANES)
        rows = slice(j * GRID_W, (j + 1) * GRID_W)
        q = q_ref[rows, cols]
        q2 = jnp.concatenate([q * keep_first, q * keep_second], axis=0)
        s = lax.dot_general(q2, kw_ref[0, pl.ds(ks, n_keys), cols], (((1,), (1,)), ((), ())),
                            preferred_element_type=F32)
        bias = jnp.concatenate(
            [tp_ref[p, WIN_ROWS - 1 - d + i] for i in range(0, WIN_ROWS, 2)], axis=-1)
        return s + bias

    def attend(j, p, s):
        ks, _ = window(j)
        cols = slice(p * LANES, (p + 1) * LANES)
        rows = slice(j * GRID_W, (j + 1) * GRID_W)
        m = jnp.max(s, axis=-1, keepdims=True)
        e = jnp.exp2(s - m)
        v_ext = jnp.concatenate([vw_ref[0, pl.ds(ks, n_keys), cols], ones_block], axis=1)
        o_ext = jnp.dot(e.astype(BF16), v_ext, preferred_element_type=F32)
        o = o_ext[:, :LANES] / o_ext[:, LANES:]
        attn_buf[rows, cols] = jnp.where(first_head, o[:GRID_W], o[GRID_W:]).astype(BF16)

    blk_in_tile = blk % MIX_PER_TOK_TILE
    at_tile_start = jnp.logical_and(blk_in_tile == 0, blk > 0)
    at_tile_end = jnp.logical_and(blk_in_tile == MIX_PER_TOK_TILE - 1, blk < n_blk - 1)
    fix_first = jnp.where(at_tile_start, ec_ref[2:3, :] * ep_ref[1:2, :], 0.0)
    fix_last = jnp.where(at_tile_end, ec_ref[3:4, :] * en_ref[0:1, :], 0.0)
    sub_row = lax.broadcasted_iota(jnp.int32, (BF16_SUBLANES, CONV_WIDTH), 0)
    top = slice(0, BF16_SUBLANES)
    bottom = slice(MIX_TILE - BF16_SUBLANES, MIX_TILE)
    bb_buf[...] = bb_ref[...]
    bb_buf[top, :] = (bb_ref[top, :].astype(F32)
                      + jnp.where(sub_row == 0, fix_first, 0.0)).astype(BF16)
    bb_buf[bottom, :] = (bb_ref[bottom, :].astype(F32)
                         + jnp.where(sub_row == BF16_SUBLANES - 1, fix_last, 0.0)).astype(BF16)

    pending = [scores(0, p) for p in range(HEAD_PAIRS)]
    for j in range(ROWS_PER_BLOCK):
        for p in range(HEAD_PAIRS):
            s = pending[p]
            if j + 1 < ROWS_PER_BLOCK:
                pending[p] = scores(j + 1, p)
            attend(j, p, s)

    ya = jnp.dot(attn_buf[...], wa_ref[...], preferred_element_type=F32)
    yb = jnp.dot(bb_buf[...], wb_ref[...], preferred_element_type=F32)
    merged = sga_ref[...] * ya.astype(BF16) + sgb_ref[...] * yb.astype(BF16)
    o_ref[...] = x_ref[...] + jnp.dot(merged, wo_ref[...], preferred_element_type=F32)


def _kv_window_start(blk, rows_per_batch):
    row0 = jnp.clip(blk * ROWS_PER_BLOCK - WIN_ROWS // 2, 0, rows_per_batch - KV_WINDOW_ROWS)
    return row0 * GRID_W


def _mixer(batch, seq, q, k, v, bb, edges, sga, sgb, x, rpb, layer, wa, wb, wo):
    rows = seq // GRID_W
    n_blk = rows // ROWS_PER_BLOCK
    tiles_per_seq = seq // TOK_TILE
    n_tiles = batch * tiles_per_seq

    def r3(a):
        return a.reshape(batch, seq, a.shape[-1])

    cur = lambda w: pl.BlockSpec((None, MIX_TILE, w), lambda b, i: (b, i, 0))
    kv_window = pl.BlockSpec(
        (pl.Element(1), pl.Element(KV_WINDOW_ROWS * GRID_W), pl.Element(ATTN_WIDTH)),
        lambda b, i: (b, _kv_window_start(i, rows), 0))
    def edge_spec(offset):
        def index(b, i):
            tile = b * tiles_per_seq + i // MIX_PER_TOK_TILE + offset
            return (jnp.clip(tile, 0, n_tiles - 1), 0)
        return pl.BlockSpec((EDGE_ROWS, CONV_WIDTH), index)

    aw = ATTN_WIDTH
    in_specs = [cur(aw), kv_window, kv_window,
                cur(CONV_WIDTH), edge_spec(-1), edge_spec(0), edge_spec(1),
                cur(D_MODEL), cur(D_MODEL), cur(D_MODEL),
                pl.BlockSpec((None,) + rpb.shape[1:], lambda b, i: (layer, 0, 0, 0),
                             pipeline_mode=pl.Buffered(1)),
                _const_spec(wa.shape),
                _const_spec(wb.shape), _const_spec(wo.shape)]
    out = pl.pallas_call(
        functools.partial(_mixer_kernel, rows),
        grid=(batch, n_blk),
        in_specs=in_specs,
        out_specs=cur(D_MODEL),
        out_shape=jax.ShapeDtypeStruct((batch, seq, D_MODEL), F32),
        scratch_shapes=[pltpu.VMEM((MIX_TILE, ATTN_WIDTH), BF16),
                        pltpu.VMEM((MIX_TILE, CONV_WIDTH), BF16),
                        pltpu.VMEM((HEAD_PAIRS, N_REL_ROWS - 1, 2 * GRID_W, LANES), F32)],
        compiler_params=pltpu.CompilerParams(
            dimension_semantics=("arbitrary", "arbitrary"), vmem_limit_bytes=VMEM_LIMIT),
        name="mixer",
    )(r3(q), r3(k), r3(v), r3(bb), edges, edges, edges, r3(sga), r3(sgb), x,
      rpb, wa, wb, wo)
    return out


def _build_bias_table(rpb_ref, tbl_ref):
    c = lax.broadcasted_iota(jnp.int32, (GRID_W, LANES), 0)
    x = lax.broadcasted_iota(jnp.int32, (GRID_W, LANES), 1)
    kc = x % GRID_W
    win_start = jnp.clip(c - WIN_COLS // 2, 0, GRID_W - WIN_COLS)
    in_window = (kc >= win_start) & (kc < win_start + WIN_COLS)
    left = x < GRID_W
    pad = GRID_W - WIN_COLS

    for h in range(N_HEADS):
        rows = slice((h % 2) * GRID_W, (h % 2 + 1) * GRID_W)
        g = jnp.concatenate(
            [jnp.zeros((N_REL_ROWS, pad), F32), rpb_ref[h] * LOG2E,
             jnp.zeros((N_REL_ROWS, LANES - N_REL_COLS - pad), F32)], axis=1)
        for rr in range(N_REL_ROWS - 1):
            g0 = jnp.broadcast_to(g[rr:rr + 1, :], (GRID_W, LANES))
            g1 = jnp.broadcast_to(g[rr + 1:rr + 2, :], (GRID_W, LANES))
            a = pltpu.roll(g0, LANES - (GRID_W - 1), 1, stride=1, stride_axis=0)
            b = pltpu.roll(g1, 1, 1, stride=1, stride_axis=0)
            tbl_ref[h // 2, rr, rows, :] = jnp.where(in_window, jnp.where(left, a, b), NEG_INF)


def _mlp_kernel(x_ref, g_ref, w1_ref, w2_ref, o_ref, h_buf):
    def act(h):
        return jnp.square(jnp.maximum(h, 0.0)).astype(BF16)

    first = slice(0, FF_CHUNK)
    u_parts = []
    for r0 in range(0, TOK_TILE, NORM_ROWS):
        xf = x_ref[r0:r0 + NORM_ROWS, :]
        ms = jnp.mean(xf * xf, axis=-1, keepdims=True)
        u_part = ((xf * lax.rsqrt(ms + EPS)) * g_ref[...]).astype(BF16)
        h_buf[r0:r0 + NORM_ROWS, first] = act(
            jnp.dot(u_part, w1_ref[:, first], preferred_element_type=F32))
        u_parts.append(u_part)
    u = jnp.concatenate(u_parts, axis=0)
    for c in range(1, D_FF // FF_CHUNK):
        cols = slice(c * FF_CHUNK, (c + 1) * FF_CHUNK)
        h_buf[:, cols] = act(jnp.dot(u, w1_ref[:, cols], preferred_element_type=F32))
    o_ref[...] = x_ref[...] + jnp.dot(h_buf[...], w2_ref[...], preferred_element_type=F32)


def _mlp(x2d, g2, layer, w1, w2):
    m = x2d.shape[0]
    tok = pl.BlockSpec((TOK_TILE, D_MODEL), lambda i: (i, 0))
    return pl.pallas_call(
        _mlp_kernel,
        grid=(m // TOK_TILE,),
        in_specs=[tok, _layer_spec(g2, layer), _const_spec(w1.shape), _const_spec(w2.shape)],
        out_specs=tok,
        out_shape=jax.ShapeDtypeStruct((m, D_MODEL), F32),
        scratch_shapes=[pltpu.VMEM((TOK_TILE, D_FF), BF16)],
        compiler_params=pltpu.CompilerParams(
            dimension_semantics=("arbitrary",), vmem_limit_bytes=VMEM_LIMIT),
        name="mlp",
    )(x2d, g2, w1, w2)


def kernel(x, norm1_g, w_in, q_norm_g, k_norm_g, rpb, conv_w, w_attn_branch, w_conv_branch,
           w_o, norm2_g, w_mlp_in, w_mlp_out):
    batch, seq, d_model = x.shape
    depth = w_in.shape[0]
    assert d_model == D_MODEL and w_in.shape[1:] == (D_MODEL, PROJ_WIDTH), (x.shape, w_in.shape)
    assert rpb.shape[1:] == (N_HEADS, N_REL_ROWS, N_REL_COLS), rpb.shape
    assert w_mlp_in.shape[1:] == (D_MODEL, D_FF) and w_mlp_out.shape[1:] == (D_FF, D_MODEL)
    assert seq % TOK_TILE == 0
    assert (seq // GRID_W) % ROWS_PER_BLOCK == 0 and seq // GRID_W >= KV_WINDOW_ROWS
    row3 = lambda a: a.reshape(depth, 1, a.shape[-1])
    for l in range(depth):
        x2d = x.reshape(batch * seq, D_MODEL)
        q, k, v, bb, edges, sga, sgb, wa, wb, wo, w1, w2 = _inproj(
            x2d, row3(norm1_g), w_in, row3(q_norm_g), row3(k_norm_g), conv_w, l,
            (w_attn_branch, w_conv_branch, w_o, w_mlp_in, w_mlp_out))
        x = _mixer(batch, seq, q, k, v, bb, edges, sga, sgb, x, rpb, l, wa, wb, wo)
        x = _mlp(x.reshape(batch * seq, D_MODEL), row3(norm2_g), l, w1, w2
                 ).reshape(batch, seq, D_MODEL)
    return x
```

```python
import functools

import jax
import jax.numpy as jnp
from jax import lax
from jax.experimental import pallas as pl
from jax.experimental.pallas import tpu as pltpu

F32 = jnp.float32
BF16 = jnp.bfloat16

D_MODEL = 1024
GRID_W = 64
N_HEADS = 8
HEAD_DIM = 64
ATTN_WIDTH = N_HEADS * HEAD_DIM
CONV_WIDTH = D_MODEL // 2
WIN_ROWS = 8
WIN_COLS = 16
D_FF = 4 * D_MODEL
EPS = 1e-6
NEG_INF = -1e30
LOG2E = 1.4426950408889634
PROJ_WIDTH = 3 * ATTN_WIDTH + 3 * CONV_WIDTH + 2 * D_MODEL

LANES = 128
HEAD_PAIRS = ATTN_WIDTH // LANES
N_REL_ROWS = 2 * WIN_ROWS - 1
N_REL_COLS = 2 * WIN_COLS - 1

TOK_TILE = 1024
NORM_ROWS = 256
ROWS_PER_BLOCK = 16
MIX_TILE = ROWS_PER_BLOCK * GRID_W
KV_WINDOW_ROWS = ROWS_PER_BLOCK + WIN_ROWS
MIX_PER_TOK_TILE = TOK_TILE // MIX_TILE
assert MIX_PER_TOK_TILE * MIX_TILE == TOK_TILE
FF_CHUNK = 512
BF16_SUBLANES = 16
EDGE_ROWS = 8
MIB = 1024 * 1024
V7X_VMEM_BYTES = 64 * MIB
VMEM_LIMIT = 56 * MIB
INPROJ_VMEM_LIMIT = 61 * MIB
assert VMEM_LIMIT < INPROJ_VMEM_LIMIT < V7X_VMEM_BYTES


def _const_spec(shape):
    zeros = (0,) * len(shape)
    return pl.BlockSpec(shape, lambda *_: zeros, pipeline_mode=pl.Buffered(1))


def _layer_spec(stacked, layer):
    return pl.BlockSpec((None,) + stacked.shape[1:], lambda *_: (layer, 0, 0),
                        pipeline_mode=pl.Buffered(1))


def _sigmoid(x):
    return 0.5 * jnp.tanh(0.5 * x) + 0.5


def _inproj_kernel(n_cast, x_ref, g1_ref, w_ref, gq_ref, gk_ref, cw_ref, *refs):
    cast_in, refs = refs[:n_cast], refs[n_cast:]
    q_ref, k_ref, v_ref, bb_ref, edge_ref, sga_ref, sgb_ref = refs[:7]
    cast_out = refs[7:-1]
    z_buf = refs[-1]
    for src, dst in zip(cast_in, cast_out):
        dst[...] = src[...].astype(BF16)

    def weights(c0, width):
        return w_ref[:, c0:c0 + width].astype(BF16)

    c0 = 3 * ATTN_WIDTH
    g0 = c0 + 3 * CONV_WIDTH
    w_first = weights(g0, D_MODEL)
    u_parts = []
    for r0 in range(0, TOK_TILE, NORM_ROWS):
        xf = x_ref[r0:r0 + NORM_ROWS, :]
        ms = jnp.mean(xf * xf, axis=-1, keepdims=True)
        u_part = ((xf * lax.rsqrt(ms + EPS)) * g1_ref[...]).astype(BF16)
        sga_ref[r0:r0 + NORM_ROWS, :] = _sigmoid(
            jnp.dot(u_part, w_first, preferred_element_type=F32)).astype(BF16)
        u_parts.append(u_part)
    u = jnp.concatenate(u_parts, axis=0)

    def proj(c0, width):
        return jnp.dot(u, weights(c0, width), preferred_element_type=F32)

    first_head = lax.broadcasted_iota(jnp.int32, (TOK_TILE, LANES), 1) < HEAD_DIM

    def head_norm(t, g):
        outs = []
        for p in range(HEAD_PAIRS):
            tp = t[:, p * LANES:(p + 1) * LANES]
            t2 = tp * tp
            ss_a = jnp.sum(jnp.where(first_head, t2, 0.0), axis=-1, keepdims=True)
            ss_b = jnp.sum(jnp.where(first_head, 0.0, t2), axis=-1, keepdims=True)
            r_a = lax.rsqrt(ss_a * (1.0 / HEAD_DIM) + EPS)
            r_b = lax.rsqrt(ss_b * (1.0 / HEAD_DIM) + EPS)
            outs.append(tp * jnp.where(first_head, r_a, r_b))
        return jnp.concatenate(outs, axis=-1) * g

    sgb_ref[...] = _sigmoid(proj(g0 + D_MODEL, D_MODEL)).astype(BF16)
    gq = jnp.concatenate([gq_ref[...]] * N_HEADS, axis=1)
    gk = jnp.concatenate([gk_ref[...]] * N_HEADS, axis=1)
    q_ref[...] = head_norm(proj(0, ATTN_WIDTH), gq * (LOG2E * HEAD_DIM ** -0.5)).astype(BF16)
    k_ref[...] = head_norm(proj(ATTN_WIDTH, ATTN_WIDTH), gk).astype(BF16)
    z = proj(c0 + CONV_WIDTH, CONV_WIDTH) * proj(c0 + 2 * CONV_WIDTH, CONV_WIDTH)
    cb = proj(c0, CONV_WIDTH)
    w_prev, w_mid, w_next = cw_ref[0:1, :], cw_ref[1:2, :], cw_ref[2:3, :]
    guard = jnp.zeros((EDGE_ROWS, CONV_WIDTH), F32)
    z_buf[0:EDGE_ROWS, :] = guard
    z_buf[EDGE_ROWS + TOK_TILE:, :] = guard
    z_buf[EDGE_ROWS:EDGE_ROWS + TOK_TILE, :] = z
    z_m1 = z_buf[EDGE_ROWS - 1:EDGE_ROWS - 1 + TOK_TILE, :]
    z_p1 = z_buf[EDGE_ROWS + 1:EDGE_ROWS + 1 + TOK_TILE, :]
    bb_ref[...] = (cb * (z_m1 * w_prev + z * w_mid + z_p1 * w_next)).astype(BF16)
    edge_ref[...] = jnp.concatenate(
        [z[0:1], z[TOK_TILE - 1:TOK_TILE], cb[0:1] * w_prev, cb[TOK_TILE - 1:TOK_TILE] * w_next,
         jnp.zeros((EDGE_ROWS - 4, CONV_WIDTH), F32)], axis=0)
    v_ref[...] = proj(2 * ATTN_WIDTH, ATTN_WIDTH).astype(BF16)


def _inproj(x2d, g1, w_in, gq, gk, cw, layer, later_weights):
    m = x2d.shape[0]
    n_steps = m // TOK_TILE
    tok = lambda w: pl.BlockSpec((TOK_TILE, w), lambda i: (i, 0))
    act_specs = ([tok(ATTN_WIDTH)] * 3 + [tok(CONV_WIDTH)]
                 + [pl.BlockSpec((EDGE_ROWS, CONV_WIDTH), lambda i: (i, 0))] + [tok(D_MODEL)] * 2)
    act_shapes = ([jax.ShapeDtypeStruct((m, ATTN_WIDTH), BF16)] * 3
                  + [jax.ShapeDtypeStruct((m, CONV_WIDTH), BF16),
                     jax.ShapeDtypeStruct((n_steps * EDGE_ROWS, CONV_WIDTH), F32)]
                  + [jax.ShapeDtypeStruct((m, D_MODEL), BF16)] * 2)
    cast_in_specs, cast_out_specs = [], []
    for w in later_weights:
        rows = w.shape[1] // n_steps
        assert rows * n_steps == w.shape[1] and rows % BF16_SUBLANES == 0, w.shape
        cast_in_specs.append(pl.BlockSpec((None, rows, w.shape[2]), lambda i: (layer, i, 0)))
        cast_out_specs.append(pl.BlockSpec((rows, w.shape[2]), lambda i: (i, 0)))
    return pl.pallas_call(
        functools.partial(_inproj_kernel, len(later_weights)),
        grid=(n_steps,),
        in_specs=[tok(D_MODEL), _layer_spec(g1, layer), _layer_spec(w_in, layer),
                  _layer_spec(gq, layer), _layer_spec(gk, layer), _layer_spec(cw, layer)]
        + cast_in_specs,
        out_specs=act_specs + cast_out_specs,
        out_shape=(act_shapes
                   + [jax.ShapeDtypeStruct(w.shape[1:], BF16) for w in later_weights]),
        compiler_params=pltpu.CompilerParams(
            dimension_semantics=("arbitrary",), vmem_limit_bytes=INPROJ_VMEM_LIMIT),
        scratch_shapes=[pltpu.VMEM((TOK_TILE + 2 * EDGE_ROWS, CONV_WIDTH), F32)],
        name="inproj",
    )(x2d, g1, w_in, gq, gk, cw, *later_weights)


def _mixer_kernel(rows_per_batch,
                  q_ref, kw_ref, vw_ref,
                  bb_ref, ep_ref, ec_ref, en_ref, sga_ref, sgb_ref, x_ref,
                  rpb_ref, wa_ref, wb_ref, wo_ref,
                  o_ref, attn_buf, bb_buf, tp_ref):
    blk = pl.program_id(1)
    n_blk = pl.num_programs(1)

    @pl.when(jnp.logical_and(pl.program_id(0) == 0, blk == 0))
    def _():
        _build_bias_table(rpb_ref, tp_ref)

    lane = lax.broadcasted_iota(jnp.int32, (GRID_W, LANES), 1)
    first_head = lane < HEAD_DIM
    keep_first = first_head.astype(BF16)
    keep_second = 1 - keep_first
    n_keys = WIN_ROWS * GRID_W
    ones_block = jnp.ones((n_keys, LANES), BF16)
    win_row0 = jnp.clip(blk * ROWS_PER_BLOCK - WIN_ROWS // 2, 0, rows_per_batch - KV_WINDOW_ROWS)

    def window(j):
        r = blk * ROWS_PER_BLOCK + j
        row_start = jnp.clip(r - WIN_ROWS // 2, 0, rows_per_batch - WIN_ROWS)
        ks = pl.multiple_of((row_start - win_row0) * GRID_W, GRID_W)
        return ks, r - row_start

    def scores(j, p):
        ks, d = window(j)
        cols = slice(p * LANES, (p + 1) * LANES)
        rows = slice(j * GRID_W, (j + 1) * GRID_W)
        q = q_ref[rows, cols]
        q2 = jnp.concatenate([q * keep_first, q * keep_second], axis=0)
        s = lax.dot_general(q2, kw_ref[0, pl.ds(ks, n_keys), cols], (((1,), (1,)), ((), ())),
                            preferred_element_type=F32)
        bias = jnp.concatenate(
            [tp_ref[p, WIN_ROWS - 1 - d + i] for i in range(0, WIN_ROWS, 2)], axis=-1)
        return s + bias

    def attend(j, p, s):
        ks, _ = window(j)
        cols = slice(p * LANES, (p + 1) * LANES)
        rows = slice(j * GRID_W, (j + 1) * GRID_W)
        m = jnp.max(s, axis=-1, keepdims=True)
        e = jnp.exp2(s - m)
        v_ext = jnp.concatenate([vw_ref[0, pl.ds(ks, n_keys), cols], ones_block], axis=1)
        o_ext = jnp.dot(e.astype(BF16), v_ext, preferred_element_type=F32)
        o = o_ext[:, :LANES] / o_ext[:, LANES:]
        attn_buf[rows, cols] = jnp.where(first_head, o[:GRID_W], o[GRID_W:]).astype(BF16)

    blk_in_tile = blk % MIX_PER_TOK_TILE
    at_tile_start = jnp.logical_and(blk_in_tile == 0, blk > 0)
    at_tile_end = jnp.logical_and(blk_in_tile == MIX_PER_TOK_TILE - 1, blk < n_blk - 1)
    fix_first = jnp.where(at_tile_start, ec_ref[2:3, :] * ep_ref[1:2, :], 0.0)
    fix_last = jnp.where(at_tile_end, ec_ref[3:4, :] * en_ref[0:1, :], 0.0)
    sub_row = lax.broadcasted_iota(jnp.int32, (BF16_SUBLANES, CONV_WIDTH), 0)
    top = slice(0, BF16_SUBLANES)
    bottom = slice(MIX_TILE - BF16_SUBLANES, MIX_TILE)
    bb_buf[...] = bb_ref[...]
    bb_buf[top, :] = (bb_ref[top, :].astype(F32)
                      + jnp.where(sub_row == 0, fix_first, 0.0)).astype(BF16)
    bb_buf[bottom, :] = (bb_ref[bottom, :].astype(F32)
                         + jnp.where(sub_row == BF16_SUBLANES - 1, fix_last, 0.0)).astype(BF16)

    pending = [scores(0, p) for p in range(HEAD_PAIRS)]
    for j in range(ROWS_PER_BLOCK):
        for p in range(HEAD_PAIRS):
            s = pending[p]
            if j + 1 < ROWS_PER_BLOCK:
                pending[p] = scores(j + 1, p)
            attend(j, p, s)

    ya = jnp.dot(attn_buf[...], wa_ref[...], preferred_element_type=F32)
    yb = jnp.dot(bb_buf[...], wb_ref[...], preferred_element_type=F32)
    merged = sga_ref[...] * ya.astype(BF16) + sgb_ref[...] * yb.astype(BF16)
    o_ref[...] = x_ref[...] + jnp.dot(merged, wo_ref[...], preferred_element_type=F32)


def _kv_window_start(blk, rows_per_batch):
    row0 = jnp.clip(blk * ROWS_PER_BLOCK - WIN_ROWS // 2, 0, rows_per_batch - KV_WINDOW_ROWS)
    return row0 * GRID_W


def _mixer(batch, seq, q, k, v, bb, edges, sga, sgb, x, rpb, layer, wa, wb, wo):
    rows = seq // GRID_W
    n_blk = rows // ROWS_PER_BLOCK
    tiles_per_seq = seq // TOK_TILE
    n_tiles = batch * tiles_per_seq

    def r3(a):
        return a.reshape(batch, seq, a.shape[-1])

    cur = lambda w: pl.BlockSpec((None, MIX_TILE, w), lambda b, i: (b, i, 0))
    kv_window = pl.BlockSpec(
        (pl.Element(1), pl.Element(KV_WINDOW_ROWS * GRID_W), pl.Element(ATTN_WIDTH)),
        lambda b, i: (b, _kv_window_start(i, rows), 0))
    def edge_spec(offset):
        def index(b, i):
            tile = b * tiles_per_seq + i // MIX_PER_TOK_TILE + offset
            return (jnp.clip(tile, 0, n_tiles - 1), 0)
        return pl.BlockSpec((EDGE_ROWS, CONV_WIDTH), index)

    aw = ATTN_WIDTH
    in_specs = [cur(aw), kv_window, kv_window,
                cur(CONV_WIDTH), edge_spec(-1), edge_spec(0), edge_spec(1),
                cur(D_MODEL), cur(D_MODEL), cur(D_MODEL),
                pl.BlockSpec((None,) + rpb.shape[1:], lambda b, i: (layer, 0, 0, 0),
                             pipeline_mode=pl.Buffered(1)),
                _const_spec(wa.shape),
                _const_spec(wb.shape), _const_spec(wo.shape)]
    out = pl.pallas_call(
        functools.partial(_mixer_kernel, rows),
        grid=(batch, n_blk),
        in_specs=in_specs,
        out_specs=cur(D_MODEL),
        out_shape=jax.ShapeDtypeStruct((batch, seq, D_MODEL), F32),
        scratch_shapes=[pltpu.VMEM((MIX_TILE, ATTN_WIDTH), BF16),
                        pltpu.VMEM((MIX_TILE, CONV_WIDTH), BF16),
                        pltpu.VMEM((HEAD_PAIRS, N_REL_ROWS - 1, 2 * GRID_W, LANES), F32)],
        compiler_params=pltpu.CompilerParams(
            dimension_semantics=("arbitrary", "arbitrary"), vmem_limit_bytes=VMEM_LIMIT),
        name="mixer",
    )(r3(q), r3(k), r3(v), r3(bb), edges, edges, edges, r3(sga), r3(sgb), x,
      rpb, wa, wb, wo)
    return out


def _build_bias_table(rpb_ref, tbl_ref):
    c = lax.broadcasted_iota(jnp.int32, (GRID_W, LANES), 0)
    x = lax.broadcasted_iota(jnp.int32, (GRID_W, LANES), 1)
    kc = x % GRID_W
    win_start = jnp.clip(c - WIN_COLS // 2, 0, GRID_W - WIN_COLS)
    in_window = (kc >= win_start) & (kc < win_start + WIN_COLS)
    left_half = lax.broadcasted_iota(jnp.int32, (1, LANES), 1) < GRID_W
    fill = jnp.zeros((N_REL_ROWS, GRID_W - N_REL_COLS), F32)

    for h in range(N_HEADS):
        rows = slice((h % 2) * GRID_W, (h % 2 + 1) * GRID_W)
        r = rpb_ref[h] * LOG2E
        first = jnp.concatenate([r, fill, jnp.zeros((N_REL_ROWS, GRID_W), F32)], axis=1)
        second = jnp.concatenate([jnp.zeros((N_REL_ROWS, GRID_W), F32), r, fill], axis=1)
        for rr in range(N_REL_ROWS - 1):
            src = jnp.where(left_half, first[rr:rr + 1, :], second[rr + 1:rr + 2, :])
            both = pltpu.roll(jnp.broadcast_to(src, (GRID_W, LANES)), LANES - (WIN_COLS - 1), 1,
                              stride=1, stride_axis=0)
            tbl_ref[h // 2, rr, rows, :] = jnp.where(in_window, both, NEG_INF)


def _mlp_kernel(x_ref, g_ref, w1_ref, w2_ref, o_ref, h_buf):
    def act(h):
        return jnp.square(jnp.maximum(h, 0.0)).astype(BF16)

    first = slice(0, FF_CHUNK)
    u_parts = []
    for r0 in range(0, TOK_TILE, NORM_ROWS):
        xf = x_ref[r0:r0 + NORM_ROWS, :]
        ms = jnp.mean(xf * xf, axis=-1, keepdims=True)
        u_part = ((xf * lax.rsqrt(ms + EPS)) * g_ref[...]).astype(BF16)
        h_buf[r0:r0 + NORM_ROWS, first] = act(
            jnp.dot(u_part, w1_ref[:, first], preferred_element_type=F32))
        u_parts.append(u_part)
    u = jnp.concatenate(u_parts, axis=0)
    for c in range(1, D_FF // FF_CHUNK):
        cols = slice(c * FF_CHUNK, (c + 1) * FF_CHUNK)
        h_buf[:, cols] = act(jnp.dot(u, w1_ref[:, cols], preferred_element_type=F32))
    o_ref[...] = x_ref[...] + jnp.dot(h_buf[...], w2_ref[...], preferred_element_type=F32)


def _mlp(x2d, g2, layer, w1, w2):
    m = x2d.shape[0]
    tok = pl.BlockSpec((TOK_TILE, D_MODEL), lambda i: (i, 0))
    return pl.pallas_call(
        _mlp_kernel,
        grid=(m // TOK_TILE,),
        in_specs=[tok, _layer_spec(g2, layer), _const_spec(w1.shape), _const_spec(w2.shape)],
        out_specs=tok,
        out_shape=jax.ShapeDtypeStruct((m, D_MODEL), F32),
        scratch_shapes=[pltpu.VMEM((TOK_TILE, D_FF), BF16)],
        compiler_params=pltpu.CompilerParams(
            dimension_semantics=("arbitrary",), vmem_limit_bytes=VMEM_LIMIT),
        name="mlp",
    )(x2d, g2, w1, w2)


def kernel(x, norm1_g, w_in, q_norm_g, k_norm_g, rpb, conv_w, w_attn_branch, w_conv_branch,
           w_o, norm2_g, w_mlp_in, w_mlp_out):
    batch, seq, d_model = x.shape
    depth = w_in.shape[0]
    assert d_model == D_MODEL and w_in.shape[1:] == (D_MODEL, PROJ_WIDTH), (x.shape, w_in.shape)
    assert rpb.shape[1:] == (N_HEADS, N_REL_ROWS, N_REL_COLS), rpb.shape
    assert w_mlp_in.shape[1:] == (D_MODEL, D_FF) and w_mlp_out.shape[1:] == (D_FF, D_MODEL)
    assert seq % TOK_TILE == 0
    assert (seq // GRID_W) % ROWS_PER_BLOCK == 0 and seq // GRID_W >= KV_WINDOW_ROWS
    row3 = lambda a: a.reshape(depth, 1, a.shape[-1])
    for l in range(depth):
        x2d = x.reshape(batch * seq, D_MODEL)
        q, k, v, bb, edges, sga, sgb, wa, wb, wo, w1, w2 = _inproj(
            x2d, row3(norm1_g), w_in, row3(q_norm_g), row3(k_norm_g), conv_w, l,
            (w_attn_branch, w_conv_branch, w_o, w_mlp_in, w_mlp_out))
        x = _mixer(batch, seq, q, k, v, bb, edges, sga, sgb, x, rpb, l, wa, wb, wo)
        x = _mlp(x.reshape(batch * seq, D_MODEL), row3(norm2_g), l, w1, w2
                 ).reshape(batch, seq, D_MODEL)
    return x
```

```python
import functools

import jax
import jax.numpy as jnp
from jax import lax
from jax.experimental import pallas as pl
from jax.experimental.pallas import tpu as pltpu

F32 = jnp.float32
BF16 = jnp.bfloat16

D_MODEL = 1024
GRID_W = 64
N_HEADS = 8
HEAD_DIM = 64
ATTN_WIDTH = N_HEADS * HEAD_DIM
CONV_WIDTH = D_MODEL // 2
WIN_ROWS = 8
WIN_COLS = 16
D_FF = 4 * D_MODEL
EPS = 1e-6
NEG_INF = -1e30
LOG2E = 1.4426950408889634
PROJ_WIDTH = 3 * ATTN_WIDTH + 3 * CONV_WIDTH + 2 * D_MODEL

LANES = 128
HEAD_PAIRS = ATTN_WIDTH // LANES
N_REL_ROWS = 2 * WIN_ROWS - 1
N_REL_COLS = 2 * WIN_COLS - 1

TOK_TILE = 1024
NORM_ROWS = 256
ROWS_PER_BLOCK = 16
MIX_TILE = ROWS_PER_BLOCK * GRID_W
KV_WINDOW_ROWS = ROWS_PER_BLOCK + WIN_ROWS
MIX_PER_TOK_TILE = TOK_TILE // MIX_TILE
assert MIX_PER_TOK_TILE * MIX_TILE == TOK_TILE
FF_CHUNK = 512
MLP_TILES_PER_STEP = 2
BF16_SUBLANES = 16
EDGE_ROWS = 8
MIB = 1024 * 1024
V7X_VMEM_BYTES = 64 * MIB
VMEM_LIMIT = 56 * MIB
INPROJ_VMEM_LIMIT = 61 * MIB
assert VMEM_LIMIT < INPROJ_VMEM_LIMIT < V7X_VMEM_BYTES


def _const_spec(shape):
    zeros = (0,) * len(shape)
    return pl.BlockSpec(shape, lambda *_: zeros, pipeline_mode=pl.Buffered(1))


def _layer_spec(stacked, layer):
    return pl.BlockSpec((None,) + stacked.shape[1:], lambda *_: (layer, 0, 0),
                        pipeline_mode=pl.Buffered(1))


def _sigmoid(x):
    return 0.5 * jnp.tanh(0.5 * x) + 0.5


def _inproj_kernel(n_cast, x_ref, g1_ref, w_ref, gq_ref, gk_ref, cw_ref, *refs):
    cast_in, refs = refs[:n_cast], refs[n_cast:]
    q_ref, k_ref, v_ref, bb_ref, edge_ref, sga_ref, sgb_ref = refs[:7]
    cast_out = refs[7:-1]
    z_buf = refs[-1]
    for src, dst in zip(cast_in, cast_out):
        dst[...] = src[...].astype(BF16)

    def weights(c0, width):
        return w_ref[:, c0:c0 + width].astype(BF16)

    c0 = 3 * ATTN_WIDTH
    g0 = c0 + 3 * CONV_WIDTH
    w_first = weights(g0, D_MODEL)
    u_parts = []
    for r0 in range(0, TOK_TILE, NORM_ROWS):
        xf = x_ref[r0:r0 + NORM_ROWS, :]
        ms = jnp.mean(xf * xf, axis=-1, keepdims=True)
        u_part = ((xf * lax.rsqrt(ms + EPS)) * g1_ref[...]).astype(BF16)
        sga_ref[r0:r0 + NORM_ROWS, :] = _sigmoid(
            jnp.dot(u_part, w_first, preferred_element_type=F32)).astype(BF16)
        u_parts.append(u_part)
    u = jnp.concatenate(u_parts, axis=0)

    def proj(c0, width):
        return jnp.dot(u, weights(c0, width), preferred_element_type=F32)

    first_head = lax.broadcasted_iota(jnp.int32, (TOK_TILE, LANES), 1) < HEAD_DIM

    def head_norm(t, g):
        outs = []
        for p in range(HEAD_PAIRS):
            tp = t[:, p * LANES:(p + 1) * LANES]
            t2 = tp * tp
            ss_a = jnp.sum(jnp.where(first_head, t2, 0.0), axis=-1, keepdims=True)
            ss_b = jnp.sum(jnp.where(first_head, 0.0, t2), axis=-1, keepdims=True)
            r_a = lax.rsqrt(ss_a * (1.0 / HEAD_DIM) + EPS)
            r_b = lax.rsqrt(ss_b * (1.0 / HEAD_DIM) + EPS)
            outs.append(tp * jnp.where(first_head, r_a, r_b))
        return jnp.concatenate(outs, axis=-1) * g

    sgb_ref[...] = _sigmoid(proj(g0 + D_MODEL, D_MODEL)).astype(BF16)
    gq = jnp.concatenate([gq_ref[...]] * N_HEADS, axis=1)
    gk = jnp.concatenate([gk_ref[...]] * N_HEADS, axis=1)
    q_ref[...] = head_norm(proj(0, ATTN_WIDTH), gq * (LOG2E * HEAD_DIM ** -0.5)).astype(BF16)
    k_ref[...] = head_norm(proj(ATTN_WIDTH, ATTN_WIDTH), gk).astype(BF16)
    z = proj(c0 + CONV_WIDTH, CONV_WIDTH) * proj(c0 + 2 * CONV_WIDTH, CONV_WIDTH)
    cb = proj(c0, CONV_WIDTH)
    w_prev, w_mid, w_next = cw_ref[0:1, :], cw_ref[1:2, :], cw_ref[2:3, :]
    guard = jnp.zeros((EDGE_ROWS, CONV_WIDTH), F32)
    z_buf[0:EDGE_ROWS, :] = guard
    z_buf[EDGE_ROWS + TOK_TILE:, :] = guard
    z_buf[EDGE_ROWS:EDGE_ROWS + TOK_TILE, :] = z
    z_m1 = z_buf[EDGE_ROWS - 1:EDGE_ROWS - 1 + TOK_TILE, :]
    z_p1 = z_buf[EDGE_ROWS + 1:EDGE_ROWS + 1 + TOK_TILE, :]
    bb_ref[...] = (cb * (z_m1 * w_prev + z * w_mid + z_p1 * w_next)).astype(BF16)
    edge_ref[...] = jnp.concatenate(
        [z[0:1], z[TOK_TILE - 1:TOK_TILE], cb[0:1] * w_prev, cb[TOK_TILE - 1:TOK_TILE] * w_next,
         jnp.zeros((EDGE_ROWS - 4, CONV_WIDTH), F32)], axis=0)
    v_ref[...] = proj(2 * ATTN_WIDTH, ATTN_WIDTH).astype(BF16)


def _inproj(x2d, g1, w_in, gq, gk, cw, layer, later_weights):
    m = x2d.shape[0]
    n_steps = m // TOK_TILE
    tok = lambda w: pl.BlockSpec((TOK_TILE, w), lambda i: (i, 0))
    act_specs = ([tok(ATTN_WIDTH)] * 3 + [tok(CONV_WIDTH)]
                 + [pl.BlockSpec((EDGE_ROWS, CONV_WIDTH), lambda i: (i, 0))] + [tok(D_MODEL)] * 2)
    act_shapes = ([jax.ShapeDtypeStruct((m, ATTN_WIDTH), BF16)] * 3
                  + [jax.ShapeDtypeStruct((m, CONV_WIDTH), BF16),
                     jax.ShapeDtypeStruct((n_steps * EDGE_ROWS, CONV_WIDTH), F32)]
                  + [jax.ShapeDtypeStruct((m, D_MODEL), BF16)] * 2)
    cast_in_specs, cast_out_specs = [], []
    for w in later_weights:
        rows = w.shape[1] // n_steps
        assert rows * n_steps == w.shape[1] and rows % BF16_SUBLANES == 0, w.shape
        cast_in_specs.append(pl.BlockSpec((None, rows, w.shape[2]), lambda i: (layer, i, 0)))
        cast_out_specs.append(pl.BlockSpec((rows, w.shape[2]), lambda i: (i, 0)))
    return pl.pallas_call(
        functools.partial(_inproj_kernel, len(later_weights)),
        grid=(n_steps,),
        in_specs=[tok(D_MODEL), _layer_spec(g1, layer), _layer_spec(w_in, layer),
                  _layer_spec(gq, layer), _layer_spec(gk, layer), _layer_spec(cw, layer)]
        + cast_in_specs,
        out_specs=act_specs + cast_out_specs,
        out_shape=(act_shapes
                   + [jax.ShapeDtypeStruct(w.shape[1:], BF16) for w in later_weights]),
        compiler_params=pltpu.CompilerParams(
            dimension_semantics=("arbitrary",), vmem_limit_bytes=INPROJ_VMEM_LIMIT),
        scratch_shapes=[pltpu.VMEM((TOK_TILE + 2 * EDGE_ROWS, CONV_WIDTH), F32)],
        name="inproj",
    )(x2d, g1, w_in, gq, gk, cw, *later_weights)


def _mixer_kernel(rows_per_batch,
                  q_ref, kw_ref, vw_ref,
                  bb_ref, ep_ref, ec_ref, en_ref, sga_ref, sgb_ref, x_ref,
                  rpb_ref, wa_ref, wb_ref, wo_ref,
                  o_ref, attn_buf, bb_buf, tp_ref):
    blk = pl.program_id(1)
    n_blk = pl.num_programs(1)

    @pl.when(jnp.logical_and(pl.program_id(0) == 0, blk == 0))
    def _():
        _build_bias_table(rpb_ref, tp_ref)

    lane = lax.broadcasted_iota(jnp.int32, (GRID_W, LANES), 1)
    first_head = lane < HEAD_DIM
    keep_first = first_head.astype(BF16)
    keep_second = 1 - keep_first
    n_keys = WIN_ROWS * GRID_W
    ones_block = jnp.ones((n_keys, LANES), BF16)
    win_row0 = jnp.clip(blk * ROWS_PER_BLOCK - WIN_ROWS // 2, 0, rows_per_batch - KV_WINDOW_ROWS)

    def window(j):
        r = blk * ROWS_PER_BLOCK + j
        row_start = jnp.clip(r - WIN_ROWS // 2, 0, rows_per_batch - WIN_ROWS)
        ks = pl.multiple_of((row_start - win_row0) * GRID_W, GRID_W)
        return ks, r - row_start

    def scores(j, p):
        ks, d = window(j)
        cols = slice(p * LANES, (p + 1) * LANES)
        rows = slice(j * GRID_W, (j + 1) * GRID_W)
        q = q_ref[rows, cols]
        q2 = jnp.concatenate([q * keep_first, q * keep_second], axis=0)
        s = lax.dot_general(q2, kw_ref[0, pl.ds(ks, n_keys), cols], (((1,), (1,)), ((), ())),
                            preferred_element_type=F32)
        bias = jnp.concatenate(
            [tp_ref[p, WIN_ROWS - 1 - d + i] for i in range(0, WIN_ROWS, 2)], axis=-1)
        return s + bias

    def attend(j, p, s):
        ks, _ = window(j)
        cols = slice(p * LANES, (p + 1) * LANES)
        rows = slice(j * GRID_W, (j + 1) * GRID_W)
        m = jnp.max(s, axis=-1, keepdims=True)
        e = jnp.exp2(s - m)
        v_ext = jnp.concatenate([vw_ref[0, pl.ds(ks, n_keys), cols], ones_block], axis=1)
        o_ext = jnp.dot(e.astype(BF16), v_ext, preferred_element_type=F32)
        o = o_ext[:, :LANES] / o_ext[:, LANES:]
        attn_buf[rows, cols] = jnp.where(first_head, o[:GRID_W], o[GRID_W:]).astype(BF16)

    blk_in_tile = blk % MIX_PER_TOK_TILE
    at_tile_start = jnp.logical_and(blk_in_tile == 0, blk > 0)
    at_tile_end = jnp.logical_and(blk_in_tile == MIX_PER_TOK_TILE - 1, blk < n_blk - 1)
    fix_first = jnp.where(at_tile_start, ec_ref[2:3, :] * ep_ref[1:2, :], 0.0)
    fix_last = jnp.where(at_tile_end, ec_ref[3:4, :] * en_ref[0:1, :], 0.0)
    sub_row = lax.broadcasted_iota(jnp.int32, (BF16_SUBLANES, CONV_WIDTH), 0)
    top = slice(0, BF16_SUBLANES)
    bottom = slice(MIX_TILE - BF16_SUBLANES, MIX_TILE)
    bb_buf[...] = bb_ref[...]
    bb_buf[top, :] = (bb_ref[top, :].astype(F32)
                      + jnp.where(sub_row == 0, fix_first, 0.0)).astype(BF16)
    bb_buf[bottom, :] = (bb_ref[bottom, :].astype(F32)
                         + jnp.where(sub_row == BF16_SUBLANES - 1, fix_last, 0.0)).astype(BF16)

    pending = [scores(0, p) for p in range(HEAD_PAIRS)]
    for j in range(ROWS_PER_BLOCK):
        for p in range(HEAD_PAIRS):
            s = pending[p]
            if j + 1 < ROWS_PER_BLOCK:
                pending[p] = scores(j + 1, p)
            attend(j, p, s)

    ya = jnp.dot(attn_buf[...], wa_ref[...], preferred_element_type=F32)
    yb = jnp.dot(bb_buf[...], wb_ref[...], preferred_element_type=F32)
    merged = sga_ref[...] * ya.astype(BF16) + sgb_ref[...] * yb.astype(BF16)
    o_ref[...] = x_ref[...] + jnp.dot(merged, wo_ref[...], preferred_element_type=F32)


def _kv_window_start(blk, rows_per_batch):
    row0 = jnp.clip(blk * ROWS_PER_BLOCK - WIN_ROWS // 2, 0, rows_per_batch - KV_WINDOW_ROWS)
    return row0 * GRID_W


def _mixer(batch, seq, q, k, v, bb, edges, sga, sgb, x, rpb, layer, wa, wb, wo):
    rows = seq // GRID_W
    n_blk = rows // ROWS_PER_BLOCK
    tiles_per_seq = seq // TOK_TILE
    n_tiles = batch * tiles_per_seq

    def r3(a):
        return a.reshape(batch, seq, a.shape[-1])

    cur = lambda w: pl.BlockSpec((None, MIX_TILE, w), lambda b, i: (b, i, 0))
    kv_window = pl.BlockSpec(
        (pl.Element(1), pl.Element(KV_WINDOW_ROWS * GRID_W), pl.Element(ATTN_WIDTH)),
        lambda b, i: (b, _kv_window_start(i, rows), 0))
    def edge_spec(offset):
        def index(b, i):
            tile = b * tiles_per_seq + i // MIX_PER_TOK_TILE + offset
            return (jnp.clip(tile, 0, n_tiles - 1), 0)
        return pl.BlockSpec((EDGE_ROWS, CONV_WIDTH), index)

    aw = ATTN_WIDTH
    in_specs = [cur(aw), kv_window, kv_window,
                cur(CONV_WIDTH), edge_spec(-1), edge_spec(0), edge_spec(1),
                cur(D_MODEL), cur(D_MODEL), cur(D_MODEL),
                pl.BlockSpec((None,) + rpb.shape[1:], lambda b, i: (layer, 0, 0, 0),
                             pipeline_mode=pl.Buffered(1)),
                _const_spec(wa.shape),
                _const_spec(wb.shape), _const_spec(wo.shape)]
    out = pl.pallas_call(
        functools.partial(_mixer_kernel, rows),
        grid=(batch, n_blk),
        in_specs=in_specs,
        out_specs=cur(D_MODEL),
        out_shape=jax.ShapeDtypeStruct((batch, seq, D_MODEL), F32),
        scratch_shapes=[pltpu.VMEM((MIX_TILE, ATTN_WIDTH), BF16),
                        pltpu.VMEM((MIX_TILE, CONV_WIDTH), BF16),
                        pltpu.VMEM((HEAD_PAIRS, N_REL_ROWS - 1, 2 * GRID_W, LANES), F32)],
        compiler_params=pltpu.CompilerParams(
            dimension_semantics=("arbitrary", "arbitrary"), vmem_limit_bytes=VMEM_LIMIT),
        name="mixer",
    )(r3(q), r3(k), r3(v), r3(bb), edges, edges, edges, r3(sga), r3(sgb), x,
      rpb, wa, wb, wo)
    return out


def _build_bias_table(rpb_ref, tbl_ref):
    c = lax.broadcasted_iota(jnp.int32, (GRID_W, LANES), 0)
    x = lax.broadcasted_iota(jnp.int32, (GRID_W, LANES), 1)
    kc = x % GRID_W
    win_start = jnp.clip(c - WIN_COLS // 2, 0, GRID_W - WIN_COLS)
    in_window = (kc >= win_start) & (kc < win_start + WIN_COLS)
    left_half = lax.broadcasted_iota(jnp.int32, (1, LANES), 1) < GRID_W
    fill = jnp.zeros((N_REL_ROWS, GRID_W - N_REL_COLS), F32)

    for h in range(N_HEADS):
        rows = slice((h % 2) * GRID_W, (h % 2 + 1) * GRID_W)
        r = rpb_ref[h] * LOG2E
        first = jnp.concatenate([r, fill, jnp.zeros((N_REL_ROWS, GRID_W), F32)], axis=1)
        second = jnp.concatenate([jnp.zeros((N_REL_ROWS, GRID_W), F32), r, fill], axis=1)
        for rr in range(N_REL_ROWS - 1):
            src = jnp.where(left_half, first[rr:rr + 1, :], second[rr + 1:rr + 2, :])
            both = pltpu.roll(jnp.broadcast_to(src, (GRID_W, LANES)), LANES - (WIN_COLS - 1), 1,
                              stride=1, stride_axis=0)
            tbl_ref[h // 2, rr, rows, :] = jnp.where(in_window, both, NEG_INF)


def _mlp_kernel(x_ref, g_ref, w1_ref, w2_ref, o_ref, h_buf):
    def act(h):
        return jnp.square(jnp.maximum(h, 0.0)).astype(BF16)

    def tile(t, carry):
        t0 = pl.multiple_of(t * TOK_TILE, TOK_TILE)
        first = slice(0, FF_CHUNK)
        u_parts = []
        for r0 in range(0, TOK_TILE, NORM_ROWS):
            xf = x_ref[pl.ds(t0 + r0, NORM_ROWS), :]
            ms = jnp.mean(xf * xf, axis=-1, keepdims=True)
            u_part = ((xf * lax.rsqrt(ms + EPS)) * g_ref[...]).astype(BF16)
            h_buf[r0:r0 + NORM_ROWS, first] = act(
                jnp.dot(u_part, w1_ref[:, first], preferred_element_type=F32))
            u_parts.append(u_part)
        u = jnp.concatenate(u_parts, axis=0)
        for c in range(1, D_FF // FF_CHUNK):
            cols = slice(c * FF_CHUNK, (c + 1) * FF_CHUNK)
            h_buf[:, cols] = act(jnp.dot(u, w1_ref[:, cols], preferred_element_type=F32))
        rows = pl.ds(t0, TOK_TILE)
        o_ref[rows, :] = x_ref[rows, :] + jnp.dot(h_buf[...], w2_ref[...],
                                                  preferred_element_type=F32)
        return carry

    lax.fori_loop(0, MLP_TILES_PER_STEP, tile, 0)


def _mlp(x2d, g2, layer, w1, w2):
    m = x2d.shape[0]
    block = MLP_TILES_PER_STEP * TOK_TILE
    tok = pl.BlockSpec((block, D_MODEL), lambda i: (i, 0))
    return pl.pallas_call(
        _mlp_kernel,
        grid=(m // block,),
        in_specs=[tok, _layer_spec(g2, layer), _const_spec(w1.shape), _const_spec(w2.shape)],
        out_specs=tok,
        out_shape=jax.ShapeDtypeStruct((m, D_MODEL), F32),
        scratch_shapes=[pltpu.VMEM((TOK_TILE, D_FF), BF16)],
        compiler_params=pltpu.CompilerParams(
            dimension_semantics=("arbitrary",), vmem_limit_bytes=INPROJ_VMEM_LIMIT),
        name="mlp",
    )(x2d, g2, w1, w2)


def kernel(x, norm1_g, w_in, q_norm_g, k_norm_g, rpb, conv_w, w_attn_branch, w_conv_branch,
           w_o, norm2_g, w_mlp_in, w_mlp_out):
    batch, seq, d_model = x.shape
    depth = w_in.shape[0]
    assert d_model == D_MODEL and w_in.shape[1:] == (D_MODEL, PROJ_WIDTH), (x.shape, w_in.shape)
    assert rpb.shape[1:] == (N_HEADS, N_REL_ROWS, N_REL_COLS), rpb.shape
    assert w_mlp_in.shape[1:] == (D_MODEL, D_FF) and w_mlp_out.shape[1:] == (D_FF, D_MODEL)
    assert seq % TOK_TILE == 0
    assert (seq // GRID_W) % ROWS_PER_BLOCK == 0 and seq // GRID_W >= KV_WINDOW_ROWS
    row3 = lambda a: a.reshape(depth, 1, a.shape[-1])
    for l in range(depth):
        x2d = x.reshape(batch * seq, D_MODEL)
        q, k, v, bb, edges, sga, sgb, wa, wb, wo, w1, w2 = _inproj(
            x2d, row3(norm1_g), w_in, row3(q_norm_g), row3(k_norm_g), conv_w, l,
            (w_attn_branch, w_conv_branch, w_o, w_mlp_in, w_mlp_out))
        x = _mixer(batch, seq, q, k, v, bb, edges, sga, sgb, x, rpb, l, wa, wb, wo)
        x = _mlp(x.reshape(batch * seq, D_MODEL), row3(norm2_g), l, w1, w2
                 ).reshape(batch, seq, D_MODEL)
    return x
```

```python
import functools

import jax
import jax.numpy as jnp
from jax import lax
from jax.experimental import pallas as pl
from jax.experimental.pallas import tpu as pltpu

F32 = jnp.float32
BF16 = jnp.bfloat16

D_MODEL = 1024
GRID_W = 64
N_HEADS = 8
HEAD_DIM = 64
ATTN_WIDTH = N_HEADS * HEAD_DIM
CONV_WIDTH = D_MODEL // 2
WIN_ROWS = 8
WIN_COLS = 16
D_FF = 4 * D_MODEL
EPS = 1e-6
NEG_INF = -1e30
LOG2E = 1.4426950408889634
PROJ_WIDTH = 3 * ATTN_WIDTH + 3 * CONV_WIDTH + 2 * D_MODEL

LANES = 128
HEAD_PAIRS = ATTN_WIDTH // LANES
N_REL_ROWS = 2 * WIN_ROWS - 1
N_REL_COLS = 2 * WIN_COLS - 1

TOK_TILE = 1024
NORM_ROWS = 256
ROWS_PER_BLOCK = 16
MIX_TILE = ROWS_PER_BLOCK * GRID_W
KV_WINDOW_ROWS = ROWS_PER_BLOCK + WIN_ROWS
MIX_PER_TOK_TILE = TOK_TILE // MIX_TILE
assert MIX_PER_TOK_TILE * MIX_TILE == TOK_TILE
FF_CHUNK = 512
BF16_SUBLANES = 16
EDGE_ROWS = 8
MIB = 1024 * 1024
V7X_VMEM_BYTES = 64 * MIB
VMEM_LIMIT = 56 * MIB
INPROJ_VMEM_LIMIT = 61 * MIB
assert VMEM_LIMIT < INPROJ_VMEM_LIMIT < V7X_VMEM_BYTES


def _const_spec(shape):
    zeros = (0,) * len(shape)
    return pl.BlockSpec(shape, lambda *_: zeros, pipeline_mode=pl.Buffered(1))


def _layer_spec(stacked, layer):
    return pl.BlockSpec((None,) + stacked.shape[1:], lambda *_: (layer, 0, 0),
                        pipeline_mode=pl.Buffered(1))


def _sigmoid(x):
    return 0.5 * jnp.tanh(0.5 * x) + 0.5


def _inproj_kernel(n_cast, x_ref, g1_ref, w_ref, gq_ref, gk_ref, cw_ref, *refs):
    cast_in, refs = refs[:n_cast], refs[n_cast:]
    q_ref, k_ref, v_ref, bb_ref, edge_ref, sga_ref, sgb_ref = refs[:7]
    cast_out = refs[7:-1]
    z_buf = refs[-1]
    for src, dst in zip(cast_in, cast_out):
        dst[...] = src[...].astype(BF16)

    def weights(c0, width):
        return w_ref[:, c0:c0 + width].astype(BF16)

    c0 = 3 * ATTN_WIDTH
    g0 = c0 + 3 * CONV_WIDTH
    w_first = weights(g0, D_MODEL)
    u_parts = []
    for r0 in range(0, TOK_TILE, NORM_ROWS):
        xf = x_ref[r0:r0 + NORM_ROWS, :]
        ms = jnp.mean(xf * xf, axis=-1, keepdims=True)
        u_part = ((xf * lax.rsqrt(ms + EPS)) * g1_ref[...]).astype(BF16)
        sga_ref[r0:r0 + NORM_ROWS, :] = _sigmoid(
            jnp.dot(u_part, w_first, preferred_element_type=F32)).astype(BF16)
        u_parts.append(u_part)
    u = jnp.concatenate(u_parts, axis=0)

    def proj(c0, width):
        return jnp.dot(u, weights(c0, width), preferred_element_type=F32)

    first_head = lax.broadcasted_iota(jnp.int32, (TOK_TILE, LANES), 1) < HEAD_DIM

    def head_norm(t, g):
        outs = []
        for p in range(HEAD_PAIRS):
            tp = t[:, p * LANES:(p + 1) * LANES]
            t2 = tp * tp
            ss_a = jnp.sum(jnp.where(first_head, t2, 0.0), axis=-1, keepdims=True)
            ss_b = jnp.sum(jnp.where(first_head, 0.0, t2), axis=-1, keepdims=True)
            r_a = lax.rsqrt(ss_a * (1.0 / HEAD_DIM) + EPS)
            r_b = lax.rsqrt(ss_b * (1.0 / HEAD_DIM) + EPS)
            outs.append(tp * jnp.where(first_head, r_a, r_b))
        return jnp.concatenate(outs, axis=-1) * g

    sgb_ref[...] = _sigmoid(proj(g0 + D_MODEL, D_MODEL)).astype(BF16)
    gq = jnp.concatenate([gq_ref[...]] * N_HEADS, axis=1)
    gk = jnp.concatenate([gk_ref[...]] * N_HEADS, axis=1)
    q_ref[...] = head_norm(proj(0, ATTN_WIDTH), gq * (LOG2E * HEAD_DIM ** -0.5)).astype(BF16)
    k_ref[...] = head_norm(proj(ATTN_WIDTH, ATTN_WIDTH), gk).astype(BF16)
    z = proj(c0 + CONV_WIDTH, CONV_WIDTH) * proj(c0 + 2 * CONV_WIDTH, CONV_WIDTH)
    cb = proj(c0, CONV_WIDTH)
    w_prev, w_mid, w_next = (cw_ref[:, t * CONV_WIDTH:(t + 1) * CONV_WIDTH] for t in range(3))
    guard = jnp.zeros((EDGE_ROWS, CONV_WIDTH), F32)
    z_buf[0:EDGE_ROWS, :] = guard
    z_buf[EDGE_ROWS + TOK_TILE:, :] = guard
    z_buf[EDGE_ROWS:EDGE_ROWS + TOK_TILE, :] = z
    z_m1 = z_buf[EDGE_ROWS - 1:EDGE_ROWS - 1 + TOK_TILE, :]
    z_p1 = z_buf[EDGE_ROWS + 1:EDGE_ROWS + 1 + TOK_TILE, :]
    bb_ref[...] = (cb * (z_m1 * w_prev + z * w_mid + z_p1 * w_next)).astype(BF16)
    edge_ref[...] = jnp.concatenate(
        [z[0:1], z[TOK_TILE - 1:TOK_TILE], cb[0:1] * w_prev, cb[TOK_TILE - 1:TOK_TILE] * w_next,
         jnp.zeros((EDGE_ROWS - 4, CONV_WIDTH), F32)], axis=0)
    v_ref[...] = proj(2 * ATTN_WIDTH, ATTN_WIDTH).astype(BF16)


def _inproj(x2d, g1, w_in, gq, gk, cw, layer, later_weights):
    m = x2d.shape[0]
    n_steps = m // TOK_TILE
    tok = lambda w: pl.BlockSpec((TOK_TILE, w), lambda i: (i, 0))
    act_specs = ([tok(ATTN_WIDTH)] * 3 + [tok(CONV_WIDTH)]
                 + [pl.BlockSpec((EDGE_ROWS, CONV_WIDTH), lambda i: (i, 0))] + [tok(D_MODEL)] * 2)
    act_shapes = ([jax.ShapeDtypeStruct((m, ATTN_WIDTH), BF16)] * 3
                  + [jax.ShapeDtypeStruct((m, CONV_WIDTH), BF16),
                     jax.ShapeDtypeStruct((n_steps * EDGE_ROWS, CONV_WIDTH), F32)]
                  + [jax.ShapeDtypeStruct((m, D_MODEL), BF16)] * 2)
    cast_in_specs, cast_out_specs = [], []
    for w in later_weights:
        rows = w.shape[1] // n_steps
        assert rows * n_steps == w.shape[1] and rows % BF16_SUBLANES == 0, w.shape
        cast_in_specs.append(pl.BlockSpec((None, rows, w.shape[2]), lambda i: (layer, i, 0)))
        cast_out_specs.append(pl.BlockSpec((rows, w.shape[2]), lambda i: (i, 0)))
    return pl.pallas_call(
        functools.partial(_inproj_kernel, len(later_weights)),
        grid=(n_steps,),
        in_specs=[tok(D_MODEL), _layer_spec(g1, layer), _layer_spec(w_in, layer),
                  _layer_spec(gq, layer), _layer_spec(gk, layer), _layer_spec(cw, layer)]
        + cast_in_specs,
        out_specs=act_specs + cast_out_specs,
        out_shape=(act_shapes
                   + [jax.ShapeDtypeStruct(w.shape[1:], BF16) for w in later_weights]),
        compiler_params=pltpu.CompilerParams(
            dimension_semantics=("arbitrary",), vmem_limit_bytes=INPROJ_VMEM_LIMIT),
        scratch_shapes=[pltpu.VMEM((TOK_TILE + 2 * EDGE_ROWS, CONV_WIDTH), F32)],
        name="inproj",
    )(x2d, g1, w_in, gq, gk, cw, *later_weights)


def _mixer_kernel(rows_per_batch,
                  q_ref, kw_ref, vw_ref,
                  bb_ref, ep_ref, ec_ref, en_ref, sga_ref, sgb_ref, x_ref,
                  rpb_ref, wa_ref, wb_ref, wo_ref,
                  o_ref, attn_buf, bb_buf, tp_ref):
    blk = pl.program_id(1)
    n_blk = pl.num_programs(1)

    @pl.when(jnp.logical_and(pl.program_id(0) == 0, blk == 0))
    def _():
        _build_bias_table(rpb_ref, tp_ref)

    lane = lax.broadcasted_iota(jnp.int32, (GRID_W, LANES), 1)
    first_head = lane < HEAD_DIM
    keep_first = first_head.astype(BF16)
    keep_second = 1 - keep_first
    n_keys = WIN_ROWS * GRID_W
    ones_block = jnp.ones((n_keys, LANES), BF16)
    win_row0 = jnp.clip(blk * ROWS_PER_BLOCK - WIN_ROWS // 2, 0, rows_per_batch - KV_WINDOW_ROWS)

    def window(j):
        r = blk * ROWS_PER_BLOCK + j
        row_start = jnp.clip(r - WIN_ROWS // 2, 0, rows_per_batch - WIN_ROWS)
        ks = pl.multiple_of((row_start - win_row0) * GRID_W, GRID_W)
        return ks, r - row_start

    def scores(j, p):
        ks, d = window(j)
        cols = slice(p * LANES, (p + 1) * LANES)
        rows = slice(j * GRID_W, (j + 1) * GRID_W)
        q = q_ref[rows, cols]
        q2 = jnp.concatenate([q * keep_first, q * keep_second], axis=0)
        s = lax.dot_general(q2, kw_ref[0, pl.ds(ks, n_keys), cols], (((1,), (1,)), ((), ())),
                            preferred_element_type=F32)
        bias = jnp.concatenate(
            [tp_ref[p, WIN_ROWS - 1 - d + i] for i in range(0, WIN_ROWS, 2)], axis=-1)
        return s + bias

    def attend(j, p, s):
        ks, _ = window(j)
        cols = slice(p * LANES, (p + 1) * LANES)
        rows = slice(j * GRID_W, (j + 1) * GRID_W)
        m = jnp.max(s, axis=-1, keepdims=True)
        e = jnp.exp2(s - m)
        v_ext = jnp.concatenate([vw_ref[0, pl.ds(ks, n_keys), cols], ones_block], axis=1)
        o_ext = jnp.dot(e.astype(BF16), v_ext, preferred_element_type=F32)
        o = o_ext[:, :LANES] / o_ext[:, LANES:]
        attn_buf[rows, cols] = jnp.where(first_head, o[:GRID_W], o[GRID_W:]).astype(BF16)

    blk_in_tile = blk % MIX_PER_TOK_TILE
    at_tile_start = jnp.logical_and(blk_in_tile == 0, blk > 0)
    at_tile_end = jnp.logical_and(blk_in_tile == MIX_PER_TOK_TILE - 1, blk < n_blk - 1)
    fix_first = jnp.where(at_tile_start, ec_ref[2:3, :] * ep_ref[1:2, :], 0.0)
    fix_last = jnp.where(at_tile_end, ec_ref[3:4, :] * en_ref[0:1, :], 0.0)
    sub_row = lax.broadcasted_iota(jnp.int32, (BF16_SUBLANES, CONV_WIDTH), 0)
    top = slice(0, BF16_SUBLANES)
    bottom = slice(MIX_TILE - BF16_SUBLANES, MIX_TILE)
    bb_buf[...] = bb_ref[...]
    bb_buf[top, :] = (bb_ref[top, :].astype(F32)
                      + jnp.where(sub_row == 0, fix_first, 0.0)).astype(BF16)
    bb_buf[bottom, :] = (bb_ref[bottom, :].astype(F32)
                         + jnp.where(sub_row == BF16_SUBLANES - 1, fix_last, 0.0)).astype(BF16)

    pending = [scores(0, p) for p in range(HEAD_PAIRS)]
    for j in range(ROWS_PER_BLOCK):
        for p in range(HEAD_PAIRS):
            s = pending[p]
            if j + 1 < ROWS_PER_BLOCK:
                pending[p] = scores(j + 1, p)
            attend(j, p, s)

    ya = jnp.dot(attn_buf[...], wa_ref[...], preferred_element_type=F32)
    yb = jnp.dot(bb_buf[...], wb_ref[...], preferred_element_type=F32)
    merged = sga_ref[...] * ya.astype(BF16) + sgb_ref[...] * yb.astype(BF16)
    o_ref[...] = x_ref[...] + jnp.dot(merged, wo_ref[...], preferred_element_type=F32)


def _kv_window_start(blk, rows_per_batch):
    row0 = jnp.clip(blk * ROWS_PER_BLOCK - WIN_ROWS // 2, 0, rows_per_batch - KV_WINDOW_ROWS)
    return row0 * GRID_W


def _mixer(batch, seq, q, k, v, bb, edges, sga, sgb, x, rpb, layer, wa, wb, wo):
    rows = seq // GRID_W
    n_blk = rows // ROWS_PER_BLOCK
    tiles_per_seq = seq // TOK_TILE
    n_tiles = batch * tiles_per_seq

    def r3(a):
        return a.reshape(batch, seq, a.shape[-1])

    cur = lambda w: pl.BlockSpec((None, MIX_TILE, w), lambda b, i: (b, i, 0))
    kv_window = pl.BlockSpec(
        (pl.Element(1), pl.Element(KV_WINDOW_ROWS * GRID_W), pl.Element(ATTN_WIDTH)),
        lambda b, i: (b, _kv_window_start(i, rows), 0))
    def edge_spec(offset):
        def index(b, i):
            tile = b * tiles_per_seq + i // MIX_PER_TOK_TILE + offset
            return (jnp.clip(tile, 0, n_tiles - 1), 0)
        return pl.BlockSpec((EDGE_ROWS, CONV_WIDTH), index)

    aw = ATTN_WIDTH
    in_specs = [cur(aw), kv_window, kv_window,
                cur(CONV_WIDTH), edge_spec(-1), edge_spec(0), edge_spec(1),
                cur(D_MODEL), cur(D_MODEL), cur(D_MODEL),
                pl.BlockSpec((None,) + rpb.shape[1:], lambda b, i: (layer, 0, 0, 0),
                             pipeline_mode=pl.Buffered(1)),
                _const_spec(wa.shape),
                _const_spec(wb.shape), _const_spec(wo.shape)]
    out = pl.pallas_call(
        functools.partial(_mixer_kernel, rows),
        grid=(batch, n_blk),
        in_specs=in_specs,
        out_specs=cur(D_MODEL),
        out_shape=jax.ShapeDtypeStruct((batch, seq, D_MODEL), F32),
        scratch_shapes=[pltpu.VMEM((MIX_TILE, ATTN_WIDTH), BF16),
                        pltpu.VMEM((MIX_TILE, CONV_WIDTH), BF16),
                        pltpu.VMEM((HEAD_PAIRS, N_REL_ROWS - 1, 2 * GRID_W, LANES), F32)],
        compiler_params=pltpu.CompilerParams(
            dimension_semantics=("arbitrary", "arbitrary"), vmem_limit_bytes=VMEM_LIMIT),
        name="mixer",
    )(r3(q), r3(k), r3(v), r3(bb), edges, edges, edges, r3(sga), r3(sgb), x,
      rpb, wa, wb, wo)
    return out


def _build_bias_table(rpb_ref, tbl_ref):
    c = lax.broadcasted_iota(jnp.int32, (GRID_W, LANES), 0)
    x = lax.broadcasted_iota(jnp.int32, (GRID_W, LANES), 1)
    kc = x % GRID_W
    win_start = jnp.clip(c - WIN_COLS // 2, 0, GRID_W - WIN_COLS)
    in_window = (kc >= win_start) & (kc < win_start + WIN_COLS)
    left_half = lax.broadcasted_iota(jnp.int32, (1, LANES), 1) < GRID_W
    fill = jnp.zeros((N_REL_ROWS, GRID_W - N_REL_COLS), F32)

    for h in range(N_HEADS):
        rows = slice((h % 2) * GRID_W, (h % 2 + 1) * GRID_W)
        r = rpb_ref[h] * LOG2E
        first = jnp.concatenate([r, fill, jnp.zeros((N_REL_ROWS, GRID_W), F32)], axis=1)
        second = jnp.concatenate([jnp.zeros((N_REL_ROWS, GRID_W), F32), r, fill], axis=1)
        for rr in range(N_REL_ROWS - 1):
            src = jnp.where(left_half, first[rr:rr + 1, :], second[rr + 1:rr + 2, :])
            both = pltpu.roll(jnp.broadcast_to(src, (GRID_W, LANES)), LANES - (WIN_COLS - 1), 1,
                              stride=1, stride_axis=0)
            tbl_ref[h // 2, rr, rows, :] = jnp.where(in_window, both, NEG_INF)


def _mlp_kernel(x_ref, g_ref, w1_ref, w2_ref, o_ref, h_buf):
    def act(h):
        return jnp.square(jnp.maximum(h, 0.0)).astype(BF16)

    first = slice(0, FF_CHUNK)
    u_parts = []
    for r0 in range(0, TOK_TILE, NORM_ROWS):
        xf = x_ref[r0:r0 + NORM_ROWS, :]
        ms = jnp.mean(xf * xf, axis=-1, keepdims=True)
        u_part = ((xf * lax.rsqrt(ms + EPS)) * g_ref[...]).astype(BF16)
        h_buf[r0:r0 + NORM_ROWS, first] = act(
            jnp.dot(u_part, w1_ref[:, first], preferred_element_type=F32))
        u_parts.append(u_part)
    u = jnp.concatenate(u_parts, axis=0)
    for c in range(1, D_FF // FF_CHUNK):
        cols = slice(c * FF_CHUNK, (c + 1) * FF_CHUNK)
        h_buf[:, cols] = act(jnp.dot(u, w1_ref[:, cols], preferred_element_type=F32))
    o_ref[...] = x_ref[...] + jnp.dot(h_buf[...], w2_ref[...], preferred_element_type=F32)


def _mlp(x2d, g2, layer, w1, w2):
    m = x2d.shape[0]
    tok = pl.BlockSpec((TOK_TILE, D_MODEL), lambda i: (i, 0))
    return pl.pallas_call(
        _mlp_kernel,
        grid=(m // TOK_TILE,),
        in_specs=[tok, _layer_spec(g2, layer), _const_spec(w1.shape), _const_spec(w2.shape)],
        out_specs=tok,
        out_shape=jax.ShapeDtypeStruct((m, D_MODEL), F32),
        scratch_shapes=[pltpu.VMEM((TOK_TILE, D_FF), BF16)],
        compiler_params=pltpu.CompilerParams(
            dimension_semantics=("arbitrary",), vmem_limit_bytes=VMEM_LIMIT),
        name="mlp",
    )(x2d, g2, w1, w2)


def kernel(x, norm1_g, w_in, q_norm_g, k_norm_g, rpb, conv_w, w_attn_branch, w_conv_branch,
           w_o, norm2_g, w_mlp_in, w_mlp_out):
    batch, seq, d_model = x.shape
    depth = w_in.shape[0]
    assert d_model == D_MODEL and w_in.shape[1:] == (D_MODEL, PROJ_WIDTH), (x.shape, w_in.shape)
    assert rpb.shape[1:] == (N_HEADS, N_REL_ROWS, N_REL_COLS), rpb.shape
    assert w_mlp_in.shape[1:] == (D_MODEL, D_FF) and w_mlp_out.shape[1:] == (D_FF, D_MODEL)
    assert seq % TOK_TILE == 0
    assert (seq // GRID_W) % ROWS_PER_BLOCK == 0 and seq // GRID_W >= KV_WINDOW_ROWS
    row3 = lambda a: a.reshape(depth, 1, a.shape[-1])
    for l in range(depth):
        x2d = x.reshape(batch * seq, D_MODEL)
        q, k, v, bb, edges, sga, sgb, wa, wb, wo, w1, w2 = _inproj(
            x2d, row3(norm1_g), w_in, row3(q_norm_g), row3(k_norm_g),
            conv_w.reshape(depth, 1, 3 * CONV_WIDTH), l,
            (w_attn_branch, w_conv_branch, w_o, w_mlp_in, w_mlp_out))
        x = _mixer(batch, seq, q, k, v, bb, edges, sga, sgb, x, rpb, l, wa, wb, wo)
        x = _mlp(x.reshape(batch * seq, D_MODEL), row3(norm2_g), l, w1, w2
                 ).reshape(batch, seq, D_MODEL)
    return x
```

```python
import functools

import jax
import jax.numpy as jnp
from jax import lax
from jax.experimental import pallas as pl
from jax.experimental.pallas import tpu as pltpu

F32 = jnp.float32
BF16 = jnp.bfloat16

D_MODEL = 1024
GRID_W = 64
N_HEADS = 8
HEAD_DIM = 64
ATTN_WIDTH = N_HEADS * HEAD_DIM
CONV_WIDTH = D_MODEL // 2
WIN_ROWS = 8
WIN_COLS = 16
D_FF = 4 * D_MODEL
EPS = 1e-6
NEG_INF = -1e30
LOG2E = 1.4426950408889634
PROJ_WIDTH = 3 * ATTN_WIDTH + 3 * CONV_WIDTH + 2 * D_MODEL

LANES = 128
HEAD_PAIRS = ATTN_WIDTH // LANES
N_REL_ROWS = 2 * WIN_ROWS - 1
N_REL_COLS = 2 * WIN_COLS - 1

TOK_TILE = 1024
NORM_ROWS = 256
ROWS_PER_BLOCK = 16
MIX_TILE = ROWS_PER_BLOCK * GRID_W
KV_WINDOW_ROWS = ROWS_PER_BLOCK + WIN_ROWS
MIX_PER_TOK_TILE = TOK_TILE // MIX_TILE
assert MIX_PER_TOK_TILE * MIX_TILE == TOK_TILE
FF_CHUNK = 512
BF16_SUBLANES = 16
EDGE_ROWS = 8
MIB = 1024 * 1024
V7X_VMEM_BYTES = 64 * MIB
VMEM_LIMIT = 56 * MIB
INPROJ_VMEM_LIMIT = 61 * MIB
assert VMEM_LIMIT < INPROJ_VMEM_LIMIT < V7X_VMEM_BYTES


def _const_spec(shape):
    zeros = (0,) * len(shape)
    return pl.BlockSpec(shape, lambda *_: zeros, pipeline_mode=pl.Buffered(1))


def _layer_spec(stacked, layer):
    return pl.BlockSpec((None,) + stacked.shape[1:], lambda *_: (layer, 0, 0),
                        pipeline_mode=pl.Buffered(1))


def _sigmoid(x):
    return 0.5 * jnp.tanh(0.5 * x) + 0.5


def _inproj_kernel(n_cast, x_ref, g1_ref, w_ref, gq_ref, gk_ref, cw_ref, *refs):
    cast_in, refs = refs[:n_cast], refs[n_cast:]
    q_ref, k_ref, v_ref, bb_ref, edge_ref, sga_ref, sgb_ref = refs[:7]
    cast_out = refs[7:-1]
    z_buf = refs[-1]
    for src, dst in zip(cast_in, cast_out):
        dst[...] = src[...].astype(BF16)

    def weights(c0, width):
        return w_ref[:, c0:c0 + width].astype(BF16)

    c0 = 3 * ATTN_WIDTH
    g0 = c0 + 3 * CONV_WIDTH
    w_first = weights(g0, D_MODEL)
    u_parts = []
    for r0 in range(0, TOK_TILE, NORM_ROWS):
        xf = x_ref[r0:r0 + NORM_ROWS, :]
        ms = jnp.mean(xf * xf, axis=-1, keepdims=True)
        u_part = ((xf * lax.rsqrt(ms + EPS)) * g1_ref[...]).astype(BF16)
        sga_ref[r0:r0 + NORM_ROWS, :] = _sigmoid(
            jnp.dot(u_part, w_first, preferred_element_type=F32)).astype(BF16)
        u_parts.append(u_part)
    u = jnp.concatenate(u_parts, axis=0)

    def proj(c0, width):
        return jnp.dot(u, weights(c0, width), preferred_element_type=F32)

    first_head = lax.broadcasted_iota(jnp.int32, (TOK_TILE, LANES), 1) < HEAD_DIM

    def head_norm(t, g):
        outs = []
        for p in range(HEAD_PAIRS):
            tp = t[:, p * LANES:(p + 1) * LANES]
            t2 = tp * tp
            ss_a = jnp.sum(jnp.where(first_head, t2, 0.0), axis=-1, keepdims=True)
            ss_b = jnp.sum(jnp.where(first_head, 0.0, t2), axis=-1, keepdims=True)
            r_a = lax.rsqrt(ss_a * (1.0 / HEAD_DIM) + EPS)
            r_b = lax.rsqrt(ss_b * (1.0 / HEAD_DIM) + EPS)
            outs.append(tp * jnp.where(first_head, r_a, r_b))
        return jnp.concatenate(outs, axis=-1) * g

    sgb_ref[...] = _sigmoid(proj(g0 + D_MODEL, D_MODEL)).astype(BF16)
    gq = jnp.concatenate([gq_ref[...]] * N_HEADS, axis=1)
    gk = jnp.concatenate([gk_ref[...]] * N_HEADS, axis=1)
    q_ref[...] = head_norm(proj(0, ATTN_WIDTH), gq * (LOG2E * HEAD_DIM ** -0.5)).astype(BF16)
    k_ref[...] = head_norm(proj(ATTN_WIDTH, ATTN_WIDTH), gk).astype(BF16)
    z = proj(c0 + CONV_WIDTH, CONV_WIDTH) * proj(c0 + 2 * CONV_WIDTH, CONV_WIDTH)
    cb = proj(c0, CONV_WIDTH)
    w_prev, w_mid, w_next = (cw_ref[:, t * CONV_WIDTH:(t + 1) * CONV_WIDTH] for t in range(3))
    guard = jnp.zeros((EDGE_ROWS, CONV_WIDTH), F32)
    z_buf[0:EDGE_ROWS, :] = guard
    z_buf[EDGE_ROWS + TOK_TILE:, :] = guard
    z_buf[EDGE_ROWS:EDGE_ROWS + TOK_TILE, :] = z
    z_m1 = z_buf[EDGE_ROWS - 1:EDGE_ROWS - 1 + TOK_TILE, :]
    z_p1 = z_buf[EDGE_ROWS + 1:EDGE_ROWS + 1 + TOK_TILE, :]
    bb_ref[...] = (cb * (z_m1 * w_prev + z * w_mid + z_p1 * w_next)).astype(BF16)
    edge_ref[...] = jnp.concatenate(
        [z[0:1], z[TOK_TILE - 1:TOK_TILE], cb[0:1] * w_prev, cb[TOK_TILE - 1:TOK_TILE] * w_next,
         jnp.zeros((EDGE_ROWS - 4, CONV_WIDTH), F32)], axis=0)
    v_ref[...] = proj(2 * ATTN_WIDTH, ATTN_WIDTH).astype(BF16)


def _inproj(x2d, g1, w_in, gq, gk, cw, layer, later_weights):
    m = x2d.shape[0]
    n_steps = m // TOK_TILE
    tok = lambda w: pl.BlockSpec((TOK_TILE, w), lambda i: (i, 0))
    act_specs = ([tok(ATTN_WIDTH)] * 3 + [tok(CONV_WIDTH)]
                 + [pl.BlockSpec((EDGE_ROWS, CONV_WIDTH), lambda i: (i, 0))] + [tok(D_MODEL)] * 2)
    act_shapes = ([jax.ShapeDtypeStruct((m, ATTN_WIDTH), BF16)] * 3
                  + [jax.ShapeDtypeStruct((m, CONV_WIDTH), BF16),
                     jax.ShapeDtypeStruct((n_steps * EDGE_ROWS, CONV_WIDTH), F32)]
                  + [jax.ShapeDtypeStruct((m, D_MODEL), BF16)] * 2)
    cast_in_specs, cast_out_specs = [], []
    for w in later_weights:
        rows = w.shape[1] // n_steps
        assert rows * n_steps == w.shape[1] and rows % BF16_SUBLANES == 0, w.shape
        cast_in_specs.append(pl.BlockSpec((None, rows, w.shape[2]), lambda i: (layer, i, 0)))
        cast_out_specs.append(pl.BlockSpec((rows, w.shape[2]), lambda i: (i, 0)))
    return pl.pallas_call(
        functools.partial(_inproj_kernel, len(later_weights)),
        grid=(n_steps,),
        in_specs=[tok(D_MODEL), _layer_spec(g1, layer), _layer_spec(w_in, layer),
                  _layer_spec(gq, layer), _layer_spec(gk, layer), _layer_spec(cw, layer)]
        + cast_in_specs,
        out_specs=act_specs + cast_out_specs,
        out_shape=(act_shapes
                   + [jax.ShapeDtypeStruct(w.shape[1:], BF16) for w in later_weights]),
        compiler_params=pltpu.CompilerParams(
            dimension_semantics=("arbitrary",), vmem_limit_bytes=INPROJ_VMEM_LIMIT),
        scratch_shapes=[pltpu.VMEM((TOK_TILE + 2 * EDGE_ROWS, CONV_WIDTH), F32)],
        name="inproj",
    )(x2d, g1, w_in, gq, gk, cw, *later_weights)


def _mixer_kernel(rows_per_batch,
                  q_ref, kw_ref, vw_ref,
                  bb_ref, ep_ref, ec_ref, en_ref, sga_ref, sgb_ref, x_ref,
                  rpb_ref, wa_ref, wb_ref, wo_ref,
                  o_ref, attn_buf, bb_buf, tp_ref):
    blk = pl.program_id(1)
    n_blk = pl.num_programs(1)

    @pl.when(jnp.logical_and(pl.program_id(0) == 0, blk == 0))
    def _():
        _build_bias_table(rpb_ref, tp_ref)

    lane = lax.broadcasted_iota(jnp.int32, (GRID_W, LANES), 1)
    first_head = lane < HEAD_DIM
    keep_first = first_head.astype(BF16)
    keep_second = 1 - keep_first
    n_keys = WIN_ROWS * GRID_W
    ones_block = jnp.ones((n_keys, LANES), BF16)
    win_row0 = jnp.clip(blk * ROWS_PER_BLOCK - WIN_ROWS // 2, 0, rows_per_batch - KV_WINDOW_ROWS)

    def window(j):
        r = blk * ROWS_PER_BLOCK + j
        row_start = jnp.clip(r - WIN_ROWS // 2, 0, rows_per_batch - WIN_ROWS)
        ks = pl.multiple_of((row_start - win_row0) * GRID_W, GRID_W)
        return ks, r - row_start

    def scores(j, p):
        ks, d = window(j)
        cols = slice(p * LANES, (p + 1) * LANES)
        rows = slice(j * GRID_W, (j + 1) * GRID_W)
        q = q_ref[rows, cols]
        q2 = jnp.concatenate([q * keep_first, q * keep_second], axis=0)
        s = lax.dot_general(q2, kw_ref[0, pl.ds(ks, n_keys), cols], (((1,), (1,)), ((), ())),
                            preferred_element_type=F32)
        bias = jnp.concatenate(
            [tp_ref[p, WIN_ROWS - 1 - d + i] for i in range(0, WIN_ROWS, 2)], axis=-1)
        return s + bias

    def attend(j, p, s):
        ks, _ = window(j)
        cols = slice(p * LANES, (p + 1) * LANES)
        rows = slice(j * GRID_W, (j + 1) * GRID_W)
        m = jnp.max(s, axis=-1, keepdims=True)
        e = jnp.exp2(s - m)
        v_ext = jnp.concatenate([vw_ref[0, pl.ds(ks, n_keys), cols], ones_block], axis=1)
        o_ext = jnp.dot(e.astype(BF16), v_ext, preferred_element_type=F32)
        o = o_ext[:, :LANES] / o_ext[:, LANES:]
        attn_buf[rows, cols] = jnp.where(first_head, o[:GRID_W], o[GRID_W:]).astype(BF16)

    blk_in_tile = blk % MIX_PER_TOK_TILE
    at_tile_start = jnp.logical_and(blk_in_tile == 0, blk > 0)
    at_tile_end = jnp.logical_and(blk_in_tile == MIX_PER_TOK_TILE - 1, blk < n_blk - 1)
    fix_first = jnp.where(at_tile_start, ec_ref[2:3, :] * ep_ref[1:2, :], 0.0)
    fix_last = jnp.where(at_tile_end, ec_ref[3:4, :] * en_ref[0:1, :], 0.0)
    sub_row = lax.broadcasted_iota(jnp.int32, (BF16_SUBLANES, CONV_WIDTH), 0)
    top = slice(0, BF16_SUBLANES)
    bottom = slice(MIX_TILE - BF16_SUBLANES, MIX_TILE)
    bb_buf[...] = bb_ref[...]
    bb_buf[top, :] = (bb_ref[top, :].astype(F32)
                      + jnp.where(sub_row == 0, fix_first, 0.0)).astype(BF16)
    bb_buf[bottom, :] = (bb_ref[bottom, :].astype(F32)
                         + jnp.where(sub_row == BF16_SUBLANES - 1, fix_last, 0.0)).astype(BF16)

    pending = [scores(0, p) for p in range(HEAD_PAIRS)]
    for j in range(ROWS_PER_BLOCK):
        for p in range(HEAD_PAIRS):
            s = pending[p]
            if j + 1 < ROWS_PER_BLOCK:
                pending[p] = scores(j + 1, p)
            attend(j, p, s)

    ya = jnp.dot(attn_buf[...], wa_ref[...], preferred_element_type=F32)
    yb = jnp.dot(bb_buf[...], wb_ref[...], preferred_element_type=F32)
    merged = sga_ref[...] * ya.astype(BF16) + sgb_ref[...] * yb.astype(BF16)
    o_ref[...] = x_ref[...] + jnp.dot(merged, wo_ref[...], preferred_element_type=F32)


def _kv_window_start(blk, rows_per_batch):
    row0 = jnp.clip(blk * ROWS_PER_BLOCK - WIN_ROWS // 2, 0, rows_per_batch - KV_WINDOW_ROWS)
    return row0 * GRID_W


def _mixer(batch, seq, q, k, v, bb, edges, sga, sgb, x, rpb, layer, wa, wb, wo):
    rows = seq // GRID_W
    n_blk = rows // ROWS_PER_BLOCK
    tiles_per_seq = seq // TOK_TILE
    n_tiles = batch * tiles_per_seq

    def r3(a):
        return a.reshape(batch, seq, a.shape[-1])

    cur = lambda w: pl.BlockSpec((None, MIX_TILE, w), lambda b, i: (b, i, 0))
    kv_window = pl.BlockSpec(
        (pl.Element(1), pl.Element(KV_WINDOW_ROWS * GRID_W), pl.Element(ATTN_WIDTH)),
        lambda b, i: (b, _kv_window_start(i, rows), 0))
    def edge_spec(offset):
        def index(b, i):
            tile = b * tiles_per_seq + i // MIX_PER_TOK_TILE + offset
            return (jnp.clip(tile, 0, n_tiles - 1), 0)
        return pl.BlockSpec((EDGE_ROWS, CONV_WIDTH), index)

    aw = ATTN_WIDTH
    in_specs = [cur(aw), kv_window, kv_window,
                cur(CONV_WIDTH), edge_spec(-1), edge_spec(0), edge_spec(1),
                cur(D_MODEL), cur(D_MODEL), cur(D_MODEL),
                pl.BlockSpec((None,) + rpb.shape[1:], lambda b, i: (layer, 0, 0, 0),
                             pipeline_mode=pl.Buffered(1)),
                _const_spec(wa.shape),
                _const_spec(wb.shape), _const_spec(wo.shape)]
    out = pl.pallas_call(
        functools.partial(_mixer_kernel, rows),
        grid=(batch, n_blk),
        in_specs=in_specs,
        out_specs=cur(D_MODEL),
        out_shape=jax.ShapeDtypeStruct((batch, seq, D_MODEL), F32),
        scratch_shapes=[pltpu.VMEM((MIX_TILE, ATTN_WIDTH), BF16),
                        pltpu.VMEM((MIX_TILE, CONV_WIDTH), BF16),
                        pltpu.VMEM((HEAD_PAIRS, N_REL_ROWS - 1, 2 * GRID_W, LANES), F32)],
        compiler_params=pltpu.CompilerParams(
            dimension_semantics=("arbitrary", "arbitrary"), vmem_limit_bytes=VMEM_LIMIT),
        name="mixer",
    )(r3(q), r3(k), r3(v), r3(bb), edges, edges, edges, r3(sga), r3(sgb), x,
      rpb, wa, wb, wo)
    return out


def _build_bias_table(rpb_ref, tbl_ref):
    c = lax.broadcasted_iota(jnp.int32, (GRID_W, LANES), 0)
    x = lax.broadcasted_iota(jnp.int32, (GRID_W, LANES), 1)
    kc = x % GRID_W
    win_start = jnp.clip(c - WIN_COLS // 2, 0, GRID_W - WIN_COLS)
    in_window = (kc >= win_start) & (kc < win_start + WIN_COLS)
    left_half = lax.broadcasted_iota(jnp.int32, (1, LANES), 1) < GRID_W
    fill = jnp.zeros((N_REL_ROWS, GRID_W - N_REL_COLS), F32)

    for h in range(N_HEADS):
        rows = slice((h % 2) * GRID_W, (h % 2 + 1) * GRID_W)
        r = rpb_ref[h] * LOG2E
        first = jnp.concatenate([r, fill, jnp.zeros((N_REL_ROWS, GRID_W), F32)], axis=1)
        second = jnp.concatenate([jnp.zeros((N_REL_ROWS, GRID_W), F32), r, fill], axis=1)
        for rr in range(N_REL_ROWS - 1):
            src = jnp.where(left_half, first[rr:rr + 1, :], second[rr + 1:rr + 2, :])
            both = pltpu.roll(jnp.broadcast_to(src, (GRID_W, LANES)), LANES - (WIN_COLS - 1), 1,
                              stride=1, stride_axis=0)
            tbl_ref[h // 2, rr, rows, :] = jnp.where(in_window, both, NEG_INF)


def _mlp_kernel(x_ref, g_ref, w1_ref, w2_ref, o_ref, h_buf):
    def act(h):
        return jnp.square(jnp.maximum(h, 0.0)).astype(BF16)

    first = slice(0, FF_CHUNK)
    u_parts = []
    for r0 in range(0, TOK_TILE, NORM_ROWS):
        xf = x_ref[r0:r0 + NORM_ROWS, :]
        ms = jnp.mean(xf * xf, axis=-1, keepdims=True)
        u_part = ((xf * lax.rsqrt(ms + EPS)) * g_ref[...]).astype(BF16)
        h_buf[r0:r0 + NORM_ROWS, first] = act(
            jnp.dot(u_part, w1_ref[:, first], preferred_element_type=F32))
        u_parts.append(u_part)
    u = jnp.concatenate(u_parts, axis=0)
    for c in range(1, D_FF // FF_CHUNK):
        cols = slice(c * FF_CHUNK, (c + 1) * FF_CHUNK)
        h_buf[:, cols] = act(jnp.dot(u, w1_ref[:, cols], preferred_element_type=F32))
    o_ref[...] = x_ref[...] + jnp.dot(h_buf[...], w2_ref[...], preferred_element_type=F32)


def _mlp_streamed_kernel(layer, x_hbm, g_ref, w1_ref, w2_ref, o_hbm, h_buf):
    tok = pl.BlockSpec((TOK_TILE, D_MODEL), lambda i: (i, 0))
    pltpu.emit_pipeline(
        lambda x_ref, o_ref: _mlp_kernel(x_ref, g_ref.at[layer], w1_ref, w2_ref, o_ref, h_buf),
        grid=(x_hbm.shape[0] // TOK_TILE,), in_specs=[tok], out_specs=[tok],
    )(x_hbm, o_hbm)


def _mlp(x2d, g2, layer, w1, w2):
    m = x2d.shape[0]
    whole = pl.BlockSpec(memory_space=pltpu.VMEM)
    return pl.pallas_call(
        functools.partial(_mlp_streamed_kernel, layer),
        in_specs=[pl.BlockSpec(memory_space=pl.ANY), whole, whole, whole],
        out_specs=pl.BlockSpec(memory_space=pl.ANY),
        out_shape=jax.ShapeDtypeStruct((m, D_MODEL), F32),
        scratch_shapes=[pltpu.VMEM((TOK_TILE, D_FF), BF16)],
        compiler_params=pltpu.CompilerParams(vmem_limit_bytes=VMEM_LIMIT),
        name="mlp",
    )(x2d, g2, w1, w2)


def kernel(x, norm1_g, w_in, q_norm_g, k_norm_g, rpb, conv_w, w_attn_branch, w_conv_branch,
           w_o, norm2_g, w_mlp_in, w_mlp_out):
    batch, seq, d_model = x.shape
    depth = w_in.shape[0]
    assert d_model == D_MODEL and w_in.shape[1:] == (D_MODEL, PROJ_WIDTH), (x.shape, w_in.shape)
    assert rpb.shape[1:] == (N_HEADS, N_REL_ROWS, N_REL_COLS), rpb.shape
    assert w_mlp_in.shape[1:] == (D_MODEL, D_FF) and w_mlp_out.shape[1:] == (D_FF, D_MODEL)
    assert seq % TOK_TILE == 0
    assert (seq // GRID_W) % ROWS_PER_BLOCK == 0 and seq // GRID_W >= KV_WINDOW_ROWS
    row3 = lambda a: a.reshape(depth, 1, a.shape[-1])
    for l in range(depth):
        x2d = x.reshape(batch * seq, D_MODEL)
        q, k, v, bb, edges, sga, sgb, wa, wb, wo, w1, w2 = _inproj(
            x2d, row3(norm1_g), w_in, row3(q_norm_g), row3(k_norm_g),
            conv_w.reshape(depth, 1, 3 * CONV_WIDTH), l,
            (w_attn_branch, w_conv_branch, w_o, w_mlp_in, w_mlp_out))
        x = _mixer(batch, seq, q, k, v, bb, edges, sga, sgb, x, rpb, l, wa, wb, wo)
        x = _mlp(x.reshape(batch * seq, D_MODEL), row3(norm2_g), l, w1, w2
                 ).reshape(batch, seq, D_MODEL)
    return x
```
